```python
import math
import jax
import jax.numpy as jnp
from jax import lax
import numpy as np

D_MODEL = 1024
BATCH = 8
SEQ = 4096
DEPTH = 2
DEC_BATCH = 8
DEC_SEQ = 2048
PAST_LEN = 128

GRID_W = 64
EPS = 1e-6

D_A = D_MODEL // 2
S5_GROUP = 16
S5_GROUPS = D_A // S5_GROUP
S5_STATE = 64
NA_HEADS = 8
D_B = D_MODEL // 2
NA_HEAD_DIM = D_B // NA_HEADS
NA_WIN_R = 8
NA_WIN_C = 16
NA_QBLK_C = 16
NA_KBLK_C = 32
D_IN_EVEN = D_A + 3 * D_B
D_MIX_EVEN = D_A + D_B

RET_HEADS = 8
RET_DK = D_MODEL // RET_HEADS
RET_DV = 2 * RET_DK
D_RET_K = RET_HEADS * RET_DK
D_RET_V = RET_HEADS * RET_DV
D_IN_ODD = 2 * D_RET_K + 2 * D_RET_V
RET_CHUNK = 128
ROPE_BASE = 10000.0

MOE_GROUPS = 4
MOE_PER_GROUP = 4
MOE_EXPERTS = MOE_GROUPS * MOE_PER_GROUP
MOE_TOPK = 2
MOE_FF = 512

N_EVEN = (DEPTH + 1) // 2
N_ODD = DEPTH // 2

kernel_name = 'hybrid_s5_natten_retention_hmoe_encoder'

F32 = jnp.float32


def rmsnorm(x, gain):
    xf = x.astype(F32)
    xf = xf * lax.rsqrt(jnp.mean(xf * xf, axis=-1, keepdims=True) + EPS)
    return (xf * gain.astype(F32)).astype(x.dtype)


def _linear_recurrence(e1, e2):
    a1, b1 = e1
    a2, b2 = e2
    return a1 * a2, a2 * b1 + b2


def s5_direction(u, lam_re, lam_im, log_dt, b_re, b_im, c_re, c_im, reverse):
    lam = lax.complex(lam_re.astype(F32), lam_im.astype(F32))
    dt = jnp.exp(log_dt.astype(F32))[:, None]
    a_bar = jnp.exp(lam * dt)
    b = lax.complex(b_re.astype(F32), b_im.astype(F32))
    b_bar = ((a_bar - 1.0) / lam)[..., None] * b
    bu = jnp.einsum('gph,btgh->btgp', b_bar, u.astype(jnp.complex64))
    a = jnp.broadcast_to(a_bar, bu.shape)
    _, states = lax.associative_scan(_linear_recurrence, (a, bu), reverse=reverse, axis=1)
    c = lax.complex(c_re.astype(F32), c_im.astype(F32))
    return jnp.einsum('ghp,btgp->btgh', c, states).real


def neighbourhood_attention(q, k, v, q_gain, k_gain, rpb):
    bsz, t, n_heads, dh = q.shape
    rows = t // GRID_W
    wr = min(NA_WIN_R, rows)
    n_qb = GRID_W // NA_QBLK_C
    qn = rmsnorm(q.astype(F32), q_gain)
    kn = rmsnorm(k.astype(F32), k_gain)
    qg = qn.reshape(bsz, rows, n_qb, NA_QBLK_C, n_heads, dh)
    kg = kn.reshape(bsz, rows, GRID_W, n_heads, dh)
    vg = v.astype(F32).reshape(bsz, rows, GRID_W, n_heads, dh)
    r = jnp.arange(rows)
    row_start = jnp.clip(r - wr // 2, 0, rows - wr)
    key_rows = row_start[:, None] + jnp.arange(wr)[None, :]
    q_col = jnp.arange(GRID_W).reshape(n_qb, NA_QBLK_C)
    k_col0 = jnp.clip(jnp.arange(n_qb) * NA_QBLK_C - NA_WIN_C // 2, 0, GRID_W - NA_KBLK_C)
    k_col = k_col0[:, None] + jnp.arange(NA_KBLK_C)[None, :]
    kb = jnp.take(jnp.take(kg, key_rows, axis=1), k_col, axis=3)
    vb = jnp.take(jnp.take(vg, key_rows, axis=1), k_col, axis=3)
    scores = jnp.einsum('brcihd,brwcjhd->bhrciwj', qg, kb) * (dh ** -0.5)
    win_start = jnp.clip(q_col - NA_WIN_C // 2, 0, GRID_W - NA_WIN_C)
    valid = (k_col[:, None, :] >= win_start[:, :, None]) & (k_col[:, None, :] < win_start[:, :, None] + NA_WIN_C)
    dr = key_rows - r[:, None] + (NA_WIN_R - 1)
    dc = jnp.clip(k_col[:, None, :] - q_col[:, :, None], -(NA_WIN_C - 1), NA_WIN_C - 1) + (NA_WIN_C - 1)
    bias = rpb.astype(F32)[:, dr[:, None, None, :, None], dc[None, :, :, None, :]]
    neg = jnp.finfo(F32).min
    scores = jnp.where(valid[:, :, None, :][None, None, None], scores + bias[None], neg)
    p = jax.nn.softmax(scores, axis=(-2, -1))
    out = jnp.einsum('bhrciwj,brwcjhd->brcihd', p, vb)
    return out.reshape(bsz, t, n_heads * dh)


def even_mixer(h, w_in, w_out, lam_re, lam_im, log_dt, b_re, b_im, c_re, c_im, s5_d, w_glu, b_glu, q_gain, k_gain, rpb):
    bsz, t, _ = h.shape
    proj = h @ w_in
    u, q, k, v = jnp.split(proj, [D_A, D_A + D_B, D_A + 2 * D_B], axis=-1)
    uf = u.astype(F32)
    ug = uf.reshape(bsz, t, S5_GROUPS, S5_GROUP)
    y_f = s5_direction(ug, lam_re[0], lam_im[0], log_dt[0], b_re[0], b_im[0], c_re[0], c_im[0], False)
    y_b = s5_direction(ug, lam_re[1], lam_im[1], log_dt[1], b_re[1], b_im[1], c_re[1], c_im[1], True)
    ys = (y_f + y_b).reshape(bsz, t, D_A) + s5_d.astype(F32) * uf
    ys = jax.nn.gelu(ys)
    a_out = ys * jax.nn.sigmoid(ys @ w_glu.astype(F32) + b_glu.astype(F32))
    qh = q.reshape(bsz, t, NA_HEADS, NA_HEAD_DIM)
    kh = k.reshape(bsz, t, NA_HEADS, NA_HEAD_DIM)
    vh = v.reshape(bsz, t, NA_HEADS, NA_HEAD_DIM)
    b_out = neighbourhood_attention(qh, kh, vh, q_gain, k_gain, rpb)
    mixed = jnp.concatenate([a_out.astype(h.dtype), b_out.astype(h.dtype)], axis=-1)
    return mixed @ w_out


def rotary(x, pos):
    half = x.shape[-1] // 2
    inv = ROPE_BASE ** (-jnp.arange(half, dtype=F32) / half)
    ang = pos[:, None] * inv[None, :]
    cos = jnp.cos(ang)[None, :, None, :]
    sin = jnp.sin(ang)[None, :, None, :]
    x1, x2 = x[..., :half], x[..., half:]
    return jnp.concatenate([x1 * cos - x2 * sin, x1 * sin + x2 * cos], axis=-1)


def retention_chunkwise(q, k, v, log_gamma, strict):
    bsz, n_heads, t, dk = q.shape
    dv = v.shape[-1]
    n_chunks = t // RET_CHUNK

    def to_chunks(z):
        return jnp.moveaxis(z.reshape(bsz, n_heads, n_chunks, RET_CHUNK, z.shape[-1]), 2, 0)

    idx = jnp.arange(RET_CHUNK, dtype=F32)
    diff = idx[:, None] - idx[None, :]
    allowed = (diff > 0) if strict else (diff >= 0)
    inner_decay = jnp.where(allowed[None], jnp.exp(log_gamma[:, None, None] * jnp.maximum(diff, 0.0)[None]), 0.0)
    q_decay = jnp.exp(log_gamma[:, None] * (idx + 1.0)[None, :])
    k_decay = jnp.exp(log_gamma[:, None] * (RET_CHUNK - 1.0 - idx)[None, :])
    chunk_decay = jnp.exp(log_gamma * RET_CHUNK)

    def step(state, qkv):
        qc, kc, vc = qkv
        scores = jnp.einsum('bhid,bhjd->bhij', qc, kc) * inner_decay
        inner = jnp.einsum('bhij,bhjv->bhiv', scores, vc)
        cross = jnp.einsum('bhid,bhdv->bhiv', qc * q_decay[..., None], state)
        state = state * chunk_decay[:, None, None] + jnp.einsum('bhjd,bhjv->bhdv', kc * k_decay[..., None], vc)
        return state, inner + cross

    s0 = jnp.zeros((bsz, n_heads, dk, dv), F32)
    _, out = lax.scan(step, s0, (to_chunks(q), to_chunks(k), to_chunks(v)))
    return jnp.moveaxis(out, 0, 2).reshape(bsz, n_heads, t, dv)


def odd_mixer(h, w_in, w_out, decay_logit, norm_gain):
    bsz, t, _ = h.shape
    proj = h @ w_in
    q, k, v, g = jnp.split(proj, [D_RET_K, 2 * D_RET_K, 2 * D_RET_K + D_RET_V], axis=-1)
    pos = jnp.arange(t, dtype=F32)
    q = rotary(q.astype(F32).reshape(bsz, t, RET_HEADS, RET_DK), pos)
    k = rotary(k.astype(F32).reshape(bsz, t, RET_HEADS, RET_DK), pos) * (RET_DK ** -0.5)
    v = v.astype(F32).reshape(bsz, t, RET_HEADS, RET_DV)
    q, k, v = (jnp.transpose(z, (0, 2, 1, 3)) for z in (q, k, v))
    log_gamma = jax.nn.log_sigmoid(decay_logit.astype(F32))
    o_f = retention_chunkwise(q, k, v, log_gamma[0], False)
    o_b = jnp.flip(retention_chunkwise(jnp.flip(q, 2), jnp.flip(k, 2), jnp.flip(v, 2), log_gamma[1], True), 2)
    o = jnp.transpose(o_f + o_b, (0, 2, 1, 3))
    o = o * lax.rsqrt(jnp.mean(o * o, axis=-1, keepdims=True) + EPS)
    o = o * norm_gain.astype(F32).reshape(RET_HEADS, RET_DV)
    y = jax.nn.silu(g.astype(F32)) * o.reshape(bsz, t, D_RET_V)
    return y.astype(h.dtype) @ w_out


def hierarchical_moe(x, w_group, b_group, w_router, b_router, w_gate, w_up, w_down):
    bsz, t, d = x.shape
    xt = x.reshape(-1, d)
    group_logits = (xt @ w_group + b_group).astype(F32)
    group_probs = jax.nn.softmax(group_logits, axis=-1)
    g_idx = jnp.argmax(group_logits, axis=-1)
    g_prob = jnp.take_along_axis(group_probs, g_idx[:, None], axis=-1)
    exp_all = (jnp.einsum('nd,dge->nge', xt, w_router) + b_router).astype(F32)
    exp_logits = jnp.take_along_axis(exp_all, g_idx[:, None, None], axis=1)[:, 0]
    top_vals, top_idx = lax.top_k(exp_logits, MOE_TOPK)
    top_w = jax.nn.softmax(top_vals, axis=-1) * g_prob
    expert_id = g_idx[:, None] * MOE_PER_GROUP + top_idx
    gates = jnp.sum(jax.nn.one_hot(expert_id, MOE_EXPERTS, dtype=F32) * top_w[..., None], axis=1)
    y = jnp.zeros(xt.shape, F32)
    for e in range(MOE_EXPERTS):
        hid = jax.nn.silu(xt @ w_gate[e]) * (xt @ w_up[e])
        y = y + gates[:, e:e + 1] * (hid @ w_down[e]).astype(F32)
    return y.reshape(bsz, t, d).astype(x.dtype)


def trunk(x, norm_mix, norm_ffn, w_in_even, w_out_even, s5_lambda_re, s5_lambda_im, s5_log_dt,
          s5_b_re, s5_b_im, s5_c_re, s5_c_im, s5_d, s5_w_glu, s5_b_glu, na_q_gain, na_k_gain, na_rpb,
          w_in_odd, w_out_odd, ret_decay_logit, ret_norm_gain, moe_w_group, moe_b_group,
          moe_w_router, moe_b_router, moe_w_gate, moe_w_up, moe_w_down):
    for layer in range(DEPTH):
        h = rmsnorm(x, norm_mix[layer])
        i = layer // 2
        if layer % 2 == 0:
            x = x + even_mixer(h, w_in_even[i], w_out_even[i], s5_lambda_re[i], s5_lambda_im[i], s5_log_dt[i],
                               s5_b_re[i], s5_b_im[i], s5_c_re[i], s5_c_im[i], s5_d[i], s5_w_glu[i], s5_b_glu[i],
                               na_q_gain[i], na_k_gain[i], na_rpb[i])
        else:
            x = x + odd_mixer(h, w_in_odd[i], w_out_odd[i], ret_decay_logit[i], ret_norm_gain[i])
        h = rmsnorm(x, norm_ffn[layer])
        x = x + hierarchical_moe(h, moe_w_group[layer], moe_b_group[layer], moe_w_router[layer],
                                 moe_b_router[layer], moe_w_gate[layer], moe_w_up[layer], moe_w_down[layer])
    return x


def setup_inputs(seed: int = 0) -> dict:
    key = jax.random.key(seed)
    ks = jax.random.split(key, 40)

    def nrm(k, shape, scale):
        return jax.random.normal(k, shape, F32) * scale

    n_idx = jnp.arange(S5_STATE, dtype=F32)
    s5_shape = (N_EVEN, 2, S5_GROUPS, S5_STATE)
    ret_logit0 = jnp.log(2.0 ** (5.0 + jnp.arange(RET_HEADS, dtype=F32)) - 1.0)
    return {
        'x_prompt': nrm(ks[0], (BATCH, SEQ, D_MODEL), 1.0),
        'x_sample': nrm(ks[1], (DEC_BATCH, DEC_SEQ, D_MODEL), 1.0),
        'norm_mix': 1.0 + nrm(ks[2], (DEPTH, D_MODEL), 0.02),
        'norm_ffn': 1.0 + nrm(ks[3], (DEPTH, D_MODEL), 0.02),
        'w_in_even': nrm(ks[4], (N_EVEN, D_MODEL, D_IN_EVEN), D_MODEL ** -0.5),
        'w_out_even': nrm(ks[5], (N_EVEN, D_MIX_EVEN, D_MODEL), D_MIX_EVEN ** -0.5),
        's5_lambda_re': -0.5 + nrm(ks[6], s5_shape, 0.01),
        's5_lambda_im': math.pi * n_idx + nrm(ks[7], s5_shape, 0.01),
        's5_log_dt': jax.random.uniform(ks[8], (N_EVEN, 2, S5_GROUPS), F32, math.log(1e-3), math.log(1e-1)),
        's5_b_re': nrm(ks[9], (N_EVEN, 2, S5_GROUPS, S5_STATE, S5_GROUP), (2 * S5_GROUP) ** -0.5),
        's5_b_im': nrm(ks[10], (N_EVEN, 2, S5_GROUPS, S5_STATE, S5_GROUP), (2 * S5_GROUP) ** -0.5),
        's5_c_re': nrm(ks[11], (N_EVEN, 2, S5_GROUPS, S5_GROUP, S5_STATE), (2 * S5_STATE) ** -0.5),
        's5_c_im': nrm(ks[12], (N_EVEN, 2, S5_GROUPS, S5_GROUP, S5_STATE), (2 * S5_STATE) ** -0.5),
        's5_d': nrm(ks[13], (N_EVEN, D_A), 1.0),
        's5_w_glu': nrm(ks[14], (N_EVEN, D_A, D_A), D_A ** -0.5),
        's5_b_glu': nrm(ks[15], (N_EVEN, D_A), 0.01),
        'na_q_gain': 1.0 + nrm(ks[16], (N_EVEN, NA_HEAD_DIM), 0.02),
        'na_k_gain': 1.0 + nrm(ks[17], (N_EVEN, NA_HEAD_DIM), 0.02),
        'na_rpb': nrm(ks[18], (N_EVEN, NA_HEADS, 2 * NA_WIN_R - 1, 2 * NA_WIN_C - 1), 0.02),
        'w_in_odd': nrm(ks[19], (N_ODD, D_MODEL, D_IN_ODD), D_MODEL ** -0.5),
        'w_out_odd': nrm(ks[20], (N_ODD, D_RET_V, D_MODEL), D_RET_V ** -0.5),
        'ret_decay_logit': ret_logit0 + nrm(ks[21], (N_ODD, 2, RET_HEADS), 0.05),
        'ret_norm_gain': 1.0 + nrm(ks[22], (N_ODD, D_RET_V), 0.02),
        'moe_w_group': nrm(ks[23], (DEPTH, D_MODEL, MOE_GROUPS), D_MODEL ** -0.5),
        'moe_b_group': nrm(ks[24], (DEPTH, MOE_GROUPS), 0.01),
        'moe_w_router': nrm(ks[25], (DEPTH, D_MODEL, MOE_GROUPS, MOE_PER_GROUP), D_MODEL ** -0.5),
        'moe_b_router': nrm(ks[26], (DEPTH, MOE_GROUPS, MOE_PER_GROUP), 0.01),
        'moe_w_gate': nrm(ks[27], (DEPTH, MOE_EXPERTS, D_MODEL, MOE_FF), D_MODEL ** -0.5),
        'moe_w_up': nrm(ks[28], (DEPTH, MOE_EXPERTS, D_MODEL, MOE_FF), D_MODEL ** -0.5),
        'moe_w_down': nrm(ks[29], (DEPTH, MOE_EXPERTS, MOE_FF, D_MODEL), MOE_FF ** -0.5),
    }


def reference(x_prompt, x_sample, norm_mix, norm_ffn, w_in_even, w_out_even, s5_lambda_re, s5_lambda_im,
              s5_log_dt, s5_b_re, s5_b_im, s5_c_re, s5_c_im, s5_d, s5_w_glu, s5_b_glu, na_q_gain, na_k_gain,
              na_rpb, w_in_odd, w_out_odd, ret_decay_logit, ret_norm_gain, moe_w_group, moe_b_group,
              moe_w_router, moe_b_router, moe_w_gate, moe_w_up, moe_w_down):
    params = (norm_mix, norm_ffn, w_in_even, w_out_even, s5_lambda_re, s5_lambda_im, s5_log_dt,
              s5_b_re, s5_b_im, s5_c_re, s5_c_im, s5_d, s5_w_glu, s5_b_glu, na_q_gain, na_k_gain, na_rpb,
              w_in_odd, w_out_odd, ret_decay_logit, ret_norm_gain, moe_w_group, moe_b_group,
              moe_w_router, moe_b_router, moe_w_gate, moe_w_up, moe_w_down)
    y_prompt = trunk(x_prompt, *params)
    y_sample = trunk(x_sample, *params)
    return (y_prompt, y_sample)
```

```python
import functools
import math

import jax
import jax.numpy as jnp
from jax import lax
from jax.experimental import pallas as pl
from jax.experimental.pallas import tpu as pltpu

F32 = jnp.float32
BF = jnp.bfloat16
EPS = 1e-6

D_MODEL = 1024
GRID_W = 64
S5_GROUPS = 32
S5_GROUP = 16
S5_STATE = 64
S5_CHUNK = 16
NA_HEADS = 8
NA_HEAD_DIM = 64
NA_WIN_R = 8
NA_WIN_C = 16
NA_HEAD_GROUP = 4
RET_HEADS = 8
RET_DK = 128
RET_DV = 256
RET_CHUNK = 256
ROPE_BASE = 10000.0
MOE_GROUPS = 4
MOE_PER_GROUP = 4
MOE_FF = 512
MOE_PAIRS = 6
MOE_CLASSES = MOE_GROUPS * MOE_PAIRS
MOE_TILE = 256
MASK_NEG = -1e30
VMEM_LIMIT = 56 * 1024 * 1024


def _params(n_axes, vmem=None):
    return pltpu.CompilerParams(dimension_semantics=("arbitrary",) * n_axes,
                                vmem_limit_bytes=vmem)


def _rms(x, gain):
    ms = jnp.mean(x * x, axis=-1, keepdims=True)
    return x * lax.rsqrt(ms + EPS) * gain


def _sigmoid(x):
    return 1.0 / (1.0 + jnp.exp(-x))


def _dot(a, b):
    return jnp.dot(a, b, preferred_element_type=F32)


def _dot_nt(a, b):
    return lax.dot_general(a, b, (((1,), (1,)), ((), ())), preferred_element_type=F32)


def _split_bf16(x):
    hi = x.astype(BF)
    lo = (x - hi.astype(F32)).astype(BF)
    return hi, lo


def _dot3(a, b):
    ah, al = _split_bf16(a)
    bh, bl = _split_bf16(b)
    return _dot(ah, bh) + _dot(ah, bl) + _dot(al, bh)


def _in_even_kernel(x_ref, g_ref, w_ref, qg_ref, kg_ref, seg_ref, u_ref, q_ref, k_ref, v_ref):
    h = _rms(x_ref[...], g_ref[...]).astype(BF)
    d = u_ref.shape[-1]
    u_ref[...] = _dot(h, w_ref[:, 0:d]).astype(BF)
    seg = seg_ref[...]

    def head_norm(z, gain):
        ssq = _dot((z * z).astype(BF), seg) * (1.0 / NA_HEAD_DIM)
        return z * lax.rsqrt(ssq + EPS) * gain

    q = _dot(h, w_ref[:, d:2 * d])
    q_ref[...] = (head_norm(q, qg_ref[...]) * (NA_HEAD_DIM ** -0.5)).astype(BF)
    k = _dot(h, w_ref[:, 2 * d:3 * d])
    k_ref[...] = head_norm(k, kg_ref[...]).astype(BF)
    v_ref[...] = _dot(h, w_ref[:, 3 * d:4 * d]).astype(BF)


def _in_even(x2, gain, w, q_gain, k_gain, tm=512):
    n, dm = x2.shape
    d = w.shape[1] // 4
    seg = (jnp.arange(d)[:, None] // NA_HEAD_DIM == jnp.arange(d)[None, :] // NA_HEAD_DIM).astype(BF)
    qg = jnp.tile(q_gain.astype(F32), NA_HEADS)[None]
    kg = jnp.tile(k_gain.astype(F32), NA_HEADS)[None]
    row = pl.BlockSpec((tm, d), lambda i: (i, 0))
    full = lambda shp: pl.BlockSpec(shp, lambda i: (0,) * len(shp))
    return pl.pallas_call(
        _in_even_kernel,
        out_shape=[jax.ShapeDtypeStruct((n, d), BF)] * 4,
        grid=(n // tm,),
        in_specs=[pl.BlockSpec((tm, dm), lambda i: (i, 0)), full((1, dm)), full(w.shape),
                  full((1, d)), full((1, d)), full((d, d))],
        out_specs=[row] * 4,
        compiler_params=_params(1, VMEM_LIMIT),
        name="in_even",
    )(x2, gain[None].astype(F32), w, qg, kg, seg)


def _s5_taps_kernel(l_ref, r_ref, o_ref):
    o_ref[0] = _dot3(l_ref[0], r_ref[0])


def _s5_taps(lhs, rhs):
    g2, m, k = lhs.shape
    nn = rhs.shape[-1]
    return pl.pallas_call(
        _s5_taps_kernel,
        out_shape=jax.ShapeDtypeStruct((g2, m, nn), F32),
        grid=(g2,),
        in_specs=[pl.BlockSpec((1, m, k), lambda i: (i, 0, 0)), pl.BlockSpec((1, k, nn), lambda i: (i, 0, 0))],
        out_specs=pl.BlockSpec((1, m, nn), lambda i: (i, 0, 0)),
        compiler_params=_params(1),
        name="s5_taps",
    )(lhs, rhs)


def _s5_operators(lam_re, lam_im, log_dt, b_re, b_im, c_re, c_im):
    L, G, H, P_ = S5_CHUNK, S5_GROUPS, S5_GROUP, S5_STATE
    lam = lax.complex(lam_re.astype(F32), lam_im.astype(F32))
    ldt = lam * jnp.exp(log_dt.astype(F32))[..., None]
    a = jnp.exp(ldt)
    bbar = ((a - 1.0) / lam)[..., None] * lax.complex(b_re.astype(F32), b_im.astype(F32))
    c = lax.complex(c_re.astype(F32), c_im.astype(F32))
    kk = jnp.arange(L + 1, dtype=F32)
    apow = jnp.exp(ldt[None] * kk[:, None, None, None])

    ce = c[:, :, None, :, :] * jnp.transpose(apow[:L], (1, 2, 0, 3))[:, :, :, None, :]
    lhs = jnp.concatenate([ce.real, -ce.imag], axis=-1).reshape(2 * G, L * H, 2 * P_)
    rhs = jnp.concatenate([bbar.real, bbar.imag], axis=2).reshape(2 * G, 2 * P_, H)
    taps = _s5_taps(lhs, rhs).reshape(2, G, L, H, H)
    taps = jnp.transpose(taps, (0, 1, 2, 4, 3))
    kf, kb = taps[0], taps[1]
    lagged = jnp.concatenate([jnp.flip(kb[:, 1:], axis=1), (kf[:, 0] + kb[:, 0])[:, None], kf[:, 1:]], axis=1)
    idx = jnp.arange(L)[None, :] - jnp.arange(L)[:, None] + (L - 1)
    m = jnp.transpose(lagged[:, idx], (0, 1, 3, 2, 4)).reshape(G, L * H, L * H)

    bt = jnp.transpose(bbar, (0, 1, 3, 2))
    pf = jnp.transpose(apow[L - 1::-1, 0], (1, 0, 2))[:, :, None, :] * bt[0][:, None]
    pb = jnp.transpose(apow[:L, 1], (1, 0, 2))[:, :, None, :] * bt[1][:, None]
    p = jnp.concatenate([pf.real, pb.real, pf.imag, pb.imag], axis=-1).reshape(G, L * H, 4 * P_)

    ct = jnp.transpose(c, (0, 1, 3, 2))
    gf = jnp.transpose(apow[1:, 0], (1, 2, 0))[..., None] * ct[0][:, :, None, :]
    gb = jnp.transpose(apow[L:0:-1, 1], (1, 2, 0))[..., None] * ct[1][:, :, None, :]
    q = jnp.concatenate([gf.real, gb.real, -gf.imag, -gb.imag], axis=1).reshape(G, 4 * P_, L * H)

    a_chunk = apow[L]
    a_mat = jnp.stack([jnp.concatenate([a_chunk[0].real, a_chunk[1].real], -1),
                       jnp.concatenate([a_chunk[0].imag, a_chunk[1].imag], -1)], axis=1)
    return m.astype(BF), p.astype(BF), q.astype(BF), a_mat.astype(F32)


def _s5_kernel(u_ref, m_ref, p_ref, q_ref, a_ref, y_ref, z_ref, s_ref, *, n_chunks, batch):
    u = u_ref[0]
    z_ref[...] = _dot(u, p_ref[0])
    half = 2 * S5_STATE
    ar = a_ref[0, 0:1, :]
    ai = a_ref[0, 1:2, :]
    fwd_lane = lax.broadcasted_iota(jnp.int32, (batch, half), 1) < S5_STATE

    def step(k, carry):
        sr, si = carry
        f0 = pl.multiple_of(k * batch, batch)
        b0 = pl.multiple_of((n_chunks - 1 - k) * batch, batch)
        s_ref[pl.ds(f0, batch), 0:S5_STATE] = sr[:, 0:S5_STATE]
        s_ref[pl.ds(b0, batch), S5_STATE:half] = sr[:, S5_STATE:half]
        s_ref[pl.ds(f0, batch), half:half + S5_STATE] = si[:, 0:S5_STATE]
        s_ref[pl.ds(b0, batch), half + S5_STATE:2 * half] = si[:, S5_STATE:half]
        zr = jnp.where(fwd_lane, z_ref[pl.ds(f0, batch), 0:half], z_ref[pl.ds(b0, batch), 0:half])
        zi = jnp.where(fwd_lane, z_ref[pl.ds(f0, batch), half:2 * half], z_ref[pl.ds(b0, batch), half:2 * half])
        return ar * sr - ai * si + zr, ar * si + ai * sr + zi

    zero = jnp.zeros((batch, half), F32)
    lax.fori_loop(0, n_chunks, step, (zero, zero))
    y = _dot(u, m_ref[0]) + _dot(s_ref[...].astype(BF), q_ref[0])
    y_ref[0] = y.astype(BF)


def _s5_scan(ug, m, p, q, a_mat, n_chunks, batch):
    g, r, w = ug.shape
    blk = lambda shp: pl.BlockSpec((1,) + shp, lambda i: (i, 0, 0))
    return pl.pallas_call(
        functools.partial(_s5_kernel, n_chunks=n_chunks, batch=batch),
        out_shape=jax.ShapeDtypeStruct((g, r, w), BF),
        grid=(g,),
        in_specs=[blk((r, w)), blk((w, w)), blk((w, w)), blk((w, w)), blk((2, w // 2))],
        out_specs=blk((r, w)),
        scratch_shapes=[pltpu.VMEM((r, w), F32), pltpu.VMEM((r, w), F32)],
        compiler_params=_params(1, VMEM_LIMIT),
        name="s5_scan",
    )(ug, m, p, q, a_mat)


def _na_bias_table(rpb):
    qc = jnp.arange(GRID_W)[:, None]
    kc = jnp.arange(GRID_W)[None, :]
    ws = jnp.clip(qc - NA_WIN_C // 2, 0, GRID_W - NA_WIN_C)
    valid = (kc >= ws) & (kc < ws + NA_WIN_C)
    dc = jnp.clip(kc - qc, -(NA_WIN_C - 1), NA_WIN_C - 1) + (NA_WIN_C - 1)
    e = jnp.where(valid[None, None], rpb.astype(F32)[:, :, dc], MASK_NEG)
    return jnp.concatenate([e[:, :-1], e[:, 1:]], axis=-1)


def _na_kernel(q_ref, k_ref, v_ref, e_ref, o_ref, *, rows):
    hg_w = NA_HEAD_GROUP * NA_HEAD_DIM
    lane_head = lax.broadcasted_iota(jnp.int32, (GRID_W, hg_w), 1) // NA_HEAD_DIM
    n_keys = NA_WIN_R * GRID_W

    def row_body(r, carry):
        rs = jnp.clip(r - NA_WIN_R // 2, 0, rows - NA_WIN_R)
        di = rs - r + (NA_WIN_R - 1)
        q0 = pl.multiple_of(r * GRID_W, GRID_W)
        k0 = pl.multiple_of(rs * GRID_W, GRID_W)
        for hg in range(NA_HEADS // NA_HEAD_GROUP):
            lanes = slice(hg * hg_w, (hg + 1) * hg_w)
            q4 = q_ref[0, pl.ds(q0, GRID_W), lanes]
            qs = jnp.concatenate([jnp.where(lane_head == hh, q4, jnp.zeros_like(q4))
                                  for hh in range(NA_HEAD_GROUP)], axis=0)
            k4 = k_ref[0, pl.ds(k0, n_keys), lanes]
            v4 = v_ref[0, pl.ds(k0, n_keys), lanes]
            s = _dot_nt(qs, k4)
            bias = jnp.concatenate(
                [jnp.concatenate([e_ref[hg * NA_HEAD_GROUP + hh, di + 2 * w2] for w2 in range(NA_WIN_R // 2)], axis=1)
                 for hh in range(NA_HEAD_GROUP)], axis=0)
            s = s + bias
            mx = jnp.max(s, axis=-1, keepdims=True)
            p = jnp.exp(s - mx)
            den = jnp.sum(p, axis=-1, keepdims=True)
            o = _dot(p.astype(BF), v4) * (1.0 / den)
            out = jnp.zeros((GRID_W, hg_w), F32)
            for hh in range(NA_HEAD_GROUP):
                out = jnp.where(lane_head == hh, o[hh * GRID_W:(hh + 1) * GRID_W], out)
            o_ref[0, pl.ds(q0, GRID_W), lanes] = out.astype(BF)
        return carry

    lax.fori_loop(0, rows, row_body, 0)


def _na(q, k, v, table):
    b, t, d = q.shape
    rows = t // GRID_W
    assert rows >= NA_WIN_R and t % GRID_W == 0
    seq = pl.BlockSpec((1, t, d), lambda i: (i, 0, 0))
    return pl.pallas_call(
        functools.partial(_na_kernel, rows=rows),
        out_shape=jax.ShapeDtypeStruct((b, t, d), BF),
        grid=(b,),
        in_specs=[seq, seq, seq, pl.BlockSpec(table.shape, lambda i: (0, 0, 0, 0))],
        out_specs=seq,
        compiler_params=_params(1, VMEM_LIMIT),
        name="na",
    )(q, k, v, table)


def _out_even_kernel(y_ref, u_ref, a_ref, x_ref, d_ref, wg_ref, bg_ref, wo_ref, o_ref):
    ys = y_ref[...].astype(F32) + d_ref[...] * u_ref[...].astype(F32)
    c0 = math.sqrt(2.0 / math.pi)
    ys = 0.5 * ys * (1.0 + jnp.tanh(c0 * (ys + 0.044715 * (ys * ys * ys))))
    gate = _sigmoid(_dot(ys.astype(BF), wg_ref[...]) + bg_ref[...])
    a_out = (ys * gate).astype(BF)
    da = a_out.shape[-1]
    o_ref[...] = x_ref[...] + _dot(a_out, wo_ref[0:da, :]) + _dot(a_ref[...], wo_ref[da:, :])


def _out_even(y, u, att, x2, s5_d, w_glu, b_glu, w_out, tm=512):
    n, dm = x2.shape
    d = y.shape[1]
    row = pl.BlockSpec((tm, d), lambda i: (i, 0))
    full = lambda shp: pl.BlockSpec(shp, lambda i: (0,) * len(shp))
    return pl.pallas_call(
        _out_even_kernel,
        out_shape=jax.ShapeDtypeStruct((n, dm), F32),
        grid=(n // tm,),
        in_specs=[row, row, row, pl.BlockSpec((tm, dm), lambda i: (i, 0)),
                  full((1, d)), full((d, d)), full((1, d)), full(w_out.shape)],
        out_specs=pl.BlockSpec((tm, dm), lambda i: (i, 0)),
        compiler_params=_params(1, VMEM_LIMIT),
        name="out_even",
    )(y, u, att, x2, s5_d[None].astype(F32), w_glu, b_glu[None].astype(F32), w_out)


def _in_odd_kernel(x_ref, g_ref, w_ref, wkt_ref, o_ref, kt_ref, h_ref):
    j = pl.program_id(1)

    @pl.when(j == 0)
    def _():
        h_ref[...] = _rms(x_ref[...], g_ref[...]).astype(BF)

    @pl.when(j != 1)
    def _():
        o_ref[...] = _dot(h_ref[...], w_ref[...]).astype(BF)

    @pl.when(j == 1)
    def _():
        kt_ref[0] = _dot_nt(wkt_ref[...], h_ref[...]).astype(BF)


def _in_odd(x2, gain, w, wkt, batch, tm=1024):
    n, dm = x2.shape
    t = n // batch
    nt = t // tm
    tn = dm
    n_col = w.shape[1] // tn
    return pl.pallas_call(
        _in_odd_kernel,
        out_shape=[jax.ShapeDtypeStruct((n, w.shape[1] - tn), BF), jax.ShapeDtypeStruct((batch, dm, t), BF)],
        grid=(n // tm, n_col),
        in_specs=[pl.BlockSpec((tm, dm), lambda i, j: (i, 0)), pl.BlockSpec((1, dm), lambda i, j: (0, 0)),
                  pl.BlockSpec((dm, tn), lambda i, j: (0, j)), pl.BlockSpec((dm, dm), lambda i, j: (0, 0))],
        out_specs=[pl.BlockSpec((tm, tn), lambda i, j: (i, jnp.maximum(j - 1, 0))),
                   pl.BlockSpec((1, dm, tm), lambda i, j: (i // nt, 0, i % nt))],
        scratch_shapes=[pltpu.VMEM((tm, dm), BF)],
        compiler_params=_params(2, VMEM_LIMIT),
        name="in_odd",
    )(x2, gain[None].astype(F32), w, wkt)


def _ret_kernel(lg_ref, q_ref, kt_ref, v_ref, g_ref, cq_ref, sq_ref, ck_ref, sk_ref, gain_ref, o_ref,
                qr_ref, kr_ref, sb_ref, sf_ref, sbc_ref, *, n_chunks):
    c = RET_CHUNK
    hd = pl.program_id(1)
    lgf = lg_ref[0, hd]
    lgb = lg_ref[1, hd]
    half = RET_DK // 2

    qf = q_ref[0].astype(F32)
    qr_ref[...] = (qf * cq_ref[...] + pltpu.roll(qf, half, 1) * sq_ref[...]).astype(BF)
    k1 = kt_ref[0, 0:half, :].astype(F32)
    k2 = kt_ref[0, half:, :].astype(F32)
    ck = ck_ref[...]
    sk = sk_ref[...]
    scale = RET_DK ** -0.5
    kr_ref[0:half, :] = ((k1 * ck - k2 * sk) * scale).astype(BF)
    kr_ref[half:, :] = ((k1 * sk + k2 * ck) * scale).astype(BF)

    ii = lax.broadcasted_iota(jnp.int32, (c, c), 0)
    jj = lax.broadcasted_iota(jnp.int32, (c, c), 1)
    dist = (ii - jj).astype(F32)
    decay = jnp.where(ii >= jj, jnp.exp(lgf * jnp.maximum(dist, 0.0)), jnp.exp(lgb * jnp.maximum(-dist, 0.0)))
    tok_col = lax.broadcasted_iota(jnp.int32, (c, 1), 0).astype(F32)
    tok_row = lax.broadcasted_iota(jnp.int32, (1, c), 1).astype(F32)
    q_dec_f = jnp.exp(lgf * (tok_col + 1.0))
    q_dec_b = jnp.exp(lgb * (c - tok_col))
    k_dec_f = jnp.exp(lgf * (c - 1.0 - tok_row))
    k_dec_b = jnp.exp(lgb * tok_row)
    chunk_f = jnp.exp(lgf * c)
    chunk_b = jnp.exp(lgb * c)

    sbc_ref[...] = jnp.zeros_like(sbc_ref)

    def back(i, carry):
        ci = n_chunks - 1 - i
        t0 = pl.multiple_of(ci * c, c)
        sb_ref[ci] = sbc_ref[...].astype(BF)
        kd = (kr_ref[:, pl.ds(t0, c)].astype(F32) * k_dec_b).astype(BF)
        sbc_ref[...] = sbc_ref[...] * chunk_b + _dot(kd, v_ref[0, pl.ds(t0, c), :])
        return carry

    lax.fori_loop(0, n_chunks, back, 0)

    sf_ref[...] = jnp.zeros_like(sf_ref)
    gain = gain_ref[...]

    def fwd(ci, carry):
        t0 = pl.multiple_of(ci * c, c)
        qc = qr_ref[pl.ds(t0, c), :]
        kc = kr_ref[:, pl.ds(t0, c)]
        vc = v_ref[0, pl.ds(t0, c), :]
        s = (_dot(qc, kc) * decay).astype(BF)
        qcf = qc.astype(F32)
        qd = jnp.concatenate([(qcf * q_dec_f).astype(BF), (qcf * q_dec_b).astype(BF)], axis=1)
        st = jnp.concatenate([sf_ref[...].astype(BF), sb_ref[ci]], axis=0)
        o = _dot(s, vc) + _dot(qd, st)
        kd = (kc.astype(F32) * k_dec_f).astype(BF)
        sf_ref[...] = sf_ref[...] * chunk_f + _dot(kd, vc)
        o = o * lax.rsqrt(jnp.mean(o * o, axis=-1, keepdims=True) + EPS) * gain
        gt = g_ref[0, pl.ds(t0, c), :].astype(F32)
        o_ref[0, pl.ds(t0, c), :] = (gt * _sigmoid(gt) * o).astype(BF)
        return carry

    lax.fori_loop(0, n_chunks, fwd, 0)


def _retention(qvg, kt, log_gamma, norm_gain, batch):
    n, wq = qvg.shape
    t = n // batch
    assert t % RET_CHUNK == 0
    n_chunks = t // RET_CHUNK
    qvg3 = qvg.reshape(batch, t, wq)
    half = RET_DK // 2
    inv = ROPE_BASE ** (-jnp.arange(half, dtype=F32) / half)
    ang = jnp.arange(t, dtype=F32)[:, None] * inv[None, :]
    cos, sin = jnp.cos(ang), jnp.sin(ang)
    cq = jnp.concatenate([cos, cos], axis=1)
    sq = jnp.concatenate([-sin, sin], axis=1)
    v_off = (RET_HEADS * RET_DK) // RET_DV
    g_off = v_off + RET_HEADS
    const = lambda shp: pl.BlockSpec(shp, lambda b, h, lg: (0,) * len(shp))
    return pl.pallas_call(
        functools.partial(_ret_kernel, n_chunks=n_chunks),
        out_shape=jax.ShapeDtypeStruct((batch, t, RET_HEADS * RET_DV), BF),
        grid_spec=pltpu.PrefetchScalarGridSpec(
            num_scalar_prefetch=1,
            grid=(batch, RET_HEADS),
            in_specs=[pl.BlockSpec((1, t, RET_DK), lambda b, h, lg: (b, 0, h)),
                      pl.BlockSpec((1, RET_DK, t), lambda b, h, lg: (b, h, 0)),
                      pl.BlockSpec((1, t, RET_DV), lambda b, h, lg: (b, 0, v_off + h)),
                      pl.BlockSpec((1, t, RET_DV), lambda b, h, lg: (b, 0, g_off + h)),
                      const((t, RET_DK)), const((t, RET_DK)), const((half, t)), const((half, t)),
                      pl.BlockSpec((1, RET_DV), lambda b, h, lg: (0, h))],
            out_specs=pl.BlockSpec((1, t, RET_DV), lambda b, h, lg: (b, 0, h)),
            scratch_shapes=[pltpu.VMEM((t, RET_DK), BF), pltpu.VMEM((RET_DK, t), BF),
                            pltpu.VMEM((n_chunks, RET_DK, RET_DV), BF),
                            pltpu.VMEM((RET_DK, RET_DV), F32), pltpu.VMEM((RET_DK, RET_DV), F32)]),
        compiler_params=_params(2, VMEM_LIMIT),
        name="retention",
    )(log_gamma, qvg3, kt, qvg3, qvg3, cq, sq, cos.T, sin.T, norm_gain[None].astype(F32))


def _out_odd_kernel(y_ref, x_ref, w_ref, o_ref):
    o_ref[...] = x_ref[...] + _dot(y_ref[...], w_ref[...])


def _out_odd(y, x2, w, tm=512):
    n, dm = x2.shape
    kdim = y.shape[1]
    return pl.pallas_call(
        _out_odd_kernel,
        out_shape=jax.ShapeDtypeStruct((n, dm), F32),
        grid=(n // tm,),
        in_specs=[pl.BlockSpec((tm, kdim), lambda i: (i, 0)), pl.BlockSpec((tm, dm), lambda i: (i, 0)),
                  pl.BlockSpec(w.shape, lambda i: (0, 0))],
        out_specs=pl.BlockSpec((tm, dm), lambda i: (i, 0)),
        compiler_params=_params(1, VMEM_LIMIT),
        name="out_odd",
    )(y, x2, w)


def _router_kernel(x_ref, g_ref, w_ref, b_ref, o_ref):
    h = _rms(x_ref[...], g_ref[...])
    logits = _dot3(h, w_ref[...]) + b_ref[...]
    lane = lax.broadcasted_iota(jnp.int32, logits.shape, 1).astype(F32)
    neg = jnp.float32(-jnp.inf)

    def top(mask):
        val = jnp.max(jnp.where(mask, logits, neg), axis=-1, keepdims=True)
        idx = jnp.min(jnp.where(mask & (logits == val), lane, 1e6), axis=-1, keepdims=True)
        return val, idx

    gmask = lane < MOE_GROUPS
    gmax, gidx = top(gmask)
    gprob = 1.0 / jnp.sum(jnp.where(gmask, jnp.exp(logits - gmax), 0.0), axis=-1, keepdims=True)
    base = MOE_GROUPS + MOE_PER_GROUP * gidx
    emask = (lane >= base) & (lane < base + MOE_PER_GROUP)
    v1, i1 = top(emask)
    v2, i2 = top(emask & (lane != i1))
    e2 = jnp.exp(v2 - v1)
    w1 = gprob / (1.0 + e2)
    w2 = gprob * e2 / (1.0 + e2)
    a1 = i1 - base
    a2 = i2 - base
    lo = jnp.minimum(a1, a2)
    hi = jnp.maximum(a1, a2)
    pair = lo * (7.0 - lo) * 0.5 + hi - lo - 1.0
    cls = gidx * MOE_PAIRS + pair
    w_lo = jnp.where(a1 < a2, w1, w2)
    w_hi = jnp.where(a1 < a2, w2, w1)
    o_ref[...] = jnp.where(lane == 0, cls, jnp.where(lane == 1, w_lo, jnp.where(lane == 2, w_hi, 0.0)))


def _router(x2, gain, w_r, b_r, tm=512):
    n, dm = x2.shape
    return pl.pallas_call(
        _router_kernel,
        out_shape=jax.ShapeDtypeStruct((n, 128), F32),
        grid=(n // tm,),
        in_specs=[pl.BlockSpec((tm, dm), lambda i: (i, 0)), pl.BlockSpec((1, dm), lambda i: (0, 0)),
                  pl.BlockSpec((dm, 128), lambda i: (0, 0)), pl.BlockSpec((1, 128), lambda i: (0, 0))],
        out_specs=pl.BlockSpec((tm, 128), lambda i: (i, 0)),
        compiler_params=_params(1, VMEM_LIMIT),
        name="moe_router",
    )(x2, gain, w_r, b_r)


def _gather_rows(src_hbm, idx_ref, buf, sem):
    n = buf.shape[0]

    def issue(r, carry):
        pltpu.make_async_copy(src_hbm.at[pl.ds(idx_ref[0, 0, r], 1), :], buf.at[pl.ds(r, 1), :], sem).start()
        return carry

    lax.fori_loop(0, n, issue, 0)

    def drain(r, carry):
        pltpu.make_async_copy(src_hbm.at[pl.ds(0, 1), :], buf.at[pl.ds(r, 1), :], sem).wait()
        return carry

    lax.fori_loop(0, n, drain, 0)


def _expert_kernel(ea_ref, eb_ref, act_ref, idx_ref, x_hbm, g_ref, wl_ref, wh_ref,
                   wga_ref, wua_ref, wda_ref, wgb_ref, wub_ref, wdb_ref, o_ref, xbuf, sem):
    i = pl.program_id(0)

    @pl.when(act_ref[i] > 0)
    def _():
        _gather_rows(x_hbm, idx_ref, xbuf, sem)
        h = _rms(xbuf[...], g_ref[...]).astype(BF)

        def ffn(wg, wu, wd, wrow):
            gate = _dot(h, wg[0])
            hid = gate * _sigmoid(gate) * _dot(h, wu[0])
            return _dot((hid * wrow[...]).astype(BF), wd[0])

        o_ref[...] = ffn(wga_ref, wua_ref, wda_ref, wl_ref) + ffn(wgb_ref, wub_ref, wdb_ref, wh_ref)

    @pl.when(act_ref[i] == 0)
    def _():
        o_ref[...] = jnp.zeros_like(o_ref)


def _experts(x2, gain, tile_ea, tile_eb, tile_act, src, w_lo, w_hi, w_gate, w_up, w_down):
    n, dm = x2.shape
    n_tiles = tile_ea.shape[0]
    tm = MOE_TILE
    ff = w_gate.shape[-1]
    wa = lambda shp: pl.BlockSpec((1,) + shp, lambda i, ea, eb, act: (ea[i], 0, 0))
    wb = lambda shp: pl.BlockSpec((1,) + shp, lambda i, ea, eb, act: (eb[i], 0, 0))
    rowv = pl.BlockSpec((tm, 1), lambda i, ea, eb, act: (i, 0))
    return pl.pallas_call(
        _expert_kernel,
        out_shape=jax.ShapeDtypeStruct((n_tiles * tm, dm), F32),
        grid_spec=pltpu.PrefetchScalarGridSpec(
            num_scalar_prefetch=3,
            grid=(n_tiles,),
            in_specs=[pl.BlockSpec((1, 1, tm), lambda i, ea, eb, act: (i, 0, 0), memory_space=pltpu.SMEM),
                      pl.BlockSpec(memory_space=pl.ANY),
                      pl.BlockSpec((1, dm), lambda i, ea, eb, act: (0, 0)),
                      rowv, rowv,
                      wa((dm, ff)), wa((dm, ff)), wa((ff, dm)), wb((dm, ff)), wb((dm, ff)), wb((ff, dm))],
            out_specs=pl.BlockSpec((tm, dm), lambda i, ea, eb, act: (i, 0)),
            scratch_shapes=[pltpu.VMEM((tm, dm), F32), pltpu.SemaphoreType.DMA]),
        compiler_params=_params(1, VMEM_LIMIT),
        name="moe_experts",
    )(tile_ea, tile_eb, tile_act, src.reshape(n_tiles, 1, tm), x2, gain, w_lo, w_hi,
      w_gate, w_up, w_down, w_gate, w_up, w_down)


def _combine_kernel(pos_ref, x_ref, y_hbm, o_ref, ybuf, sem):
    _gather_rows(y_hbm, pos_ref, ybuf, sem)
    o_ref[...] = x_ref[...] + ybuf[...]


def _combine(x2, y_sorted, pos, tm=256):
    n, dm = x2.shape
    return pl.pallas_call(
        _combine_kernel,
        out_shape=jax.ShapeDtypeStruct((n, dm), F32),
        grid=(n // tm,),
        in_specs=[pl.BlockSpec((1, 1, tm), lambda i: (i, 0, 0), memory_space=pltpu.SMEM),
                  pl.BlockSpec((tm, dm), lambda i: (i, 0)),
                  pl.BlockSpec(memory_space=pl.ANY)],
        out_specs=pl.BlockSpec((tm, dm), lambda i: (i, 0)),
        scratch_shapes=[pltpu.VMEM((tm, dm), F32), pltpu.SemaphoreType.DMA],
        compiler_params=_params(1, VMEM_LIMIT),
        name="moe_combine",
    )(pos.reshape(n // tm, 1, tm), x2, y_sorted)


def _moe(x2, gain, w_r, b_r, w_gate, w_up, w_down):
    n, dm = x2.shape
    tm = MOE_TILE
    gain = gain[None].astype(F32)
    info = _router(x2, gain, w_r, b_r)
    cls = info[:, 0].astype(jnp.int32)
    onehot = (cls[:, None] == jnp.arange(MOE_CLASSES)[None, :]).astype(jnp.int32)
    csum = jnp.cumsum(onehot, axis=0)
    counts = csum[-1]
    rank = jnp.sum(csum * onehot, axis=1) - 1
    padded = ((counts + tm - 1) // tm) * tm
    ends = jnp.cumsum(padded)
    starts = ends - padded
    pos = starts[cls] + rank
    n_tiles = n // tm + MOE_CLASSES
    src = jnp.zeros((n_tiles * tm,), jnp.int32).at[pos].set(jnp.arange(n, dtype=jnp.int32))
    w_lo = jnp.zeros((n_tiles * tm, 1), F32).at[pos, 0].set(info[:, 1])
    w_hi = jnp.zeros((n_tiles * tm, 1), F32).at[pos, 0].set(info[:, 2])
    tile_start = jnp.arange(n_tiles, dtype=jnp.int32) * tm
    tile_cls = jnp.minimum(jnp.sum(tile_start[:, None] >= ends[None, :], axis=1), MOE_CLASSES - 1)
    tile_act = (tile_start < ends[-1]).astype(jnp.int32)
    pair_lo = jnp.array([0, 0, 0, 1, 1, 2], jnp.int32)
    pair_hi = jnp.array([1, 2, 3, 2, 3, 3], jnp.int32)
    grp = tile_cls // MOE_PAIRS
    tile_ea = (grp * MOE_PER_GROUP + pair_lo[tile_cls % MOE_PAIRS]).astype(jnp.int32)
    tile_eb = (grp * MOE_PER_GROUP + pair_hi[tile_cls % MOE_PAIRS]).astype(jnp.int32)
    y_sorted = _experts(x2, gain, tile_ea, tile_eb, tile_act, src, w_lo, w_hi, w_gate, w_up, w_down)
    return _combine(x2, y_sorted, pos.astype(jnp.int32))


def _router_weights(w_group, b_group, w_router, b_router):
    dm = w_group.shape[0]
    w = jnp.concatenate([w_group.astype(F32), w_router.astype(F32).reshape(dm, -1)], axis=1)
    b = jnp.concatenate([b_group.astype(F32), b_router.astype(F32).reshape(-1)])
    pad = 128 - w.shape[1]
    return jnp.pad(w, ((0, 0), (0, pad))), jnp.pad(b, (0, pad))[None]


def _trunk(x, p):
    b, t, dm = x.shape
    n = b * t
    x2 = x.reshape(n, dm)

    u, q, k, v = _in_even(x2, p["norm_mix"][0], p["w_in_even"], p["na_q_gain"], p["na_k_gain"])
    n_chunks = t // S5_CHUNK
    ug = jnp.transpose(u.reshape(b, n_chunks, S5_CHUNK, S5_GROUPS, S5_GROUP), (3, 1, 0, 2, 4))
    ug = ug.reshape(S5_GROUPS, n_chunks * b, S5_CHUNK * S5_GROUP)
    yg = _s5_scan(ug, *p["s5_ops"], n_chunks, b)
    y = jnp.transpose(yg.reshape(S5_GROUPS, n_chunks, b, S5_CHUNK, S5_GROUP), (2, 1, 3, 0, 4)).reshape(n, -1)
    d_att = q.shape[1]
    att = _na(q.reshape(b, t, d_att), k.reshape(b, t, d_att), v.reshape(b, t, d_att), p["na_table"])
    x2 = _out_even(y, u, att.reshape(n, d_att), x2, p["s5_d"], p["s5_w_glu"], p["s5_b_glu"], p["w_out_even"])
    x2 = _moe(x2, p["norm_ffn"][0], *p["router"][0], *p["experts"][0])

    qvg, kt = _in_odd(x2, p["norm_mix"][1], p["w_in_odd"], p["w_k_t"], b)
    yr = _retention(qvg, kt, p["log_gamma"], p["ret_norm_gain"], b)
    x2 = _out_odd(yr.reshape(n, -1), x2, p["w_out_odd"])
    x2 = _moe(x2, p["norm_ffn"][1], *p["router"][1], *p["experts"][1])
    return x2.reshape(b, t, dm)


def kernel(x_prompt, x_sample, norm_mix, norm_ffn, w_in_even, w_out_even, s5_lambda_re, s5_lambda_im, s5_log_dt, s5_b_re, s5_b_im, s5_c_re, s5_c_im, s5_d, s5_w_glu, s5_b_glu, na_q_gain, na_k_gain, na_rpb, w_in_odd, w_out_odd, ret_decay_logit, ret_norm_gain, moe_w_group, moe_b_group, moe_w_router, moe_b_router, moe_w_gate, moe_w_up, moe_w_down):
    assert norm_mix.shape[0] == 2
    dk_all = RET_HEADS * RET_DK
    w_odd = w_in_odd[0].astype(BF)
    p = {
        "norm_mix": norm_mix, "norm_ffn": norm_ffn,
        "w_in_even": w_in_even[0].astype(BF), "w_out_even": w_out_even[0].astype(BF),
        "s5_ops": _s5_operators(s5_lambda_re[0], s5_lambda_im[0], s5_log_dt[0], s5_b_re[0], s5_b_im[0],
                                s5_c_re[0], s5_c_im[0]),
        "s5_d": s5_d[0], "s5_w_glu": s5_w_glu[0].astype(BF), "s5_b_glu": s5_b_glu[0],
        "na_q_gain": na_q_gain[0], "na_k_gain": na_k_gain[0], "na_table": _na_bias_table(na_rpb[0]),
        "w_in_odd": w_odd, "w_k_t": w_odd[:, dk_all:2 * dk_all].T,
        "w_out_odd": w_out_odd[0].astype(BF),
        "log_gamma": jax.nn.log_sigmoid(ret_decay_logit[0].astype(F32)),
        "ret_norm_gain": ret_norm_gain[0],
        "router": [_router_weights(moe_w_group[l], moe_b_group[l], moe_w_router[l], moe_b_router[l]) for l in range(2)],
        "experts": [(moe_w_gate[l].astype(BF), moe_w_up[l].astype(BF), moe_w_down[l].astype(BF)) for l in range(2)],
    }
    return _trunk(x_prompt, p), _trunk(x_sample, p)
```

```python
import functools
import math

import jax
import jax.numpy as jnp
from jax import lax
from jax.experimental import pallas as pl
from jax.experimental.pallas import tpu as pltpu

F32 = jnp.float32
BF = jnp.bfloat16
EPS = 1e-6

D_MODEL = 1024
GRID_W = 64
S5_GROUPS = 32
S5_GROUP = 16
S5_STATE = 64
S5_CHUNK = 16
NA_HEADS = 8
NA_HEAD_DIM = 64
NA_WIN_R = 8
NA_WIN_C = 16
NA_HEAD_GROUP = 4
RET_HEADS = 8
RET_DK = 128
RET_DV = 256
RET_CHUNK = 256
ROPE_BASE = 10000.0
MOE_GROUPS = 4
MOE_PER_GROUP = 4
MOE_FF = 512
MOE_PAIRS = 6
MOE_CLASSES = MOE_GROUPS * MOE_PAIRS
MOE_TILE = 256
ROW_ALIGN = 8
MOE_FILL = MOE_TILE + ROW_ALIGN
MASK_NEG = -1e30
VMEM_LIMIT = 56 * 1024 * 1024


def _params(n_axes, vmem=None):
    return pltpu.CompilerParams(dimension_semantics=("arbitrary",) * n_axes,
                                vmem_limit_bytes=vmem)


def _rms(x, gain):
    ms = jnp.mean(x * x, axis=-1, keepdims=True)
    return x * lax.rsqrt(ms + EPS) * gain


def _sigmoid(x):
    return 1.0 / (1.0 + jnp.exp(-x))


def _dot(a, b):
    return jnp.dot(a, b, preferred_element_type=F32)


def _dot_nt(a, b):
    return lax.dot_general(a, b, (((1,), (1,)), ((), ())), preferred_element_type=F32)


def _split_bf16(x):
    hi = x.astype(BF)
    lo = (x - hi.astype(F32)).astype(BF)
    return hi, lo


def _dot3(a, b):
    ah, al = _split_bf16(a)
    bh, bl = _split_bf16(b)
    return _dot(ah, bh) + _dot(ah, bl) + _dot(al, bh)


def _in_even_kernel(x_ref, g_ref, w_ref, qg_ref, kg_ref, seg_ref, u_ref, q_ref, k_ref, v_ref):
    h = _rms(x_ref[...], g_ref[...]).astype(BF)
    d = u_ref.shape[-1]
    u_ref[...] = _dot(h, w_ref[:, 0:d]).astype(BF)
    seg = seg_ref[...]

    def head_norm(z, gain):
        ssq = _dot((z * z).astype(BF), seg) * (1.0 / NA_HEAD_DIM)
        return z * lax.rsqrt(ssq + EPS) * gain

    q = _dot(h, w_ref[:, d:2 * d])
    q_ref[...] = (head_norm(q, qg_ref[...]) * (NA_HEAD_DIM ** -0.5)).astype(BF)
    k = _dot(h, w_ref[:, 2 * d:3 * d])
    k_ref[...] = head_norm(k, kg_ref[...]).astype(BF)
    v_ref[...] = _dot(h, w_ref[:, 3 * d:4 * d]).astype(BF)


def _in_even(x2, gain, w, q_gain, k_gain, tm=512):
    n, dm = x2.shape
    d = w.shape[1] // 4
    seg = (jnp.arange(d)[:, None] // NA_HEAD_DIM == jnp.arange(d)[None, :] // NA_HEAD_DIM).astype(BF)
    qg = jnp.tile(q_gain.astype(F32), NA_HEADS)[None]
    kg = jnp.tile(k_gain.astype(F32), NA_HEADS)[None]
    row = pl.BlockSpec((tm, d), lambda i: (i, 0))
    full = lambda shp: pl.BlockSpec(shp, lambda i: (0,) * len(shp))
    return pl.pallas_call(
        _in_even_kernel,
        out_shape=[jax.ShapeDtypeStruct((n, d), BF)] * 4,
        grid=(n // tm,),
        in_specs=[pl.BlockSpec((tm, dm), lambda i: (i, 0)), full((1, dm)), full(w.shape),
                  full((1, d)), full((1, d)), full((d, d))],
        out_specs=[row] * 4,
        compiler_params=_params(1, VMEM_LIMIT),
        name="in_even",
    )(x2, gain[None].astype(F32), w, qg, kg, seg)


def _s5_taps_kernel(l_ref, r_ref, o_ref):
    o_ref[0] = _dot3(l_ref[0], r_ref[0])


def _s5_taps(lhs, rhs):
    g2, m, k = lhs.shape
    nn = rhs.shape[-1]
    return pl.pallas_call(
        _s5_taps_kernel,
        out_shape=jax.ShapeDtypeStruct((g2, m, nn), F32),
        grid=(g2,),
        in_specs=[pl.BlockSpec((1, m, k), lambda i: (i, 0, 0)), pl.BlockSpec((1, k, nn), lambda i: (i, 0, 0))],
        out_specs=pl.BlockSpec((1, m, nn), lambda i: (i, 0, 0)),
        compiler_params=_params(1),
        name="s5_taps",
    )(lhs, rhs)


def _s5_operators(lam_re, lam_im, log_dt, b_re, b_im, c_re, c_im):
    L, G, H, P_ = S5_CHUNK, S5_GROUPS, S5_GROUP, S5_STATE
    lr, li = lam_re.astype(F32), lam_im.astype(F32)
    dt = jnp.exp(log_dt.astype(F32))[..., None]
    kk = jnp.arange(L + 1, dtype=F32)[:, None, None, None]
    mag = jnp.exp(kk * (lr * dt)[None])
    ang = kk * (li * dt)[None]
    pw = (mag * jnp.cos(ang), mag * jnp.sin(ang))

    def cmul(x, y):
        return x[0] * y[0] - x[1] * y[1], x[0] * y[1] + x[1] * y[0]

    xr, xi = pw[0][1] - 1.0, pw[1][1]
    den = lr * lr + li * li
    coef = ((xr * lr + xi * li) / den, (xi * lr - xr * li) / den)
    bbar = cmul((coef[0][..., None], coef[1][..., None]), (b_re.astype(F32), b_im.astype(F32)))
    c = (c_re.astype(F32), c_im.astype(F32))

    def at(k_slice, direction, perm):
        return tuple(jnp.transpose(z[k_slice, direction], perm) for z in pw)

    apl = tuple(jnp.transpose(z[:L], (1, 2, 0, 3))[:, :, :, None, :] for z in pw)
    ce = cmul((c[0][:, :, None], c[1][:, :, None]), apl)
    lhs = jnp.concatenate([ce[0], -ce[1]], axis=-1).reshape(2 * G, L * H, 2 * P_)
    rhs = jnp.concatenate([bbar[0], bbar[1]], axis=2).reshape(2 * G, 2 * P_, H)
    taps = _s5_taps(lhs, rhs).reshape(2, G, L, H, H)
    taps = jnp.transpose(taps, (0, 1, 2, 4, 3))
    kf, kb = taps[0], taps[1]
    lagged = jnp.concatenate([jnp.flip(kb[:, 1:], axis=1), (kf[:, 0] + kb[:, 0])[:, None], kf[:, 1:]], axis=1)
    idx = jnp.arange(L)[None, :] - jnp.arange(L)[:, None] + (L - 1)
    m = jnp.transpose(lagged[:, idx], (0, 1, 3, 2, 4)).reshape(G, L * H, L * H)

    bt = tuple(jnp.transpose(z, (0, 1, 3, 2)) for z in bbar)
    expand_j = lambda z: z[:, :, None, :]
    pf = cmul(tuple(map(expand_j, at(slice(L - 1, None, -1), 0, (1, 0, 2)))), (bt[0][0][:, None], bt[1][0][:, None]))
    pb = cmul(tuple(map(expand_j, at(slice(0, L), 1, (1, 0, 2)))), (bt[0][1][:, None], bt[1][1][:, None]))
    p = jnp.concatenate([pf[0], pb[0], pf[1], pb[1]], axis=-1).reshape(G, L * H, 4 * P_)

    ct = tuple(jnp.transpose(z, (0, 1, 3, 2)) for z in c)
    expand_h = lambda z: z[..., None]
    gf = cmul(tuple(map(expand_h, at(slice(1, None), 0, (1, 2, 0)))), (ct[0][0][:, :, None, :], ct[1][0][:, :, None, :]))
    gb = cmul(tuple(map(expand_h, at(slice(L, 0, -1), 1, (1, 2, 0)))), (ct[0][1][:, :, None, :], ct[1][1][:, :, None, :]))
    q = jnp.concatenate([gf[0], gb[0], -gf[1], -gb[1]], axis=1).reshape(G, 4 * P_, L * H)

    a_mat = jnp.stack([jnp.concatenate([pw[0][L, 0], pw[0][L, 1]], -1),
                       jnp.concatenate([pw[1][L, 0], pw[1][L, 1]], -1)], axis=1)
    return m.astype(BF), p.astype(BF), q.astype(BF), a_mat.astype(F32)


def _s5_kernel(u_ref, m_ref, p_ref, q_ref, a_ref, y_ref, z_ref, s_ref, *, n_chunks, batch):
    u = u_ref[0]
    z_ref[...] = _dot(u, p_ref[0])
    half = 2 * S5_STATE
    ar = a_ref[0, 0:1, :]
    ai = a_ref[0, 1:2, :]
    fwd_lane = lax.broadcasted_iota(jnp.int32, (batch, half), 1) < S5_STATE

    def step(k, carry):
        sr, si = carry
        f0 = pl.multiple_of(k * batch, batch)
        b0 = pl.multiple_of((n_chunks - 1 - k) * batch, batch)
        s_ref[pl.ds(f0, batch), 0:S5_STATE] = sr[:, 0:S5_STATE]
        s_ref[pl.ds(b0, batch), S5_STATE:half] = sr[:, S5_STATE:half]
        s_ref[pl.ds(f0, batch), half:half + S5_STATE] = si[:, 0:S5_STATE]
        s_ref[pl.ds(b0, batch), half + S5_STATE:2 * half] = si[:, S5_STATE:half]
        zr = jnp.where(fwd_lane, z_ref[pl.ds(f0, batch), 0:half], z_ref[pl.ds(b0, batch), 0:half])
        zi = jnp.where(fwd_lane, z_ref[pl.ds(f0, batch), half:2 * half], z_ref[pl.ds(b0, batch), half:2 * half])
        return ar * sr - ai * si + zr, ar * si + ai * sr + zi

    zero = jnp.zeros((batch, half), F32)
    lax.fori_loop(0, n_chunks, step, (zero, zero))
    y = _dot(u, m_ref[0]) + _dot(s_ref[...].astype(BF), q_ref[0])
    y_ref[0] = y.astype(BF)


def _s5_scan(ug, m, p, q, a_mat, n_chunks, batch):
    g, r, w = ug.shape
    blk = lambda shp: pl.BlockSpec((1,) + shp, lambda i: (i, 0, 0))
    return pl.pallas_call(
        functools.partial(_s5_kernel, n_chunks=n_chunks, batch=batch),
        out_shape=jax.ShapeDtypeStruct((g, r, w), BF),
        grid=(g,),
        in_specs=[blk((r, w)), blk((w, w)), blk((w, w)), blk((w, w)), blk((2, w // 2))],
        out_specs=blk((r, w)),
        scratch_shapes=[pltpu.VMEM((r, w), F32), pltpu.VMEM((r, w), F32)],
        compiler_params=_params(1, VMEM_LIMIT),
        name="s5_scan",
    )(ug, m, p, q, a_mat)


def _na_bias_table(rpb):
    qc = jnp.arange(GRID_W)[:, None]
    kc = jnp.arange(GRID_W)[None, :]
    ws = jnp.clip(qc - NA_WIN_C // 2, 0, GRID_W - NA_WIN_C)
    valid = (kc >= ws) & (kc < ws + NA_WIN_C)
    dc = jnp.clip(kc - qc, -(NA_WIN_C - 1), NA_WIN_C - 1) + (NA_WIN_C - 1)
    e = jnp.where(valid[None, None], rpb.astype(F32)[:, :, dc], MASK_NEG)
    return jnp.concatenate([e[:, :-1], e[:, 1:]], axis=-1)


def _na_kernel(q_ref, k_ref, v_ref, e_ref, o_ref, *, rows):
    hg_w = NA_HEAD_GROUP * NA_HEAD_DIM
    lane_head = lax.broadcasted_iota(jnp.int32, (GRID_W, hg_w), 1) // NA_HEAD_DIM
    n_keys = NA_WIN_R * GRID_W

    def row_body(r, carry):
        rs = jnp.clip(r - NA_WIN_R // 2, 0, rows - NA_WIN_R)
        di = rs - r + (NA_WIN_R - 1)
        q0 = pl.multiple_of(r * GRID_W, GRID_W)
        k0 = pl.multiple_of(rs * GRID_W, GRID_W)
        for hg in range(NA_HEADS // NA_HEAD_GROUP):
            lanes = slice(hg * hg_w, (hg + 1) * hg_w)
            q4 = q_ref[0, pl.ds(q0, GRID_W), lanes]
            qs = jnp.concatenate([jnp.where(lane_head == hh, q4, jnp.zeros_like(q4))
                                  for hh in range(NA_HEAD_GROUP)], axis=0)
            k4 = k_ref[0, pl.ds(k0, n_keys), lanes]
            v4 = v_ref[0, pl.ds(k0, n_keys), lanes]
            s = _dot_nt(qs, k4)
            bias = jnp.concatenate(
                [jnp.concatenate([e_ref[hg * NA_HEAD_GROUP + hh, di + 2 * w2] for w2 in range(NA_WIN_R // 2)], axis=1)
                 for hh in range(NA_HEAD_GROUP)], axis=0)
            s = s + bias
            mx = jnp.max(s, axis=-1, keepdims=True)
            p = jnp.exp(s - mx)
            den = jnp.sum(p, axis=-1, keepdims=True)
            o = _dot(p.astype(BF), v4) * (1.0 / den)
            out = jnp.zeros((GRID_W, hg_w), F32)
            for hh in range(NA_HEAD_GROUP):
                out = jnp.where(lane_head == hh, o[hh * GRID_W:(hh + 1) * GRID_W], out)
            o_ref[0, pl.ds(q0, GRID_W), lanes] = out.astype(BF)
        return carry

    lax.fori_loop(0, rows, row_body, 0)


def _na(q, k, v, table):
    b, t, d = q.shape
    rows = t // GRID_W
    assert rows >= NA_WIN_R and t % GRID_W == 0
    seq = pl.BlockSpec((1, t, d), lambda i: (i, 0, 0))
    return pl.pallas_call(
        functools.partial(_na_kernel, rows=rows),
        out_shape=jax.ShapeDtypeStruct((b, t, d), BF),
        grid=(b,),
        in_specs=[seq, seq, seq, pl.BlockSpec(table.shape, lambda i: (0, 0, 0, 0))],
        out_specs=seq,
        compiler_params=_params(1, VMEM_LIMIT),
        name="na",
    )(q, k, v, table)


def _out_even_kernel(y_ref, u_ref, a_ref, x_ref, d_ref, wg_ref, bg_ref, wo_ref, o_ref):
    ys = y_ref[...].astype(F32) + d_ref[...] * u_ref[...].astype(F32)
    c0 = math.sqrt(2.0 / math.pi)
    ys = 0.5 * ys * (1.0 + jnp.tanh(c0 * (ys + 0.044715 * (ys * ys * ys))))
    gate = _sigmoid(_dot(ys.astype(BF), wg_ref[...]) + bg_ref[...])
    a_out = (ys * gate).astype(BF)
    da = a_out.shape[-1]
    o_ref[...] = x_ref[...] + _dot(a_out, wo_ref[0:da, :]) + _dot(a_ref[...], wo_ref[da:, :])


def _out_even(y, u, att, x2, s5_d, w_glu, b_glu, w_out, tm=512):
    n, dm = x2.shape
    d = y.shape[1]
    row = pl.BlockSpec((tm, d), lambda i: (i, 0))
    full = lambda shp: pl.BlockSpec(shp, lambda i: (0,) * len(shp))
    return pl.pallas_call(
        _out_even_kernel,
        out_shape=jax.ShapeDtypeStruct((n, dm), F32),
        grid=(n // tm,),
        in_specs=[row, row, row, pl.BlockSpec((tm, dm), lambda i: (i, 0)),
                  full((1, d)), full((d, d)), full((1, d)), full(w_out.shape)],
        out_specs=pl.BlockSpec((tm, dm), lambda i: (i, 0)),
        compiler_params=_params(1, VMEM_LIMIT),
        name="out_even",
    )(y, u, att, x2, s5_d[None].astype(F32), w_glu, b_glu[None].astype(F32), w_out)


def _in_odd_kernel(x_ref, g_ref, w_ref, wkt_ref, o_ref, kt_ref, h_ref):
    j = pl.program_id(1)

    @pl.when(j == 0)
    def _():
        h_ref[...] = _rms(x_ref[...], g_ref[...]).astype(BF)

    @pl.when(j != 1)
    def _():
        o_ref[...] = _dot(h_ref[...], w_ref[...]).astype(BF)

    @pl.when(j == 1)
    def _():
        kt_ref[0] = _dot_nt(wkt_ref[...], h_ref[...]).astype(BF)


def _in_odd(x2, gain, w, wkt, batch, tm=1024):
    n, dm = x2.shape
    t = n // batch
    nt = t // tm
    tn = dm
    n_col = w.shape[1] // tn
    return pl.pallas_call(
        _in_odd_kernel,
        out_shape=[jax.ShapeDtypeStruct((n, w.shape[1] - tn), BF), jax.ShapeDtypeStruct((batch, dm, t), BF)],
        grid=(n // tm, n_col),
        in_specs=[pl.BlockSpec((tm, dm), lambda i, j: (i, 0)), pl.BlockSpec((1, dm), lambda i, j: (0, 0)),
                  pl.BlockSpec((dm, tn), lambda i, j: (0, j)), pl.BlockSpec((dm, dm), lambda i, j: (0, 0))],
        out_specs=[pl.BlockSpec((tm, tn), lambda i, j: (i, jnp.maximum(j - 1, 0))),
                   pl.BlockSpec((1, dm, tm), lambda i, j: (i // nt, 0, i % nt))],
        scratch_shapes=[pltpu.VMEM((tm, dm), BF)],
        compiler_params=_params(2, VMEM_LIMIT),
        name="in_odd",
    )(x2, gain[None].astype(F32), w, wkt)


def _ret_kernel(lg_ref, q_ref, kt_ref, v_ref, g_ref, cq_ref, sq_ref, ck_ref, sk_ref, gain_ref, o_ref,
                qr_ref, kr_ref, sb_ref, sf_ref, sbc_ref, *, n_chunks):
    c = RET_CHUNK
    hd = pl.program_id(1)
    lgf = lg_ref[0, hd]
    lgb = lg_ref[1, hd]
    half = RET_DK // 2

    qf = q_ref[0].astype(F32)
    qr_ref[...] = (qf * cq_ref[...] + pltpu.roll(qf, half, 1) * sq_ref[...]).astype(BF)
    k1 = kt_ref[0, 0:half, :].astype(F32)
    k2 = kt_ref[0, half:, :].astype(F32)
    ck = ck_ref[...]
    sk = sk_ref[...]
    scale = RET_DK ** -0.5
    kr_ref[0:half, :] = ((k1 * ck - k2 * sk) * scale).astype(BF)
    kr_ref[half:, :] = ((k1 * sk + k2 * ck) * scale).astype(BF)

    ii = lax.broadcasted_iota(jnp.int32, (c, c), 0)
    jj = lax.broadcasted_iota(jnp.int32, (c, c), 1)
    dist = (ii - jj).astype(F32)
    decay = jnp.where(ii >= jj, jnp.exp(lgf * jnp.maximum(dist, 0.0)), jnp.exp(lgb * jnp.maximum(-dist, 0.0)))
    tok_col = lax.broadcasted_iota(jnp.int32, (c, 1), 0).astype(F32)
    tok_row = lax.broadcasted_iota(jnp.int32, (1, c), 1).astype(F32)
    q_dec_f = jnp.exp(lgf * (tok_col + 1.0))
    q_dec_b = jnp.exp(lgb * (c - tok_col))
    k_dec_f = jnp.exp(lgf * (c - 1.0 - tok_row))
    k_dec_b = jnp.exp(lgb * tok_row)
    chunk_f = jnp.exp(lgf * c)
    chunk_b = jnp.exp(lgb * c)

    sbc_ref[...] = jnp.zeros_like(sbc_ref)

    def back(i, carry):
        ci = n_chunks - 1 - i
        t0 = pl.multiple_of(ci * c, c)
        sb_ref[ci] = sbc_ref[...].astype(BF)
        kd = (kr_ref[:, pl.ds(t0, c)].astype(F32) * k_dec_b).astype(BF)
        sbc_ref[...] = sbc_ref[...] * chunk_b + _dot(kd, v_ref[0, pl.ds(t0, c), :])
        return carry

    lax.fori_loop(0, n_chunks, back, 0, unroll=2)

    sf_ref[...] = jnp.zeros_like(sf_ref)
    gain = gain_ref[...]

    def fwd(ci, carry):
        t0 = pl.multiple_of(ci * c, c)
        qc = qr_ref[pl.ds(t0, c), :]
        kc = kr_ref[:, pl.ds(t0, c)]
        vc = v_ref[0, pl.ds(t0, c), :]
        s = (_dot(qc, kc) * decay).astype(BF)
        qcf = qc.astype(F32)
        qd = jnp.concatenate([(qcf * q_dec_f).astype(BF), (qcf * q_dec_b).astype(BF)], axis=1)
        st = jnp.concatenate([sf_ref[...].astype(BF), sb_ref[ci]], axis=0)
        o = _dot(s, vc) + _dot(qd, st)
        kd = (kc.astype(F32) * k_dec_f).astype(BF)
        sf_ref[...] = sf_ref[...] * chunk_f + _dot(kd, vc)
        o = o * lax.rsqrt(jnp.mean(o * o, axis=-1, keepdims=True) + EPS) * gain
        gt = g_ref[0, pl.ds(t0, c), :].astype(F32)
        o_ref[0, pl.ds(t0, c), :] = (gt * _sigmoid(gt) * o).astype(BF)
        return carry

    lax.fori_loop(0, n_chunks, fwd, 0, unroll=2)


def _retention(qvg, kt, log_gamma, norm_gain, batch):
    n, wq = qvg.shape
    t = n // batch
    assert t % RET_CHUNK == 0
    n_chunks = t // RET_CHUNK
    qvg3 = qvg.reshape(batch, t, wq)
    half = RET_DK // 2
    inv = ROPE_BASE ** (-jnp.arange(half, dtype=F32) / half)
    ang = jnp.arange(t, dtype=F32)[:, None] * inv[None, :]
    cos, sin = jnp.cos(ang), jnp.sin(ang)
    cq = jnp.concatenate([cos, cos], axis=1)
    sq = jnp.concatenate([-sin, sin], axis=1)
    v_off = (RET_HEADS * RET_DK) // RET_DV
    g_off = v_off + RET_HEADS
    const = lambda shp: pl.BlockSpec(shp, lambda b, h, lg: (0,) * len(shp))
    return pl.pallas_call(
        functools.partial(_ret_kernel, n_chunks=n_chunks),
        out_shape=jax.ShapeDtypeStruct((batch, t, RET_HEADS * RET_DV), BF),
        grid_spec=pltpu.PrefetchScalarGridSpec(
            num_scalar_prefetch=1,
            grid=(batch, RET_HEADS),
            in_specs=[pl.BlockSpec((1, t, RET_DK), lambda b, h, lg: (b, 0, h)),
                      pl.BlockSpec((1, RET_DK, t), lambda b, h, lg: (b, h, 0)),
                      pl.BlockSpec((1, t, RET_DV), lambda b, h, lg: (b, 0, v_off + h)),
                      pl.BlockSpec((1, t, RET_DV), lambda b, h, lg: (b, 0, g_off + h)),
                      const((t, RET_DK)), const((t, RET_DK)), const((half, t)), const((half, t)),
                      pl.BlockSpec((1, RET_DV), lambda b, h, lg: (0, h))],
            out_specs=pl.BlockSpec((1, t, RET_DV), lambda b, h, lg: (b, 0, h)),
            scratch_shapes=[pltpu.VMEM((t, RET_DK), BF), pltpu.VMEM((RET_DK, t), BF),
                            pltpu.VMEM((n_chunks, RET_DK, RET_DV), BF),
                            pltpu.VMEM((RET_DK, RET_DV), F32), pltpu.VMEM((RET_DK, RET_DV), F32)]),
        compiler_params=_params(2, VMEM_LIMIT),
        name="retention",
    )(log_gamma, qvg3, kt, qvg3, qvg3, cq, sq, cos.T, sin.T, norm_gain[None].astype(F32))


def _out_odd_kernel(y_ref, x_ref, w_ref, o_ref):
    o_ref[...] = x_ref[...] + _dot(y_ref[...], w_ref[...])


def _out_odd(y, x2, w, tm=512):
    n, dm = x2.shape
    kdim = y.shape[1]
    return pl.pallas_call(
        _out_odd_kernel,
        out_shape=jax.ShapeDtypeStruct((n, dm), F32),
        grid=(n // tm,),
        in_specs=[pl.BlockSpec((tm, kdim), lambda i: (i, 0)), pl.BlockSpec((tm, dm), lambda i: (i, 0)),
                  pl.BlockSpec(w.shape, lambda i: (0, 0))],
        out_specs=pl.BlockSpec((tm, dm), lambda i: (i, 0)),
        compiler_params=_params(1, VMEM_LIMIT),
        name="out_odd",
    )(y, x2, w)


def _router_kernel(x_ref, g_ref, w_ref, b_ref, tri_ref, o_ref, cnt_ref, run_ref):
    @pl.when(pl.program_id(0) == 0)
    def _():
        run_ref[...] = jnp.zeros_like(run_ref)

    h = _rms(x_ref[...], g_ref[...])
    logits = _dot3(h, w_ref[...]) + b_ref[...]
    lane = lax.broadcasted_iota(jnp.int32, logits.shape, 1).astype(F32)
    neg = jnp.float32(-jnp.inf)

    def top(mask):
        val = jnp.max(jnp.where(mask, logits, neg), axis=-1, keepdims=True)
        idx = jnp.min(jnp.where(mask & (logits == val), lane, 1e6), axis=-1, keepdims=True)
        return val, idx

    gmask = lane < MOE_GROUPS
    gmax, gidx = top(gmask)
    gprob = 1.0 / jnp.sum(jnp.where(gmask, jnp.exp(logits - gmax), 0.0), axis=-1, keepdims=True)
    base = MOE_GROUPS + MOE_PER_GROUP * gidx
    emask = (lane >= base) & (lane < base + MOE_PER_GROUP)
    v1, i1 = top(emask)
    v2, i2 = top(emask & (lane != i1))
    e2 = jnp.exp(v2 - v1)
    w1 = gprob / (1.0 + e2)
    w2 = gprob * e2 / (1.0 + e2)
    a1 = i1 - base
    a2 = i2 - base
    lo = jnp.minimum(a1, a2)
    hi = jnp.maximum(a1, a2)
    pair = lo * (7.0 - lo) * 0.5 + hi - lo - 1.0
    cls = gidx * MOE_PAIRS + pair
    w_lo = jnp.where(a1 < a2, w1, w2)
    w_hi = jnp.where(a1 < a2, w2, w1)
    onehot = lane == cls
    earlier = _dot(tri_ref[...], jnp.where(onehot, 1.0, 0.0).astype(BF)) + run_ref[...]
    rank = jnp.sum(jnp.where(onehot, earlier, 0.0), axis=-1, keepdims=True)
    run_ref[...] += jnp.sum(jnp.where(onehot, 1.0, 0.0), axis=0, keepdims=True)
    cnt_ref[...] = run_ref[...]
    o_ref[...] = jnp.where(lane == 0, cls, jnp.where(lane == 1, w_lo, jnp.where(lane == 2, w_hi,
                           jnp.where(lane == 3, rank, 0.0))))


def _router(x2, gain, w_r, b_r, tm=512):
    n, dm = x2.shape
    tri = (jnp.arange(tm)[:, None] > jnp.arange(tm)[None, :]).astype(BF)
    const = lambda shp: pl.BlockSpec(shp, lambda i: (0, 0))
    return pl.pallas_call(
        _router_kernel,
        out_shape=[jax.ShapeDtypeStruct((n, 128), F32), jax.ShapeDtypeStruct((1, 128), F32)],
        grid=(n // tm,),
        in_specs=[pl.BlockSpec((tm, dm), lambda i: (i, 0)), const((1, dm)), const((dm, 128)), const((1, 128)),
                  const((tm, tm))],
        out_specs=[pl.BlockSpec((tm, 128), lambda i: (i, 0)), const((1, 128))],
        scratch_shapes=[pltpu.VMEM((1, 128), F32)],
        compiler_params=_params(1, VMEM_LIMIT),
        name="moe_router",
    )(x2, gain, w_r, b_r, tri)


ROW_DMA_UNROLL = 8


def _start_row_dmas(n, copy_of_row):
    def issue(blk, carry):
        for k in range(ROW_DMA_UNROLL):
            copy_of_row(blk * ROW_DMA_UNROLL + k).start()
        return carry

    lax.fori_loop(0, n // ROW_DMA_UNROLL, issue, 0)


def _dispatch_kernel(fpos_ref, fon_ref, pos_ref, x_ref, info_ref, o_hbm, aug_ref, zero_ref, sem):
    tm, dm = x_ref.shape

    @pl.when(pl.program_id(0) == 0)
    def _():
        zero_ref[...] = jnp.zeros_like(zero_ref)
        n_fill = fpos_ref.shape[0]

        def fill(k):
            return pltpu.make_async_copy(
                zero_ref, o_hbm.at[pl.ds(pl.multiple_of(fpos_ref[k], ROW_ALIGN), MOE_FILL), :], sem)

        for k in range(n_fill - 1):
            pl.when(fon_ref[k] > 0)(lambda k=k: fill(k).start())
        for k in range(n_fill - 1):
            pl.when(fon_ref[k] > 0)(lambda k=k: fill(k).wait())
        fill(n_fill - 1).start()
        fill(n_fill - 1).wait()

    aug_ref[:, 0:dm] = x_ref[...]
    aug_ref[:, dm:] = info_ref[...]
    _start_row_dmas(tm, lambda r: pltpu.make_async_copy(
        aug_ref.at[pl.ds(r, 1), :], o_hbm.at[pl.ds(pos_ref[0, 0, r], 1), :], sem))
    pltpu.make_async_copy(aug_ref, o_hbm.at[pl.ds(0, tm), :], sem).wait()


def _dispatch(x2, info, pos, first_pad, n_rows, tm=256):
    n, dm = x2.shape
    wa = dm + info.shape[1]
    class_fill = first_pad // ROW_ALIGN * ROW_ALIGN
    tail = class_fill[-1] + MOE_FILL + jnp.arange((n_rows - n) // MOE_FILL + 1, dtype=jnp.int32) * MOE_FILL
    tail_on = tail + MOE_FILL <= n_rows
    fpos = jnp.concatenate([class_fill, jnp.where(tail_on, tail, 0), jnp.array([n_rows - MOE_FILL], jnp.int32)])
    fon = jnp.concatenate([jnp.ones_like(class_fill), tail_on.astype(jnp.int32), jnp.ones((1,), jnp.int32)])
    return pl.pallas_call(
        _dispatch_kernel,
        out_shape=jax.ShapeDtypeStruct((n_rows, wa), F32),
        grid_spec=pltpu.PrefetchScalarGridSpec(
            num_scalar_prefetch=2,
            grid=(n // tm,),
            in_specs=[pl.BlockSpec((1, 1, tm), lambda i, fp, fo: (i, 0, 0), memory_space=pltpu.SMEM),
                      pl.BlockSpec((tm, dm), lambda i, fp, fo: (i, 0)),
                      pl.BlockSpec((tm, info.shape[1]), lambda i, fp, fo: (i, 0))],
            out_specs=pl.BlockSpec(memory_space=pl.ANY),
            scratch_shapes=[pltpu.VMEM((tm, wa), F32), pltpu.VMEM((MOE_FILL, wa), F32), pltpu.SemaphoreType.DMA]),
        compiler_params=_params(1, VMEM_LIMIT),
        name="moe_dispatch",
    )(fpos.astype(jnp.int32), fon, pos.reshape(n // tm, 1, tm), x2, info)


def _expert_kernel(ea_ref, eb_ref, act_ref, xa_ref, g_ref,
                   wga_ref, wua_ref, wda_ref, wgb_ref, wub_ref, wdb_ref, o_ref):
    i = pl.program_id(0)
    dm = o_ref.shape[1]

    @pl.when(act_ref[i] > 0)
    def _():
        h = _rms(xa_ref[:, 0:dm], g_ref[...]).astype(BF)

        def ffn(wg, wu, wd, wrow):
            gate = _dot(h, wg[0])
            hid = gate * _sigmoid(gate) * _dot(h, wu[0])
            return _dot((hid * wrow).astype(BF), wd[0])

        o_ref[...] = (ffn(wga_ref, wua_ref, wda_ref, xa_ref[:, dm + 1:dm + 2])
                      + ffn(wgb_ref, wub_ref, wdb_ref, xa_ref[:, dm + 2:dm + 3]))

    @pl.when(act_ref[i] == 0)
    def _():
        o_ref[...] = jnp.zeros_like(o_ref)


def _experts(xa_sorted, gain, tile_ea, tile_eb, tile_act, w_gate, w_up, w_down):
    dm = w_gate.shape[1]
    wa = xa_sorted.shape[1]
    n_tiles = tile_ea.shape[0]
    tm = MOE_TILE
    ff = w_gate.shape[-1]
    wsel = lambda shp, which: pl.BlockSpec((1,) + shp, lambda i, ea, eb, act: ((ea, eb)[which][i], 0, 0))
    return pl.pallas_call(
        _expert_kernel,
        out_shape=jax.ShapeDtypeStruct((n_tiles * tm, dm), F32),
        grid_spec=pltpu.PrefetchScalarGridSpec(
            num_scalar_prefetch=3,
            grid=(n_tiles,),
            in_specs=[pl.BlockSpec((tm, wa), lambda i, ea, eb, act: (i * act[i], 0)),
                      pl.BlockSpec((1, dm), lambda i, ea, eb, act: (0, 0)),
                      wsel((dm, ff), 0), wsel((dm, ff), 0), wsel((ff, dm), 0),
                      wsel((dm, ff), 1), wsel((dm, ff), 1), wsel((ff, dm), 1)],
            out_specs=pl.BlockSpec((tm, dm), lambda i, ea, eb, act: (i, 0))),
        compiler_params=_params(1, VMEM_LIMIT),
        name="moe_experts",
    )(tile_ea, tile_eb, tile_act, xa_sorted, gain, w_gate, w_up, w_down, w_gate, w_up, w_down)


def _combine_kernel(pos_ref, x_ref, y_hbm, o_ref, ybuf, sem):
    tm = ybuf.shape[0]
    _start_row_dmas(tm, lambda r: pltpu.make_async_copy(
        y_hbm.at[pl.ds(pos_ref[0, 0, r], 1), :], ybuf.at[pl.ds(r, 1), :], sem))
    pltpu.make_async_copy(y_hbm.at[pl.ds(0, tm), :], ybuf, sem).wait()
    o_ref[...] = x_ref[...] + ybuf[...]


def _combine(x2, y_sorted, pos, tm=256):
    n, dm = x2.shape
    return pl.pallas_call(
        _combine_kernel,
        out_shape=jax.ShapeDtypeStruct((n, dm), F32),
        grid=(n // tm,),
        in_specs=[pl.BlockSpec((1, 1, tm), lambda i: (i, 0, 0), memory_space=pltpu.SMEM),
                  pl.BlockSpec((tm, dm), lambda i: (i, 0)),
                  pl.BlockSpec(memory_space=pl.ANY)],
        out_specs=pl.BlockSpec((tm, dm), lambda i: (i, 0)),
        scratch_shapes=[pltpu.VMEM((tm, dm), F32), pltpu.SemaphoreType.DMA],
        compiler_params=_params(1, VMEM_LIMIT),
        name="moe_combine",
    )(pos.reshape(n // tm, 1, tm), x2, y_sorted)


def _moe(x2, gain, w_r, b_r, w_gate, w_up, w_down):
    n, dm = x2.shape
    tm = MOE_TILE
    gain = gain[None].astype(F32)
    info, cnt = _router(x2, gain, w_r, b_r)
    cls = info[:, 0].astype(jnp.int32)
    counts = cnt[0, :MOE_CLASSES].astype(jnp.int32)
    padded = ((counts + tm - 1) // tm) * tm
    ends = jnp.cumsum(padded)
    starts = ends - padded
    pos = starts[cls] + info[:, 3].astype(jnp.int32)
    n_tiles = n // tm + MOE_CLASSES
    tile_start = jnp.arange(n_tiles, dtype=jnp.int32) * tm
    tile_cls = jnp.minimum(jnp.sum(tile_start[:, None] >= ends[None, :], axis=1), MOE_CLASSES - 1)
    tile_act = (tile_start < ends[-1]).astype(jnp.int32)
    pair_lo = jnp.array([0, 0, 0, 1, 1, 2], jnp.int32)
    pair_hi = jnp.array([1, 2, 3, 2, 3, 3], jnp.int32)
    grp = tile_cls // MOE_PAIRS
    tile_ea = (grp * MOE_PER_GROUP + pair_lo[tile_cls % MOE_PAIRS]).astype(jnp.int32)
    tile_eb = (grp * MOE_PER_GROUP + pair_hi[tile_cls % MOE_PAIRS]).astype(jnp.int32)
    xa_sorted = _dispatch(x2, info, pos, (starts + counts).astype(jnp.int32), (n_tiles + 2) * tm)
    y_sorted = _experts(xa_sorted, gain, tile_ea, tile_eb, tile_act, w_gate, w_up, w_down)
    return _combine(x2, y_sorted, pos)


def _router_weights(w_group, b_group, w_router, b_router):
    dm = w_group.shape[0]
    w = jnp.concatenate([w_group.astype(F32), w_router.astype(F32).reshape(dm, -1)], axis=1)
    b = jnp.concatenate([b_group.astype(F32), b_router.astype(F32).reshape(-1)])
    pad = 128 - w.shape[1]
    return jnp.pad(w, ((0, 0), (0, pad))), jnp.pad(b, (0, pad))[None]


def _trunk(x, p):
    b, t, dm = x.shape
    n = b * t
    x2 = x.reshape(n, dm)

    u, q, k, v = _in_even(x2, p["norm_mix"][0], p["w_in_even"], p["na_q_gain"], p["na_k_gain"])
    n_chunks = t // S5_CHUNK
    ug = jnp.transpose(u.reshape(b, n_chunks, S5_CHUNK, S5_GROUPS, S5_GROUP), (3, 1, 0, 2, 4))
    ug = ug.reshape(S5_GROUPS, n_chunks * b, S5_CHUNK * S5_GROUP)
    yg = _s5_scan(ug, *p["s5_ops"], n_chunks, b)
    y = jnp.transpose(yg.reshape(S5_GROUPS, n_chunks, b, S5_CHUNK, S5_GROUP), (2, 1, 3, 0, 4)).reshape(n, -1)
    d_att = q.shape[1]
    att = _na(q.reshape(b, t, d_att), k.reshape(b, t, d_att), v.reshape(b, t, d_att), p["na_table"])
    x2 = _out_even(y, u, att.reshape(n, d_att), x2, p["s5_d"], p["s5_w_glu"], p["s5_b_glu"], p["w_out_even"])
    x2 = _moe(x2, p["norm_ffn"][0], *p["router"][0], *p["experts"][0])

    qvg, kt = _in_odd(x2, p["norm_mix"][1], p["w_in_odd"], p["w_k_t"], b)
    yr = _retention(qvg, kt, p["log_gamma"], p["ret_norm_gain"], b)
    x2 = _out_odd(yr.reshape(n, -1), x2, p["w_out_odd"])
    x2 = _moe(x2, p["norm_ffn"][1], *p["router"][1], *p["experts"][1])
    return x2.reshape(b, t, dm)


def kernel(x_prompt, x_sample, norm_mix, norm_ffn, w_in_even, w_out_even, s5_lambda_re, s5_lambda_im, s5_log_dt, s5_b_re, s5_b_im, s5_c_re, s5_c_im, s5_d, s5_w_glu, s5_b_glu, na_q_gain, na_k_gain, na_rpb, w_in_odd, w_out_odd, ret_decay_logit, ret_norm_gain, moe_w_group, moe_b_group, moe_w_router, moe_b_router, moe_w_gate, moe_w_up, moe_w_down):
    assert norm_mix.shape[0] == 2
    dk_all = RET_HEADS * RET_DK
    w_odd = w_in_odd[0].astype(BF)
    p = {
        "norm_mix": norm_mix, "norm_ffn": norm_ffn,
        "w_in_even": w_in_even[0].astype(BF), "w_out_even": w_out_even[0].astype(BF),
        "s5_ops": _s5_operators(s5_lambda_re[0], s5_lambda_im[0], s5_log_dt[0], s5_b_re[0], s5_b_im[0],
                                s5_c_re[0], s5_c_im[0]),
        "s5_d": s5_d[0], "s5_w_glu": s5_w_glu[0].astype(BF), "s5_b_glu": s5_b_glu[0],
        "na_q_gain": na_q_gain[0], "na_k_gain": na_k_gain[0], "na_table": _na_bias_table(na_rpb[0]),
        "w_in_odd": w_odd, "w_k_t": w_odd[:, dk_all:2 * dk_all].T,
        "w_out_odd": w_out_odd[0].astype(BF),
        "log_gamma": jax.nn.log_sigmoid(ret_decay_logit[0].astype(F32)),
        "ret_norm_gain": ret_norm_gain[0],
        "router": [_router_weights(moe_w_group[l], moe_b_group[l], moe_w_router[l], moe_b_router[l]) for l in range(2)],
        "experts": [(moe_w_gate[l].astype(BF), moe_w_up[l].astype(BF), moe_w_down[l].astype(BF)) for l in range(2)],
    }
    return _trunk(x_prompt, p), _trunk(x_sample, p)
```

```python
import functools
import math

import jax
import jax.numpy as jnp
from jax import lax
from jax.experimental import pallas as pl
from jax.experimental.pallas import tpu as pltpu

F32 = jnp.float32
BF = jnp.bfloat16
EPS = 1e-6

D_MODEL = 1024
GRID_W = 64
S5_GROUPS = 32
S5_GROUP = 16
S5_STATE = 64
S5_CHUNK = 16
NA_HEADS = 8
NA_HEAD_DIM = 64
NA_WIN_R = 8
NA_WIN_C = 16
NA_HEAD_GROUP = 4
RET_HEADS = 8
RET_DK = 128
RET_DV = 256
RET_CHUNK = 256
ROPE_BASE = 10000.0
MOE_GROUPS = 4
MOE_PER_GROUP = 4
MOE_FF = 512
MOE_PAIRS = 6
MOE_CLASSES = MOE_GROUPS * MOE_PAIRS
MOE_TILE = 256
ROW_ALIGN = 8
MOE_FILL = MOE_TILE + ROW_ALIGN
MASK_NEG = -1e30
VMEM_LIMIT = 56 * 1024 * 1024


def _params(n_axes, vmem=None):
    return pltpu.CompilerParams(dimension_semantics=("arbitrary",) * n_axes,
                                vmem_limit_bytes=vmem)


def _rms(x, gain):
    ms = jnp.mean(x * x, axis=-1, keepdims=True)
    return x * lax.rsqrt(ms + EPS) * gain


def _sigmoid(x):
    return 1.0 / (1.0 + jnp.exp(-x))


def _dot(a, b):
    return jnp.dot(a, b, preferred_element_type=F32)


def _dot_nt(a, b):
    return lax.dot_general(a, b, (((1,), (1,)), ((), ())), preferred_element_type=F32)


def _split_bf16(x):
    hi = x.astype(BF)
    lo = (x - hi.astype(F32)).astype(BF)
    return hi, lo


def _dot3(a, b):
    ah, al = _split_bf16(a)
    bh, bl = _split_bf16(b)
    return _dot(ah, bh) + _dot(ah, bl) + _dot(al, bh)


def _chunk_major_perm(tm):
    nc = tm // S5_CHUNK
    r = jnp.arange(tm)
    src = (r % nc) * S5_CHUNK + r // nc
    return (src[:, None] == jnp.arange(tm)[None, :]).astype(BF)


def _in_even_kernel(x_ref, g_ref, w_ref, qg_ref, kg_ref, seg_ref, perm_ref, ug_ref, q_ref, k_ref, v_ref):
    h = _rms(x_ref[...], g_ref[...]).astype(BF)
    d = q_ref.shape[-1]
    nc = x_ref.shape[0] // S5_CHUNK
    up = _dot(perm_ref[...], _dot(h, w_ref[:, 0:d]).astype(BF))
    for g in range(S5_GROUPS):
        lanes = slice(g * S5_GROUP, (g + 1) * S5_GROUP)
        ug_ref[g, 0] = jnp.concatenate([up[j * nc:(j + 1) * nc, lanes] for j in range(S5_CHUNK)], axis=1).astype(BF)
    seg = seg_ref[...]

    def head_norm(z, gain):
        ssq = _dot((z * z).astype(BF), seg) * (1.0 / NA_HEAD_DIM)
        return z * lax.rsqrt(ssq + EPS) * gain

    q = _dot(h, w_ref[:, d:2 * d])
    q_ref[...] = (head_norm(q, qg_ref[...]) * (NA_HEAD_DIM ** -0.5)).astype(BF)
    k = _dot(h, w_ref[:, 2 * d:3 * d])
    k_ref[...] = head_norm(k, kg_ref[...]).astype(BF)
    v_ref[...] = _dot(h, w_ref[:, 3 * d:4 * d]).astype(BF)


def _in_even(x2, gain, w, q_gain, k_gain, batch, tm=512):
    n, dm = x2.shape
    d = w.shape[1] // 4
    t = n // batch
    nt = t // tm
    nc = tm // S5_CHUNK
    seg = (jnp.arange(d)[:, None] // NA_HEAD_DIM == jnp.arange(d)[None, :] // NA_HEAD_DIM).astype(BF)
    qg = jnp.tile(q_gain.astype(F32), NA_HEADS)[None]
    kg = jnp.tile(k_gain.astype(F32), NA_HEADS)[None]
    row = pl.BlockSpec((tm, d), lambda i: (i, 0))
    full = lambda shp: pl.BlockSpec(shp, lambda i: (0,) * len(shp))
    gw = S5_CHUNK * S5_GROUP
    perm = _chunk_major_perm(tm)
    return pl.pallas_call(
        _in_even_kernel,
        out_shape=[jax.ShapeDtypeStruct((S5_GROUPS, batch, t // S5_CHUNK, gw), BF)] + [jax.ShapeDtypeStruct((n, d), BF)] * 3,
        grid=(n // tm,),
        in_specs=[pl.BlockSpec((tm, dm), lambda i: (i, 0)), full((1, dm)), full(w.shape),
                  full((1, d)), full((1, d)), full((d, d)), full((tm, tm))],
        out_specs=[pl.BlockSpec((S5_GROUPS, 1, nc, gw), lambda i: (0, i // nt, i % nt, 0)), row, row, row],
        compiler_params=_params(1, VMEM_LIMIT),
        name="in_even",
    )(x2, gain[None].astype(F32), w, qg, kg, seg, perm)


def _s5_taps_kernel(l_ref, r_ref, o_ref):
    o_ref[0] = _dot3(l_ref[0], r_ref[0])


def _s5_taps(lhs, rhs):
    g2, m, k = lhs.shape
    nn = rhs.shape[-1]
    return pl.pallas_call(
        _s5_taps_kernel,
        out_shape=jax.ShapeDtypeStruct((g2, m, nn), F32),
        grid=(g2,),
        in_specs=[pl.BlockSpec((1, m, k), lambda i: (i, 0, 0)), pl.BlockSpec((1, k, nn), lambda i: (i, 0, 0))],
        out_specs=pl.BlockSpec((1, m, nn), lambda i: (i, 0, 0)),
        compiler_params=_params(1),
        name="s5_taps",
    )(lhs, rhs)


def _s5_operators(lam_re, lam_im, log_dt, b_re, b_im, c_re, c_im):
    L, G, H, P_ = S5_CHUNK, S5_GROUPS, S5_GROUP, S5_STATE
    lr, li = lam_re.astype(F32), lam_im.astype(F32)
    dt = jnp.exp(log_dt.astype(F32))[..., None]
    kk = jnp.arange(L + 1, dtype=F32)[:, None, None, None]
    mag = jnp.exp(kk * (lr * dt)[None])
    ang = kk * (li * dt)[None]
    pw = (mag * jnp.cos(ang), mag * jnp.sin(ang))

    def cmul(x, y):
        return x[0] * y[0] - x[1] * y[1], x[0] * y[1] + x[1] * y[0]

    xr, xi = pw[0][1] - 1.0, pw[1][1]
    den = lr * lr + li * li
    coef = ((xr * lr + xi * li) / den, (xi * lr - xr * li) / den)
    bbar = cmul((coef[0][..., None], coef[1][..., None]), (b_re.astype(F32), b_im.astype(F32)))
    c = (c_re.astype(F32), c_im.astype(F32))

    def at(k_slice, direction, perm):
        return tuple(jnp.transpose(z[k_slice, direction], perm) for z in pw)

    apl = tuple(jnp.transpose(z[:L], (1, 2, 0, 3))[:, :, :, None, :] for z in pw)
    ce = cmul((c[0][:, :, None], c[1][:, :, None]), apl)
    lhs = jnp.concatenate([ce[0], -ce[1]], axis=-1).reshape(2 * G, L * H, 2 * P_)
    rhs = jnp.concatenate([bbar[0], bbar[1]], axis=2).reshape(2 * G, 2 * P_, H)
    taps = _s5_taps(lhs, rhs).reshape(2, G, L, H, H)
    taps = jnp.transpose(taps, (0, 1, 2, 4, 3))
    kf, kb = taps[0], taps[1]
    lagged = jnp.concatenate([jnp.flip(kb[:, 1:], axis=1), (kf[:, 0] + kb[:, 0])[:, None], kf[:, 1:]], axis=1)
    idx = jnp.arange(L)[None, :] - jnp.arange(L)[:, None] + (L - 1)
    m = jnp.transpose(lagged[:, idx], (0, 1, 3, 2, 4)).reshape(G, L * H, L * H)

    bt = tuple(jnp.transpose(z, (0, 1, 3, 2)) for z in bbar)
    expand_j = lambda z: z[:, :, None, :]
    pf = cmul(tuple(map(expand_j, at(slice(L - 1, None, -1), 0, (1, 0, 2)))), (bt[0][0][:, None], bt[1][0][:, None]))
    pb = cmul(tuple(map(expand_j, at(slice(0, L), 1, (1, 0, 2)))), (bt[0][1][:, None], bt[1][1][:, None]))
    p = jnp.concatenate([pf[0], pb[0], pf[1], pb[1]], axis=-1).reshape(G, L * H, 4 * P_)

    ct = tuple(jnp.transpose(z, (0, 1, 3, 2)) for z in c)
    expand_h = lambda z: z[..., None]
    gf = cmul(tuple(map(expand_h, at(slice(1, None), 0, (1, 2, 0)))), (ct[0][0][:, :, None, :], ct[1][0][:, :, None, :]))
    gb = cmul(tuple(map(expand_h, at(slice(L, 0, -1), 1, (1, 2, 0)))), (ct[0][1][:, :, None, :], ct[1][1][:, :, None, :]))
    q = jnp.concatenate([gf[0], gb[0], -gf[1], -gb[1]], axis=1).reshape(G, 4 * P_, L * H)

    a_mat = jnp.stack([jnp.concatenate([pw[0][L, 0], pw[0][L, 1]], -1),
                       jnp.concatenate([pw[1][L, 0], pw[1][L, 1]], -1)], axis=1)
    return m.astype(BF), p.astype(BF), q.astype(BF), a_mat.astype(F32)


def _s5_kernel(u_ref, m_ref, p_ref, q_ref, a_ref, d_ref, y_ref, z_ref, s_ref, *, n_chunks, batch):
    u = u_ref[0]
    z_ref[...] = _dot(u, p_ref[0])
    half = 2 * S5_STATE
    ar = a_ref[0, 0:1, :]
    ai = a_ref[0, 1:2, :]
    fwd_lane = lax.broadcasted_iota(jnp.int32, (batch, half), 1) < S5_STATE

    def step(k, carry):
        sr, si = carry
        f0 = pl.multiple_of(k * batch, batch)
        b0 = pl.multiple_of((n_chunks - 1 - k) * batch, batch)
        s_ref[pl.ds(f0, batch), 0:S5_STATE] = sr[:, 0:S5_STATE]
        s_ref[pl.ds(b0, batch), S5_STATE:half] = sr[:, S5_STATE:half]
        s_ref[pl.ds(f0, batch), half:half + S5_STATE] = si[:, 0:S5_STATE]
        s_ref[pl.ds(b0, batch), half + S5_STATE:2 * half] = si[:, S5_STATE:half]
        zr = jnp.where(fwd_lane, z_ref[pl.ds(f0, batch), 0:half], z_ref[pl.ds(b0, batch), 0:half])
        zi = jnp.where(fwd_lane, z_ref[pl.ds(f0, batch), half:2 * half], z_ref[pl.ds(b0, batch), half:2 * half])
        return ar * sr - ai * si + zr, ar * si + ai * sr + zi

    zero = jnp.zeros((batch, half), F32)
    lax.fori_loop(0, n_chunks, step, (zero, zero))
    y = _dot(u, m_ref[0]) + _dot(s_ref[...].astype(BF), q_ref[0]) + u.astype(F32) * d_ref[0]
    y_ref[0] = y.astype(BF)


def _s5_scan(ug, m, p, q, a_mat, d_skip, n_chunks, batch):
    g, r, w = ug.shape
    blk = lambda shp: pl.BlockSpec((1,) + shp, lambda i: (i, 0, 0))
    return pl.pallas_call(
        functools.partial(_s5_kernel, n_chunks=n_chunks, batch=batch),
        out_shape=jax.ShapeDtypeStruct((g, r, w), BF),
        grid=(g,),
        in_specs=[blk((r, w)), blk((w, w)), blk((w, w)), blk((w, w)), blk((2, w // 2)), blk((1, w))],
        out_specs=blk((r, w)),
        scratch_shapes=[pltpu.VMEM((r, w), F32), pltpu.VMEM((r, w), F32)],
        compiler_params=_params(1, VMEM_LIMIT),
        name="s5_scan",
    )(ug, m, p, q, a_mat, d_skip)


def _na_bias_table(rpb):
    qc = jnp.arange(GRID_W)[:, None]
    kc = jnp.arange(GRID_W)[None, :]
    ws = jnp.clip(qc - NA_WIN_C // 2, 0, GRID_W - NA_WIN_C)
    valid = (kc >= ws) & (kc < ws + NA_WIN_C)
    dc = jnp.clip(kc - qc, -(NA_WIN_C - 1), NA_WIN_C - 1) + (NA_WIN_C - 1)
    e = jnp.where(valid[None, None], rpb.astype(F32)[:, :, dc], MASK_NEG)
    return jnp.concatenate([e[:, :-1], e[:, 1:]], axis=-1)


def _na_kernel(q_ref, k_ref, v_ref, e_ref, o_ref, *, rows):
    hg_w = NA_HEAD_GROUP * NA_HEAD_DIM
    lane_head = lax.broadcasted_iota(jnp.int32, (GRID_W, hg_w), 1) // NA_HEAD_DIM
    n_keys = NA_WIN_R * GRID_W

    def row_body(r, carry):
        rs = jnp.clip(r - NA_WIN_R // 2, 0, rows - NA_WIN_R)
        di = rs - r + (NA_WIN_R - 1)
        q0 = pl.multiple_of(r * GRID_W, GRID_W)
        k0 = pl.multiple_of(rs * GRID_W, GRID_W)
        for hg in range(NA_HEADS // NA_HEAD_GROUP):
            lanes = slice(hg * hg_w, (hg + 1) * hg_w)
            q4 = q_ref[0, pl.ds(q0, GRID_W), lanes]
            qs = jnp.concatenate([jnp.where(lane_head == hh, q4, jnp.zeros_like(q4))
                                  for hh in range(NA_HEAD_GROUP)], axis=0)
            k4 = k_ref[0, pl.ds(k0, n_keys), lanes]
            v4 = v_ref[0, pl.ds(k0, n_keys), lanes]
            s = _dot_nt(qs, k4)
            bias = jnp.concatenate(
                [jnp.concatenate([e_ref[hg * NA_HEAD_GROUP + hh, di + 2 * w2] for w2 in range(NA_WIN_R // 2)], axis=1)
                 for hh in range(NA_HEAD_GROUP)], axis=0)
            s = s + bias
            mx = jnp.max(s, axis=-1, keepdims=True)
            p = jnp.exp(s - mx)
            den = jnp.sum(p, axis=-1, keepdims=True)
            o = _dot(p.astype(BF), v4) * (1.0 / den)
            out = jnp.zeros((GRID_W, hg_w), F32)
            for hh in range(NA_HEAD_GROUP):
                out = jnp.where(lane_head == hh, o[hh * GRID_W:(hh + 1) * GRID_W], out)
            o_ref[0, pl.ds(q0, GRID_W), lanes] = out.astype(BF)
        return carry

    lax.fori_loop(0, rows, row_body, 0)


def _na(q, k, v, table):
    b, t, d = q.shape
    rows = t // GRID_W
    assert rows >= NA_WIN_R and t % GRID_W == 0
    seq = pl.BlockSpec((1, t, d), lambda i: (i, 0, 0))
    return pl.pallas_call(
        functools.partial(_na_kernel, rows=rows),
        out_shape=jax.ShapeDtypeStruct((b, t, d), BF),
        grid=(b,),
        in_specs=[seq, seq, seq, pl.BlockSpec(table.shape, lambda i: (0, 0, 0, 0))],
        out_specs=seq,
        compiler_params=_params(1, VMEM_LIMIT),
        name="na",
    )(q, k, v, table)


def _out_even_kernel(yg_ref, a_ref, x_ref, wg_ref, bg_ref, wo_ref, perm_ref, o_ref):
    rows = [jnp.concatenate([yg_ref[g, 0, :, i * S5_GROUP:(i + 1) * S5_GROUP] for g in range(S5_GROUPS)], axis=1)
            for i in range(S5_CHUNK)]
    ys = _dot(perm_ref[...], jnp.concatenate(rows, axis=0))
    c0 = math.sqrt(2.0 / math.pi)
    ys = 0.5 * ys * (1.0 + jnp.tanh(c0 * (ys + 0.044715 * (ys * ys * ys))))
    gate = _sigmoid(_dot(ys.astype(BF), wg_ref[...]) + bg_ref[...])
    a_out = (ys * gate).astype(BF)
    da = a_out.shape[-1]
    o_ref[...] = x_ref[...] + _dot(a_out, wo_ref[0:da, :]) + _dot(a_ref[...], wo_ref[da:, :])


def _out_even(yg, att, x2, w_glu, b_glu, w_out, tm=512):
    n, dm = x2.shape
    d = att.shape[1]
    groups, batch, n_chunks, gw = yg.shape
    nt = (n // batch) // tm
    nc = tm // S5_CHUNK
    row = pl.BlockSpec((tm, d), lambda i: (i, 0))
    full = lambda shp: pl.BlockSpec(shp, lambda i: (0,) * len(shp))
    return pl.pallas_call(
        _out_even_kernel,
        out_shape=jax.ShapeDtypeStruct((n, dm), F32),
        grid=(n // tm,),
        in_specs=[pl.BlockSpec((groups, 1, nc, gw), lambda i: (0, i // nt, i % nt, 0)), row,
                  pl.BlockSpec((tm, dm), lambda i: (i, 0)), full((d, d)), full((1, d)), full(w_out.shape),
                  full((tm, tm))],
        out_specs=pl.BlockSpec((tm, dm), lambda i: (i, 0)),
        compiler_params=_params(1, VMEM_LIMIT),
        name="out_even",
    )(yg, att, x2, w_glu, b_glu[None].astype(F32), w_out, _chunk_major_perm(tm).T)


def _in_odd_kernel(x_ref, g_ref, w_ref, wkt_ref, cq_ref, sq_ref, ck_ref, sk_ref, o_ref, kt_ref):
    h = _rms(x_ref[...], g_ref[...]).astype(BF)
    dk_all = RET_HEADS * RET_DK
    half = RET_DK // 2
    q = _dot(h, w_ref[:, 0:dk_all])
    cq, sq = cq_ref[...], sq_ref[...]
    for hd in range(RET_HEADS):
        lanes = slice(hd * RET_DK, (hd + 1) * RET_DK)
        qh = q[:, lanes]
        o_ref[:, lanes] = (qh * cq + pltpu.roll(qh, half, 1) * sq).astype(BF)
    kt = _dot_nt(wkt_ref[...], h)
    ck, sk = ck_ref[...], sk_ref[...]
    scale = RET_DK ** -0.5
    for hd in range(RET_HEADS):
        k1 = kt[hd * RET_DK:hd * RET_DK + half, :]
        k2 = kt[hd * RET_DK + half:(hd + 1) * RET_DK, :]
        kt_ref[0, hd * RET_DK:hd * RET_DK + half, :] = ((k1 * ck - k2 * sk) * scale).astype(BF)
        kt_ref[0, hd * RET_DK + half:(hd + 1) * RET_DK, :] = ((k1 * sk + k2 * ck) * scale).astype(BF)
    for c0 in range(dk_all, o_ref.shape[1], dk_all):
        o_ref[:, c0:c0 + dk_all] = _dot(h, w_ref[:, c0 + dk_all:c0 + 2 * dk_all]).astype(BF)


def _in_odd(x2, gain, w, wkt, batch, tm=512):
    n, dm = x2.shape
    t = n // batch
    nt = t // tm
    dk_all = RET_HEADS * RET_DK
    half = RET_DK // 2
    inv = ROPE_BASE ** (-jnp.arange(half, dtype=F32) / half)
    ang = jnp.arange(t, dtype=F32)[:, None] * inv[None, :]
    cos, sin = jnp.cos(ang), jnp.sin(ang)
    cq = jnp.concatenate([cos, cos], axis=1)
    sq = jnp.concatenate([-sin, sin], axis=1)
    const = lambda shp: pl.BlockSpec(shp, lambda i: (0,) * len(shp))
    return pl.pallas_call(
        _in_odd_kernel,
        out_shape=[jax.ShapeDtypeStruct((n, w.shape[1] - dk_all), BF), jax.ShapeDtypeStruct((batch, dk_all, t), BF)],
        grid=(n // tm,),
        in_specs=[pl.BlockSpec((tm, dm), lambda i: (i, 0)), const((1, dm)), const(w.shape), const(wkt.shape),
                  pl.BlockSpec((tm, RET_DK), lambda i: (i % nt, 0)), pl.BlockSpec((tm, RET_DK), lambda i: (i % nt, 0)),
                  pl.BlockSpec((half, tm), lambda i: (0, i % nt)), pl.BlockSpec((half, tm), lambda i: (0, i % nt))],
        out_specs=[pl.BlockSpec((tm, w.shape[1] - dk_all), lambda i: (i, 0)),
                   pl.BlockSpec((1, dk_all, tm), lambda i: (i // nt, 0, i % nt))],
        compiler_params=_params(1, VMEM_LIMIT),
        name="in_odd",
    )(x2, gain[None].astype(F32), w, wkt, cq, sq, cos.T, sin.T)


def _ret_kernel(lg_ref, q_ref, kt_ref, v_ref, g_ref, gain_ref, o_ref, sb_ref, sf_ref, sbc_ref, *, n_chunks):
    c = RET_CHUNK
    hd = pl.program_id(1)
    lgf = lg_ref[0, hd]
    lgb = lg_ref[1, hd]

    ii = lax.broadcasted_iota(jnp.int32, (c, c), 0)
    jj = lax.broadcasted_iota(jnp.int32, (c, c), 1)
    dist = (ii - jj).astype(F32)
    decay = jnp.where(ii >= jj, jnp.exp(lgf * jnp.maximum(dist, 0.0)), jnp.exp(lgb * jnp.maximum(-dist, 0.0)))
    tok_col = lax.broadcasted_iota(jnp.int32, (c, 1), 0).astype(F32)
    tok_row = lax.broadcasted_iota(jnp.int32, (1, c), 1).astype(F32)
    q_dec_f = jnp.exp(lgf * (tok_col + 1.0))
    q_dec_b = jnp.exp(lgb * (c - tok_col))
    k_dec_f = jnp.exp(lgf * (c - 1.0 - tok_row))
    k_dec_b = jnp.exp(lgb * tok_row)
    chunk_f = jnp.exp(lgf * c)
    chunk_b = jnp.exp(lgb * c)

    sbc_ref[...] = jnp.zeros_like(sbc_ref)

    def back(i, carry):
        ci = n_chunks - 1 - i
        t0 = pl.multiple_of(ci * c, c)
        sb_ref[ci] = sbc_ref[...].astype(BF)
        kd = (kt_ref[0, :, pl.ds(t0, c)].astype(F32) * k_dec_b).astype(BF)
        sbc_ref[...] = sbc_ref[...] * chunk_b + _dot(kd, v_ref[0, pl.ds(t0, c), :])
        return carry

    lax.fori_loop(0, n_chunks, back, 0, unroll=2)

    sf_ref[...] = jnp.zeros_like(sf_ref)
    gain = gain_ref[...]

    def fwd(ci, carry):
        t0 = pl.multiple_of(ci * c, c)
        qc = q_ref[0, pl.ds(t0, c), :]
        kc = kt_ref[0, :, pl.ds(t0, c)]
        vc = v_ref[0, pl.ds(t0, c), :]
        s = (_dot(qc, kc) * decay).astype(BF)
        qcf = qc.astype(F32)
        qd = jnp.concatenate([(qcf * q_dec_f).astype(BF), (qcf * q_dec_b).astype(BF)], axis=1)
        st = jnp.concatenate([sf_ref[...].astype(BF), sb_ref[ci]], axis=0)
        o = _dot(s, vc) + _dot(qd, st)
        kd = (kc.astype(F32) * k_dec_f).astype(BF)
        sf_ref[...] = sf_ref[...] * chunk_f + _dot(kd, vc)
        o = o * lax.rsqrt(jnp.mean(o * o, axis=-1, keepdims=True) + EPS) * gain
        gt = g_ref[0, pl.ds(t0, c), :].astype(F32)
        o_ref[0, pl.ds(t0, c), :] = (gt * _sigmoid(gt) * o).astype(BF)
        return carry

    lax.fori_loop(0, n_chunks, fwd, 0, unroll=2)


def _retention(qvg, kt, log_gamma, norm_gain, batch):
    n, wq = qvg.shape
    t = n // batch
    assert t % RET_CHUNK == 0
    n_chunks = t // RET_CHUNK
    qvg3 = qvg.reshape(batch, t, wq)
    v_off = (RET_HEADS * RET_DK) // RET_DV
    g_off = v_off + RET_HEADS
    return pl.pallas_call(
        functools.partial(_ret_kernel, n_chunks=n_chunks),
        out_shape=jax.ShapeDtypeStruct((batch, t, RET_HEADS * RET_DV), BF),
        grid_spec=pltpu.PrefetchScalarGridSpec(
            num_scalar_prefetch=1,
            grid=(batch, RET_HEADS),
            in_specs=[pl.BlockSpec((1, t, RET_DK), lambda b, h, lg: (b, 0, h)),
                      pl.BlockSpec((1, RET_DK, t), lambda b, h, lg: (b, h, 0)),
                      pl.BlockSpec((1, t, RET_DV), lambda b, h, lg: (b, 0, v_off + h)),
                      pl.BlockSpec((1, t, RET_DV), lambda b, h, lg: (b, 0, g_off + h)),
                      pl.BlockSpec((1, RET_DV), lambda b, h, lg: (0, h))],
            out_specs=pl.BlockSpec((1, t, RET_DV), lambda b, h, lg: (b, 0, h)),
            scratch_shapes=[pltpu.VMEM((n_chunks, RET_DK, RET_DV), BF),
                            pltpu.VMEM((RET_DK, RET_DV), F32), pltpu.VMEM((RET_DK, RET_DV), F32)]),
        compiler_params=_params(2, VMEM_LIMIT),
        name="retention",
    )(log_gamma, qvg3, kt, qvg3, qvg3, norm_gain[None].astype(F32))


def _out_odd_kernel(y_ref, x_ref, w_ref, o_ref):
    o_ref[...] = x_ref[...] + _dot(y_ref[...], w_ref[...])


def _out_odd(y, x2, w, tm=512):
    n, dm = x2.shape
    kdim = y.shape[1]
    return pl.pallas_call(
        _out_odd_kernel,
        out_shape=jax.ShapeDtypeStruct((n, dm), F32),
        grid=(n // tm,),
        in_specs=[pl.BlockSpec((tm, kdim), lambda i: (i, 0)), pl.BlockSpec((tm, dm), lambda i: (i, 0)),
                  pl.BlockSpec(w.shape, lambda i: (0, 0))],
        out_specs=pl.BlockSpec((tm, dm), lambda i: (i, 0)),
        compiler_params=_params(1, VMEM_LIMIT),
        name="out_odd",
    )(y, x2, w)


def _router_kernel(x_ref, g_ref, w_ref, b_ref, tri_ref, o_ref, cnt_ref, run_ref):
    @pl.when(pl.program_id(0) == 0)
    def _():
        run_ref[...] = jnp.zeros_like(run_ref)

    h = _rms(x_ref[...], g_ref[...])
    logits = _dot3(h, w_ref[...]) + b_ref[...]
    lane = lax.broadcasted_iota(jnp.int32, logits.shape, 1).astype(F32)
    neg = jnp.float32(-jnp.inf)

    def top(mask):
        val = jnp.max(jnp.where(mask, logits, neg), axis=-1, keepdims=True)
        idx = jnp.min(jnp.where(mask & (logits == val), lane, 1e6), axis=-1, keepdims=True)
        return val, idx

    gmask = lane < MOE_GROUPS
    gmax, gidx = top(gmask)
    gprob = 1.0 / jnp.sum(jnp.where(gmask, jnp.exp(logits - gmax), 0.0), axis=-1, keepdims=True)
    base = MOE_GROUPS + MOE_PER_GROUP * gidx
    emask = (lane >= base) & (lane < base + MOE_PER_GROUP)
    v1, i1 = top(emask)
    v2, i2 = top(emask & (lane != i1))
    e2 = jnp.exp(v2 - v1)
    w1 = gprob / (1.0 + e2)
    w2 = gprob * e2 / (1.0 + e2)
    a1 = i1 - base
    a2 = i2 - base
    lo = jnp.minimum(a1, a2)
    hi = jnp.maximum(a1, a2)
    pair = lo * (7.0 - lo) * 0.5 + hi - lo - 1.0
    cls = gidx * MOE_PAIRS + pair
    w_lo = jnp.where(a1 < a2, w1, w2)
    w_hi = jnp.where(a1 < a2, w2, w1)
    onehot = lane == cls
    earlier = _dot(tri_ref[...], jnp.where(onehot, 1.0, 0.0).astype(BF)) + run_ref[...]
    rank = jnp.sum(jnp.where(onehot, earlier, 0.0), axis=-1, keepdims=True)
    run_ref[...] += jnp.sum(jnp.where(onehot, 1.0, 0.0), axis=0, keepdims=True)
    cnt_ref[...] = run_ref[...]
    o_ref[...] = jnp.where(lane == 0, cls, jnp.where(lane == 1, w_lo, jnp.where(lane == 2, w_hi,
                           jnp.where(lane == 3, rank, 0.0))))


def _router(x2, gain, w_r, b_r, tm=512):
    n, dm = x2.shape
    tri = (jnp.arange(tm)[:, None] > jnp.arange(tm)[None, :]).astype(BF)
    const = lambda shp: pl.BlockSpec(shp, lambda i: (0, 0))
    return pl.pallas_call(
        _router_kernel,
        out_shape=[jax.ShapeDtypeStruct((n, 128), F32), jax.ShapeDtypeStruct((1, 128), F32)],
        grid=(n // tm,),
        in_specs=[pl.BlockSpec((tm, dm), lambda i: (i, 0)), const((1, dm)), const((dm, 128)), const((1, 128)),
                  const((tm, tm))],
        out_specs=[pl.BlockSpec((tm, 128), lambda i: (i, 0)), const((1, 128))],
        scratch_shapes=[pltpu.VMEM((1, 128), F32)],
        compiler_params=_params(1, VMEM_LIMIT),
        name="moe_router",
    )(x2, gain, w_r, b_r, tri)


ROW_DMA_UNROLL = 8


def _start_row_dmas(n, copy_of_row):
    def issue(blk, carry):
        for k in range(ROW_DMA_UNROLL):
            copy_of_row(blk * ROW_DMA_UNROLL + k).start()
        return carry

    lax.fori_loop(0, n // ROW_DMA_UNROLL, issue, 0)


def _dispatch_kernel(fpos_ref, fon_ref, pos_ref, x_ref, info_ref, o_hbm, aug_ref, zero_ref, sem, row_sem):
    tm, dm = x_ref.shape

    @pl.when(pl.program_id(0) == 0)
    def _():
        zero_ref[...] = jnp.zeros_like(zero_ref)
        n_fill = fpos_ref.shape[0]

        def fill(k):
            return pltpu.make_async_copy(
                zero_ref, o_hbm.at[pl.ds(pl.multiple_of(fpos_ref[k], ROW_ALIGN), MOE_FILL), :], sem)

        for k in range(n_fill - 1):
            pl.when(fon_ref[k] > 0)(lambda k=k: fill(k).start())
        for k in range(n_fill - 1):
            pl.when(fon_ref[k] > 0)(lambda k=k: fill(k).wait())
        fill(n_fill - 1).start()
        fill(n_fill - 1).wait()

    i = pl.program_id(0)
    n_steps = pl.num_programs(0)
    slot = i % 2

    def wait_rows(s):
        pltpu.make_async_copy(aug_ref.at[s], o_hbm.at[pl.ds(0, tm), :], row_sem.at[s]).wait()

    pl.when(i >= 2)(lambda: wait_rows(slot))
    aug_ref[slot, :, 0:dm] = x_ref[...]
    aug_ref[slot, :, dm:] = info_ref[...]
    _start_row_dmas(tm, lambda r: pltpu.make_async_copy(
        aug_ref.at[slot, pl.ds(r, 1), :], o_hbm.at[pl.ds(pos_ref[0, 0, r], 1), :], row_sem.at[slot]))

    @pl.when(i == n_steps - 1)
    def _():
        pl.when(n_steps >= 2)(lambda: wait_rows(1 - slot))
        wait_rows(slot)


def _dispatch(x2, info, pos, first_pad, n_rows, tm=256):
    n, dm = x2.shape
    wa = dm + info.shape[1]
    class_fill = first_pad // ROW_ALIGN * ROW_ALIGN
    tail = class_fill[-1] + MOE_FILL + jnp.arange((n_rows - n) // MOE_FILL + 1, dtype=jnp.int32) * MOE_FILL
    tail_on = tail + MOE_FILL <= n_rows
    fpos = jnp.concatenate([class_fill, jnp.where(tail_on, tail, 0), jnp.array([n_rows - MOE_FILL], jnp.int32)])
    fon = jnp.concatenate([jnp.ones_like(class_fill), tail_on.astype(jnp.int32), jnp.ones((1,), jnp.int32)])
    return pl.pallas_call(
        _dispatch_kernel,
        out_shape=jax.ShapeDtypeStruct((n_rows, wa), F32),
        grid_spec=pltpu.PrefetchScalarGridSpec(
            num_scalar_prefetch=2,
            grid=(n // tm,),
            in_specs=[pl.BlockSpec((1, 1, tm), lambda i, fp, fo: (i, 0, 0), memory_space=pltpu.SMEM),
                      pl.BlockSpec((tm, dm), lambda i, fp, fo: (i, 0)),
                      pl.BlockSpec((tm, info.shape[1]), lambda i, fp, fo: (i, 0))],
            out_specs=pl.BlockSpec(memory_space=pl.ANY),
            scratch_shapes=[pltpu.VMEM((2, tm, wa), F32), pltpu.VMEM((MOE_FILL, wa), F32),
                            pltpu.SemaphoreType.DMA, pltpu.SemaphoreType.DMA((2,))]),
        compiler_params=_params(1, VMEM_LIMIT),
        name="moe_dispatch",
    )(fpos.astype(jnp.int32), fon, pos.reshape(n // tm, 1, tm), x2, info)


def _expert_kernel(ea_ref, eb_ref, act_ref, xa_ref, g_ref,
                   wga_ref, wua_ref, wda_ref, wgb_ref, wub_ref, wdb_ref, o_ref):
    i = pl.program_id(0)
    dm = o_ref.shape[1]

    @pl.when(act_ref[i] > 0)
    def _():
        h = _rms(xa_ref[:, 0:dm], g_ref[...]).astype(BF)

        def ffn(wg, wu, wd, wrow):
            gate = _dot(h, wg[0])
            hid = gate * _sigmoid(gate) * _dot(h, wu[0])
            return _dot((hid * wrow).astype(BF), wd[0])

        o_ref[...] = (ffn(wga_ref, wua_ref, wda_ref, xa_ref[:, dm + 1:dm + 2])
                      + ffn(wgb_ref, wub_ref, wdb_ref, xa_ref[:, dm + 2:dm + 3]))

    @pl.when(act_ref[i] == 0)
    def _():
        o_ref[...] = jnp.zeros_like(o_ref)


def _experts(xa_sorted, gain, tile_ea, tile_eb, tile_act, w_gate, w_up, w_down):
    dm = w_gate.shape[1]
    wa = xa_sorted.shape[1]
    n_tiles = tile_ea.shape[0]
    tm = MOE_TILE
    ff = w_gate.shape[-1]
    wsel = lambda shp, which: pl.BlockSpec((1,) + shp, lambda i, ea, eb, act: ((ea, eb)[which][i], 0, 0))
    return pl.pallas_call(
        _expert_kernel,
        out_shape=jax.ShapeDtypeStruct((n_tiles * tm, dm), F32),
        grid_spec=pltpu.PrefetchScalarGridSpec(
            num_scalar_prefetch=3,
            grid=(n_tiles,),
            in_specs=[pl.BlockSpec((tm, wa), lambda i, ea, eb, act: (i * act[i], 0)),
                      pl.BlockSpec((1, dm), lambda i, ea, eb, act: (0, 0)),
                      wsel((dm, ff), 0), wsel((dm, ff), 0), wsel((ff, dm), 0),
                      wsel((dm, ff), 1), wsel((dm, ff), 1), wsel((ff, dm), 1)],
            out_specs=pl.BlockSpec((tm, dm), lambda i, ea, eb, act: (i, 0))),
        compiler_params=_params(1, VMEM_LIMIT),
        name="moe_experts",
    )(tile_ea, tile_eb, tile_act, xa_sorted, gain, w_gate, w_up, w_down, w_gate, w_up, w_down)


def _combine_kernel(pos_ref, nxt_ref, x_ref, y_hbm, o_ref, ybuf, sem):
    i = pl.program_id(0)
    n_steps = pl.num_programs(0)
    tm = ybuf.shape[1]
    slot = i % 2

    def gather(idx_ref, s):
        _start_row_dmas(tm, lambda r: pltpu.make_async_copy(
            y_hbm.at[pl.ds(idx_ref[0, 0, r], 1), :], ybuf.at[s, pl.ds(r, 1), :], sem.at[s]))

    @pl.when(i == 0)
    def _():
        gather(pos_ref, 0)

    @pl.when(i + 1 < n_steps)
    def _():
        gather(nxt_ref, 1 - slot)

    pltpu.make_async_copy(y_hbm.at[pl.ds(0, tm), :], ybuf.at[slot], sem.at[slot]).wait()
    o_ref[...] = x_ref[...] + ybuf[slot]


def _combine(x2, y_sorted, pos, tm=256):
    n, dm = x2.shape
    n_steps = n // tm
    pos3 = pos.reshape(n_steps, 1, tm)
    return pl.pallas_call(
        _combine_kernel,
        out_shape=jax.ShapeDtypeStruct((n, dm), F32),
        grid=(n_steps,),
        in_specs=[pl.BlockSpec((1, 1, tm), lambda i: (i, 0, 0), memory_space=pltpu.SMEM),
                  pl.BlockSpec((1, 1, tm), lambda i: (jnp.minimum(i + 1, n_steps - 1), 0, 0), memory_space=pltpu.SMEM),
                  pl.BlockSpec((tm, dm), lambda i: (i, 0)),
                  pl.BlockSpec(memory_space=pl.ANY)],
        out_specs=pl.BlockSpec((tm, dm), lambda i: (i, 0)),
        scratch_shapes=[pltpu.VMEM((2, tm, dm), F32), pltpu.SemaphoreType.DMA((2,))],
        compiler_params=_params(1, VMEM_LIMIT),
        name="moe_combine",
    )(pos3, pos3, x2, y_sorted)


def _moe(x2, gain, w_r, b_r, w_gate, w_up, w_down):
    n, dm = x2.shape
    tm = MOE_TILE
    gain = gain[None].astype(F32)
    info, cnt = _router(x2, gain, w_r, b_r)
    cls = info[:, 0].astype(jnp.int32)
    counts = cnt[0, :MOE_CLASSES].astype(jnp.int32)
    padded = ((counts + tm - 1) // tm) * tm
    ends = jnp.cumsum(padded)
    starts = ends - padded
    pos = starts[cls] + info[:, 3].astype(jnp.int32)
    n_tiles = n // tm + MOE_CLASSES
    tile_start = jnp.arange(n_tiles, dtype=jnp.int32) * tm
    tile_cls = jnp.minimum(jnp.sum(tile_start[:, None] >= ends[None, :], axis=1), MOE_CLASSES - 1)
    tile_act = (tile_start < ends[-1]).astype(jnp.int32)
    pair_lo = jnp.array([0, 0, 0, 1, 1, 2], jnp.int32)
    pair_hi = jnp.array([1, 2, 3, 2, 3, 3], jnp.int32)
    grp = tile_cls // MOE_PAIRS
    tile_ea = (grp * MOE_PER_GROUP + pair_lo[tile_cls % MOE_PAIRS]).astype(jnp.int32)
    tile_eb = (grp * MOE_PER_GROUP + pair_hi[tile_cls % MOE_PAIRS]).astype(jnp.int32)
    xa_sorted = _dispatch(x2, info, pos, (starts + counts).astype(jnp.int32), (n_tiles + 2) * tm)
    y_sorted = _experts(xa_sorted, gain, tile_ea, tile_eb, tile_act, w_gate, w_up, w_down)
    return _combine(x2, y_sorted, pos)


def _router_weights(w_group, b_group, w_router, b_router):
    dm = w_group.shape[0]
    w = jnp.concatenate([w_group.astype(F32), w_router.astype(F32).reshape(dm, -1)], axis=1)
    b = jnp.concatenate([b_group.astype(F32), b_router.astype(F32).reshape(-1)])
    pad = 128 - w.shape[1]
    return jnp.pad(w, ((0, 0), (0, pad))), jnp.pad(b, (0, pad))[None]


def _trunk(x, p):
    b, t, dm = x.shape
    n = b * t
    x2 = x.reshape(n, dm)

    ug, q, k, v = _in_even(x2, p["norm_mix"][0], p["w_in_even"], p["na_q_gain"], p["na_k_gain"], b)
    n_chunks = t // S5_CHUNK
    gw = S5_CHUNK * S5_GROUP
    ug = jnp.swapaxes(ug, 1, 2).reshape(S5_GROUPS, n_chunks * b, gw)
    yg = _s5_scan(ug, *p["s5_ops"], p["s5_d_rows"], n_chunks, b)
    yg = jnp.swapaxes(yg.reshape(S5_GROUPS, n_chunks, b, gw), 1, 2)
    d_att = q.shape[1]
    att = _na(q.reshape(b, t, d_att), k.reshape(b, t, d_att), v.reshape(b, t, d_att), p["na_table"])
    x2 = _out_even(yg, att.reshape(n, d_att), x2, p["s5_w_glu"], p["s5_b_glu"], p["w_out_even"])
    x2 = _moe(x2, p["norm_ffn"][0], *p["router"][0], *p["experts"][0])

    qvg, kt = _in_odd(x2, p["norm_mix"][1], p["w_in_odd"], p["w_k_t"], b)
    yr = _retention(qvg, kt, p["log_gamma"], p["ret_norm_gain"], b)
    x2 = _out_odd(yr.reshape(n, -1), x2, p["w_out_odd"])
    x2 = _moe(x2, p["norm_ffn"][1], *p["router"][1], *p["experts"][1])
    return x2.reshape(b, t, dm)


def kernel(x_prompt, x_sample, norm_mix, norm_ffn, w_in_even, w_out_even, s5_lambda_re, s5_lambda_im, s5_log_dt, s5_b_re, s5_b_im, s5_c_re, s5_c_im, s5_d, s5_w_glu, s5_b_glu, na_q_gain, na_k_gain, na_rpb, w_in_odd, w_out_odd, ret_decay_logit, ret_norm_gain, moe_w_group, moe_b_group, moe_w_router, moe_b_router, moe_w_gate, moe_w_up, moe_w_down):
    assert norm_mix.shape[0] == 2
    dk_all = RET_HEADS * RET_DK
    w_odd = w_in_odd[0].astype(BF)
    p = {
        "norm_mix": norm_mix, "norm_ffn": norm_ffn,
        "w_in_even": w_in_even[0].astype(BF), "w_out_even": w_out_even[0].astype(BF),
        "s5_ops": _s5_operators(s5_lambda_re[0], s5_lambda_im[0], s5_log_dt[0], s5_b_re[0], s5_b_im[0],
                                s5_c_re[0], s5_c_im[0]),
        "s5_d_rows": jnp.tile(s5_d[0].astype(F32).reshape(S5_GROUPS, 1, S5_GROUP), (1, 1, S5_CHUNK)),
        "s5_w_glu": s5_w_glu[0].astype(BF), "s5_b_glu": s5_b_glu[0],
        "na_q_gain": na_q_gain[0], "na_k_gain": na_k_gain[0], "na_table": _na_bias_table(na_rpb[0]),
        "w_in_odd": w_odd, "w_k_t": w_odd[:, dk_all:2 * dk_all].T,
        "w_out_odd": w_out_odd[0].astype(BF),
        "log_gamma": jax.nn.log_sigmoid(ret_decay_logit[0].astype(F32)),
        "ret_norm_gain": ret_norm_gain[0],
        "router": [_router_weights(moe_w_group[l], moe_b_group[l], moe_w_router[l], moe_b_router[l]) for l in range(2)],
        "experts": [(moe_w_gate[l].astype(BF), moe_w_up[l].astype(BF), moe_w_down[l].astype(BF)) for l in range(2)],
    }
    return _trunk(x_prompt, p), _trunk(x_sample, p)
```

```python
import functools
import math

import jax
import jax.numpy as jnp
from jax import lax
from jax.experimental import pallas as pl
from jax.experimental.pallas import tpu as pltpu

F32 = jnp.float32
BF = jnp.bfloat16
EPS = 1e-6

D_MODEL = 1024
GRID_W = 64
S5_GROUPS = 32
S5_GROUP = 16
S5_STATE = 64
S5_CHUNK = 16
NA_HEADS = 8
NA_HEAD_DIM = 64
NA_WIN_R = 8
NA_WIN_C = 16
NA_HEAD_GROUP = 4
RET_HEADS = 8
RET_DK = 128
RET_DV = 256
RET_CHUNK = 256
ROPE_BASE = 10000.0
MOE_GROUPS = 4
MOE_PER_GROUP = 4
MOE_FF = 512
MOE_PAIRS = 6
MOE_CLASSES = MOE_GROUPS * MOE_PAIRS
MOE_TILE = 256
ROW_ALIGN = 8
MOE_FILL = MOE_TILE + ROW_ALIGN
MASK_NEG = -1e30
VMEM_LIMIT = 56 * 1024 * 1024


def _params(n_axes, vmem=None):
    return pltpu.CompilerParams(dimension_semantics=("arbitrary",) * n_axes,
                                vmem_limit_bytes=vmem)


def _rms(x, gain):
    ms = jnp.mean(x * x, axis=-1, keepdims=True)
    return x * lax.rsqrt(ms + EPS) * gain


def _sigmoid(x):
    return 1.0 / (1.0 + jnp.exp(-x))


def _dot(a, b):
    return jnp.dot(a, b, preferred_element_type=F32)


def _dot_nt(a, b):
    return lax.dot_general(a, b, (((1,), (1,)), ((), ())), preferred_element_type=F32)


def _split_bf16(x):
    hi = x.astype(BF)
    lo = (x - hi.astype(F32)).astype(BF)
    return hi, lo


def _dot3(a, b):
    ah, al = _split_bf16(a)
    bh, bl = _split_bf16(b)
    return _dot(ah, bh) + _dot(ah, bl) + _dot(al, bh)


def _chunk_major_perm(tm):
    nc = tm // S5_CHUNK
    r = jnp.arange(tm)
    src = (r % nc) * S5_CHUNK + r // nc
    return (src[:, None] == jnp.arange(tm)[None, :]).astype(BF)


def _in_even_kernel(x_ref, g_ref, w_ref, qg_ref, kg_ref, seg_ref, perm_ref, ug_ref, q_ref, k_ref, v_ref):
    h = _rms(x_ref[...], g_ref[...]).astype(BF)
    d = q_ref.shape[-1]
    nc = x_ref.shape[0] // S5_CHUNK
    up = _dot(perm_ref[...], _dot(h, w_ref[:, 0:d]).astype(BF))
    for g in range(S5_GROUPS):
        lanes = slice(g * S5_GROUP, (g + 1) * S5_GROUP)
        ug_ref[g, 0] = jnp.concatenate([up[j * nc:(j + 1) * nc, lanes] for j in range(S5_CHUNK)], axis=1).astype(BF)
    seg = seg_ref[...]

    def head_norm(z, gain):
        ssq = _dot((z * z).astype(BF), seg) * (1.0 / NA_HEAD_DIM)
        return z * lax.rsqrt(ssq + EPS) * gain

    q = _dot(h, w_ref[:, d:2 * d])
    q_ref[...] = (head_norm(q, qg_ref[...]) * (NA_HEAD_DIM ** -0.5)).astype(BF)
    k = _dot(h, w_ref[:, 2 * d:3 * d])
    k_ref[...] = head_norm(k, kg_ref[...]).astype(BF)
    v_ref[...] = _dot(h, w_ref[:, 3 * d:4 * d]).astype(BF)


def _in_even(x2, gain, w, q_gain, k_gain, batch, tm=512):
    n, dm = x2.shape
    d = w.shape[1] // 4
    t = n // batch
    nt = t // tm
    nc = tm // S5_CHUNK
    seg = (jnp.arange(d)[:, None] // NA_HEAD_DIM == jnp.arange(d)[None, :] // NA_HEAD_DIM).astype(BF)
    qg = jnp.tile(q_gain.astype(F32), NA_HEADS)[None]
    kg = jnp.tile(k_gain.astype(F32), NA_HEADS)[None]
    row = pl.BlockSpec((tm, d), lambda i: (i, 0))
    full = lambda shp: pl.BlockSpec(shp, lambda i: (0,) * len(shp))
    gw = S5_CHUNK * S5_GROUP
    perm = _chunk_major_perm(tm)
    return pl.pallas_call(
        _in_even_kernel,
        out_shape=[jax.ShapeDtypeStruct((S5_GROUPS, batch, t // S5_CHUNK, gw), BF)] + [jax.ShapeDtypeStruct((n, d), BF)] * 3,
        grid=(n // tm,),
        in_specs=[pl.BlockSpec((tm, dm), lambda i: (i, 0)), full((1, dm)), full(w.shape),
                  full((1, d)), full((1, d)), full((d, d)), full((tm, tm))],
        out_specs=[pl.BlockSpec((S5_GROUPS, 1, nc, gw), lambda i: (0, i // nt, i % nt, 0)), row, row, row],
        compiler_params=_params(1, VMEM_LIMIT),
        name="in_even",
    )(x2, gain[None].astype(F32), w, qg, kg, seg, perm)


def _s5_taps_kernel(l_ref, r_ref, o_ref):
    o_ref[0] = _dot3(l_ref[0], r_ref[0])


def _s5_taps(lhs, rhs):
    g2, m, k = lhs.shape
    nn = rhs.shape[-1]
    return pl.pallas_call(
        _s5_taps_kernel,
        out_shape=jax.ShapeDtypeStruct((g2, m, nn), F32),
        grid=(g2,),
        in_specs=[pl.BlockSpec((1, m, k), lambda i: (i, 0, 0)), pl.BlockSpec((1, k, nn), lambda i: (i, 0, 0))],
        out_specs=pl.BlockSpec((1, m, nn), lambda i: (i, 0, 0)),
        compiler_params=_params(1),
        name="s5_taps",
    )(lhs, rhs)


def _s5_operators(lam_re, lam_im, log_dt, b_re, b_im, c_re, c_im):
    L, G, H, P_ = S5_CHUNK, S5_GROUPS, S5_GROUP, S5_STATE
    lr, li = lam_re.astype(F32), lam_im.astype(F32)
    dt = jnp.exp(log_dt.astype(F32))[..., None]
    kk = jnp.arange(L + 1, dtype=F32)[:, None, None, None]
    mag = jnp.exp(kk * (lr * dt)[None])
    ang = kk * (li * dt)[None]
    pw = (mag * jnp.cos(ang), mag * jnp.sin(ang))

    def cmul(x, y):
        return x[0] * y[0] - x[1] * y[1], x[0] * y[1] + x[1] * y[0]

    xr, xi = pw[0][1] - 1.0, pw[1][1]
    den = lr * lr + li * li
    coef = ((xr * lr + xi * li) / den, (xi * lr - xr * li) / den)
    bbar = cmul((coef[0][..., None], coef[1][..., None]), (b_re.astype(F32), b_im.astype(F32)))
    c = (c_re.astype(F32), c_im.astype(F32))

    def at(k_slice, direction, perm):
        return tuple(jnp.transpose(z[k_slice, direction], perm) for z in pw)

    apl = tuple(jnp.transpose(z[:L], (1, 2, 0, 3))[:, :, :, None, :] for z in pw)
    ce = cmul((c[0][:, :, None], c[1][:, :, None]), apl)
    lhs = jnp.concatenate([ce[0], -ce[1]], axis=-1).reshape(2 * G, L * H, 2 * P_)
    rhs = jnp.concatenate([bbar[0], bbar[1]], axis=2).reshape(2 * G, 2 * P_, H)
    taps = _s5_taps(lhs, rhs).reshape(2, G, L, H, H)
    taps = jnp.transpose(taps, (0, 1, 2, 4, 3))
    kf, kb = taps[0], taps[1]
    lagged = jnp.concatenate([jnp.flip(kb[:, 1:], axis=1), (kf[:, 0] + kb[:, 0])[:, None], kf[:, 1:]], axis=1)
    idx = jnp.arange(L)[None, :] - jnp.arange(L)[:, None] + (L - 1)
    m = jnp.transpose(lagged[:, idx], (0, 1, 3, 2, 4)).reshape(G, L * H, L * H)

    bt = tuple(jnp.transpose(z, (0, 1, 3, 2)) for z in bbar)
    expand_j = lambda z: z[:, :, None, :]
    pf = cmul(tuple(map(expand_j, at(slice(L - 1, None, -1), 0, (1, 0, 2)))), (bt[0][0][:, None], bt[1][0][:, None]))
    pb = cmul(tuple(map(expand_j, at(slice(0, L), 1, (1, 0, 2)))), (bt[0][1][:, None], bt[1][1][:, None]))
    p = jnp.concatenate([pf[0], pb[0], pf[1], pb[1]], axis=-1).reshape(G, L * H, 4 * P_)

    ct = tuple(jnp.transpose(z, (0, 1, 3, 2)) for z in c)
    expand_h = lambda z: z[..., None]
    gf = cmul(tuple(map(expand_h, at(slice(1, None), 0, (1, 2, 0)))), (ct[0][0][:, :, None, :], ct[1][0][:, :, None, :]))
    gb = cmul(tuple(map(expand_h, at(slice(L, 0, -1), 1, (1, 2, 0)))), (ct[0][1][:, :, None, :], ct[1][1][:, :, None, :]))
    q = jnp.concatenate([gf[0], gb[0], -gf[1], -gb[1]], axis=1).reshape(G, 4 * P_, L * H)

    a_mat = jnp.stack([jnp.concatenate([pw[0][L, 0], pw[0][L, 1]], -1),
                       jnp.concatenate([pw[1][L, 0], pw[1][L, 1]], -1)], axis=1)
    return m.astype(BF), p.astype(BF), q.astype(BF), a_mat.astype(F32)


def _s5_kernel(u_ref, m_ref, p_ref, q_ref, a_ref, d_ref, y_ref, z_ref, s_ref, *, n_chunks, batch):
    u = u_ref[0]
    z_ref[...] = _dot(u, p_ref[0])
    half = 2 * S5_STATE
    ar = a_ref[0, 0:1, :]
    ai = a_ref[0, 1:2, :]
    fwd_lane = lax.broadcasted_iota(jnp.int32, (batch, half), 1) < S5_STATE

    def step(k, carry):
        sr, si = carry
        f0 = pl.multiple_of(k * batch, batch)
        b0 = pl.multiple_of((n_chunks - 1 - k) * batch, batch)
        s_ref[pl.ds(f0, batch), 0:S5_STATE] = sr[:, 0:S5_STATE]
        s_ref[pl.ds(b0, batch), S5_STATE:half] = sr[:, S5_STATE:half]
        s_ref[pl.ds(f0, batch), half:half + S5_STATE] = si[:, 0:S5_STATE]
        s_ref[pl.ds(b0, batch), half + S5_STATE:2 * half] = si[:, S5_STATE:half]
        zr = jnp.where(fwd_lane, z_ref[pl.ds(f0, batch), 0:half], z_ref[pl.ds(b0, batch), 0:half])
        zi = jnp.where(fwd_lane, z_ref[pl.ds(f0, batch), half:2 * half], z_ref[pl.ds(b0, batch), half:2 * half])
        return ar * sr - ai * si + zr, ar * si + ai * sr + zi

    zero = jnp.zeros((batch, half), F32)
    lax.fori_loop(0, n_chunks, step, (zero, zero), unroll=8)
    y = _dot(u, m_ref[0]) + _dot(s_ref[...].astype(BF), q_ref[0]) + u.astype(F32) * d_ref[0]
    y_ref[0] = y.astype(BF)


def _s5_scan(ug, m, p, q, a_mat, d_skip, n_chunks, batch):
    g, r, w = ug.shape
    blk = lambda shp: pl.BlockSpec((1,) + shp, lambda i: (i, 0, 0))
    return pl.pallas_call(
        functools.partial(_s5_kernel, n_chunks=n_chunks, batch=batch),
        out_shape=jax.ShapeDtypeStruct((g, r, w), BF),
        grid=(g,),
        in_specs=[blk((r, w)), blk((w, w)), blk((w, w)), blk((w, w)), blk((2, w // 2)), blk((1, w))],
        out_specs=blk((r, w)),
        scratch_shapes=[pltpu.VMEM((r, w), F32), pltpu.VMEM((r, w), F32)],
        compiler_params=_params(1, VMEM_LIMIT),
        name="s5_scan",
    )(ug, m, p, q, a_mat, d_skip)


def _na_bias_table(rpb):
    qc = jnp.arange(GRID_W)[:, None]
    kc = jnp.arange(GRID_W)[None, :]
    ws = jnp.clip(qc - NA_WIN_C // 2, 0, GRID_W - NA_WIN_C)
    valid = (kc >= ws) & (kc < ws + NA_WIN_C)
    dc = jnp.clip(kc - qc, -(NA_WIN_C - 1), NA_WIN_C - 1) + (NA_WIN_C - 1)
    e = jnp.where(valid[None, None], rpb.astype(F32)[:, :, dc], MASK_NEG)
    return jnp.concatenate([e[:, :-1], e[:, 1:]], axis=-1)


def _na_kernel(q_ref, k_ref, v_ref, e_ref, o_ref, *, rows):
    hg_w = NA_HEAD_GROUP * NA_HEAD_DIM
    lane_head = lax.broadcasted_iota(jnp.int32, (GRID_W, hg_w), 1) // NA_HEAD_DIM
    n_keys = NA_WIN_R * GRID_W

    def row_body(r, carry):
        rs = jnp.clip(r - NA_WIN_R // 2, 0, rows - NA_WIN_R)
        di = rs - r + (NA_WIN_R - 1)
        q0 = pl.multiple_of(r * GRID_W, GRID_W)
        k0 = pl.multiple_of(rs * GRID_W, GRID_W)
        for hg in range(NA_HEADS // NA_HEAD_GROUP):
            lanes = slice(hg * hg_w, (hg + 1) * hg_w)
            q4 = q_ref[0, pl.ds(q0, GRID_W), lanes]
            qs = jnp.concatenate([jnp.where(lane_head == hh, q4, jnp.zeros_like(q4))
                                  for hh in range(NA_HEAD_GROUP)], axis=0)
            k4 = k_ref[0, pl.ds(k0, n_keys), lanes]
            v4 = v_ref[0, pl.ds(k0, n_keys), lanes]
            s = _dot_nt(qs, k4)
            bias = jnp.concatenate(
                [jnp.concatenate([e_ref[hg * NA_HEAD_GROUP + hh, di + 2 * w2] for w2 in range(NA_WIN_R // 2)], axis=1)
                 for hh in range(NA_HEAD_GROUP)], axis=0)
            s = s + bias
            mx = jnp.max(s, axis=-1, keepdims=True)
            p = jnp.exp(s - mx)
            den = jnp.sum(p, axis=-1, keepdims=True)
            o = _dot(p.astype(BF), v4) * (1.0 / den)
            out = jnp.zeros((GRID_W, hg_w), F32)
            for hh in range(NA_HEAD_GROUP):
                out = jnp.where(lane_head == hh, o[hh * GRID_W:(hh + 1) * GRID_W], out)
            o_ref[0, pl.ds(q0, GRID_W), lanes] = out.astype(BF)
        return carry

    lax.fori_loop(0, rows, row_body, 0, unroll=4)


def _na(q, k, v, table):
    b, t, d = q.shape
    rows = t // GRID_W
    assert rows >= NA_WIN_R and t % GRID_W == 0
    seq = pl.BlockSpec((1, t, d), lambda i: (i, 0, 0))
    return pl.pallas_call(
        functools.partial(_na_kernel, rows=rows),
        out_shape=jax.ShapeDtypeStruct((b, t, d), BF),
        grid=(b,),
        in_specs=[seq, seq, seq, pl.BlockSpec(table.shape, lambda i: (0, 0, 0, 0))],
        out_specs=seq,
        compiler_params=_params(1, VMEM_LIMIT),
        name="na",
    )(q, k, v, table)


def _out_even_kernel(yg_ref, a_ref, x_ref, wg_ref, bg_ref, wo_ref, perm_ref, *route_refs):
    o_ref = route_refs[N_ROUTE_IN]
    rows = [jnp.concatenate([yg_ref[g, 0, :, i * S5_GROUP:(i + 1) * S5_GROUP] for g in range(S5_GROUPS)], axis=1)
            for i in range(S5_CHUNK)]
    ys = _dot(perm_ref[...], jnp.concatenate(rows, axis=0))
    c0 = math.sqrt(2.0 / math.pi)
    ys = 0.5 * ys * (1.0 + jnp.tanh(c0 * (ys + 0.044715 * (ys * ys * ys))))
    gate = _sigmoid(_dot(ys.astype(BF), wg_ref[...]) + bg_ref[...])
    a_out = (ys * gate).astype(BF)
    da = a_out.shape[-1]
    x_new = x_ref[...] + _dot(a_out, wo_ref[0:da, :]) + _dot(a_ref[...], wo_ref[da:, :])
    o_ref[...] = x_new
    _route(x_new, *route_refs[:N_ROUTE_IN], *route_refs[N_ROUTE_IN + 1:])


def _out_even(yg, att, x2, w_glu, b_glu, w_out, router, tm=512):
    n, dm = x2.shape
    d = att.shape[1]
    groups, batch, n_chunks, gw = yg.shape
    nt = (n // batch) // tm
    nc = tm // S5_CHUNK
    row = pl.BlockSpec((tm, d), lambda i: (i, 0))
    full = lambda shp: pl.BlockSpec(shp, lambda i: (0,) * len(shp))
    r_ops, r_in, r_shapes, r_out, r_scratch = _router_io(n, dm, tm, *router)
    return pl.pallas_call(
        _out_even_kernel,
        out_shape=[jax.ShapeDtypeStruct((n, dm), F32)] + r_shapes,
        grid=(n // tm,),
        in_specs=[pl.BlockSpec((groups, 1, nc, gw), lambda i: (0, i // nt, i % nt, 0)), row,
                  pl.BlockSpec((tm, dm), lambda i: (i, 0)), full((d, d)), full((1, d)), full(w_out.shape),
                  full((tm, tm))] + r_in,
        out_specs=[pl.BlockSpec((tm, dm), lambda i: (i, 0))] + r_out,
        scratch_shapes=r_scratch,
        compiler_params=_params(1, VMEM_LIMIT),
        name="out_even",
    )(yg, att, x2, w_glu, b_glu[None].astype(F32), w_out, _chunk_major_perm(tm).T, *r_ops)


def _in_odd_kernel(x_ref, g_ref, w_ref, wkt_ref, cq_ref, sq_ref, ck_ref, sk_ref, o_ref, kt_ref):
    h = _rms(x_ref[...], g_ref[...]).astype(BF)
    dk_all = RET_HEADS * RET_DK
    half = RET_DK // 2
    q = _dot(h, w_ref[:, 0:dk_all])
    cq, sq = cq_ref[...], sq_ref[...]
    for hd in range(RET_HEADS):
        lanes = slice(hd * RET_DK, (hd + 1) * RET_DK)
        qh = q[:, lanes]
        o_ref[:, lanes] = (qh * cq + pltpu.roll(qh, half, 1) * sq).astype(BF)
    kt = _dot_nt(wkt_ref[...], h)
    ck, sk = ck_ref[...], sk_ref[...]
    scale = RET_DK ** -0.5
    for hd in range(RET_HEADS):
        k1 = kt[hd * RET_DK:hd * RET_DK + half, :]
        k2 = kt[hd * RET_DK + half:(hd + 1) * RET_DK, :]
        kt_ref[0, hd * RET_DK:hd * RET_DK + half, :] = ((k1 * ck - k2 * sk) * scale).astype(BF)
        kt_ref[0, hd * RET_DK + half:(hd + 1) * RET_DK, :] = ((k1 * sk + k2 * ck) * scale).astype(BF)
    g0 = dk_all + RET_HEADS * RET_DV
    for c0 in range(dk_all, o_ref.shape[1], dk_all):
        z = _dot(h, w_ref[:, c0 + dk_all:c0 + 2 * dk_all])
        o_ref[:, c0:c0 + dk_all] = (z * _sigmoid(z) if c0 >= g0 else z).astype(BF)


def _in_odd(x2, gain, w, wkt, batch, tm=512):
    n, dm = x2.shape
    t = n // batch
    nt = t // tm
    dk_all = RET_HEADS * RET_DK
    half = RET_DK // 2
    inv = ROPE_BASE ** (-jnp.arange(half, dtype=F32) / half)
    ang = jnp.arange(t, dtype=F32)[:, None] * inv[None, :]
    cos, sin = jnp.cos(ang), jnp.sin(ang)
    cq = jnp.concatenate([cos, cos], axis=1)
    sq = jnp.concatenate([-sin, sin], axis=1)
    const = lambda shp: pl.BlockSpec(shp, lambda i: (0,) * len(shp))
    return pl.pallas_call(
        _in_odd_kernel,
        out_shape=[jax.ShapeDtypeStruct((n, w.shape[1] - dk_all), BF), jax.ShapeDtypeStruct((batch, dk_all, t), BF)],
        grid=(n // tm,),
        in_specs=[pl.BlockSpec((tm, dm), lambda i: (i, 0)), const((1, dm)), const(w.shape), const(wkt.shape),
                  pl.BlockSpec((tm, RET_DK), lambda i: (i % nt, 0)), pl.BlockSpec((tm, RET_DK), lambda i: (i % nt, 0)),
                  pl.BlockSpec((half, tm), lambda i: (0, i % nt)), pl.BlockSpec((half, tm), lambda i: (0, i % nt))],
        out_specs=[pl.BlockSpec((tm, w.shape[1] - dk_all), lambda i: (i, 0)),
                   pl.BlockSpec((1, dk_all, tm), lambda i: (i // nt, 0, i % nt))],
        compiler_params=_params(1, VMEM_LIMIT),
        name="in_odd",
    )(x2, gain[None].astype(F32), w, wkt, cq, sq, cos.T, sin.T)


def _ret_kernel(lg_ref, q_ref, kt_ref, v_ref, g_ref, gain_ref, o_ref, sb_ref, sf_ref, sbc_ref, *, n_chunks):
    c = RET_CHUNK
    hd = pl.program_id(1)
    lgf = lg_ref[0, hd]
    lgb = lg_ref[1, hd]

    ii = lax.broadcasted_iota(jnp.int32, (c, c), 0)
    jj = lax.broadcasted_iota(jnp.int32, (c, c), 1)
    dist = (ii - jj).astype(F32)
    decay = jnp.where(ii >= jj, jnp.exp(lgf * jnp.maximum(dist, 0.0)), jnp.exp(lgb * jnp.maximum(-dist, 0.0)))
    tok_col = lax.broadcasted_iota(jnp.int32, (c, 1), 0).astype(F32)
    tok_row = lax.broadcasted_iota(jnp.int32, (1, c), 1).astype(F32)
    q_dec_f = jnp.exp(lgf * (tok_col + 1.0))
    q_dec_b = jnp.exp(lgb * (c - tok_col))
    k_dec_f = jnp.exp(lgf * (c - 1.0 - tok_row))
    k_dec_b = jnp.exp(lgb * tok_row)
    chunk_f = jnp.exp(lgf * c)
    chunk_b = jnp.exp(lgb * c)

    sbc_ref[...] = jnp.zeros_like(sbc_ref)

    def back(i, carry):
        ci = n_chunks - 1 - i
        t0 = pl.multiple_of(ci * c, c)
        sb_ref[ci] = sbc_ref[...].astype(BF)
        kd = (kt_ref[0, :, pl.ds(t0, c)].astype(F32) * k_dec_b).astype(BF)
        sbc_ref[...] = sbc_ref[...] * chunk_b + _dot(kd, v_ref[0, pl.ds(t0, c), :])
        return carry

    lax.fori_loop(0, n_chunks, back, 0, unroll=4)

    sf_ref[...] = jnp.zeros_like(sf_ref)
    gain = gain_ref[...]

    def fwd(ci, carry):
        t0 = pl.multiple_of(ci * c, c)
        qc = q_ref[0, pl.ds(t0, c), :]
        kc = kt_ref[0, :, pl.ds(t0, c)]
        vc = v_ref[0, pl.ds(t0, c), :]
        s = (_dot(qc, kc) * decay).astype(BF)
        qcf = qc.astype(F32)
        qd = jnp.concatenate([(qcf * q_dec_f).astype(BF), (qcf * q_dec_b).astype(BF)], axis=1)
        st = jnp.concatenate([sf_ref[...].astype(BF), sb_ref[ci]], axis=0)
        o = _dot(s, vc) + _dot(qd, st)
        kd = (kc.astype(F32) * k_dec_f).astype(BF)
        sf_ref[...] = sf_ref[...] * chunk_f + _dot(kd, vc)
        o = o * lax.rsqrt(jnp.mean(o * o, axis=-1, keepdims=True) + EPS) * gain
        o_ref[0, pl.ds(t0, c), :] = (g_ref[0, pl.ds(t0, c), :].astype(F32) * o).astype(BF)
        return carry

    lax.fori_loop(0, n_chunks, fwd, 0, unroll=4)


def _retention(qvg, kt, log_gamma, norm_gain, batch):
    n, wq = qvg.shape
    t = n // batch
    assert t % RET_CHUNK == 0
    n_chunks = t // RET_CHUNK
    qvg3 = qvg.reshape(batch, t, wq)
    v_off = (RET_HEADS * RET_DK) // RET_DV
    g_off = v_off + RET_HEADS
    return pl.pallas_call(
        functools.partial(_ret_kernel, n_chunks=n_chunks),
        out_shape=jax.ShapeDtypeStruct((batch, t, RET_HEADS * RET_DV), BF),
        grid_spec=pltpu.PrefetchScalarGridSpec(
            num_scalar_prefetch=1,
            grid=(batch, RET_HEADS),
            in_specs=[pl.BlockSpec((1, t, RET_DK), lambda b, h, lg: (b, 0, h)),
                      pl.BlockSpec((1, RET_DK, t), lambda b, h, lg: (b, h, 0)),
                      pl.BlockSpec((1, t, RET_DV), lambda b, h, lg: (b, 0, v_off + h)),
                      pl.BlockSpec((1, t, RET_DV), lambda b, h, lg: (b, 0, g_off + h)),
                      pl.BlockSpec((1, RET_DV), lambda b, h, lg: (0, h))],
            out_specs=pl.BlockSpec((1, t, RET_DV), lambda b, h, lg: (b, 0, h)),
            scratch_shapes=[pltpu.VMEM((n_chunks, RET_DK, RET_DV), BF),
                            pltpu.VMEM((RET_DK, RET_DV), F32), pltpu.VMEM((RET_DK, RET_DV), F32)]),
        compiler_params=_params(2, VMEM_LIMIT),
        name="retention",
    )(log_gamma, qvg3, kt, qvg3, qvg3, norm_gain[None].astype(F32))


def _out_odd_kernel(y_ref, x_ref, w_ref, *route_refs):
    o_ref = route_refs[N_ROUTE_IN]
    x_new = x_ref[...] + _dot(y_ref[...], w_ref[...])
    o_ref[...] = x_new
    _route(x_new, *route_refs[:N_ROUTE_IN], *route_refs[N_ROUTE_IN + 1:])


def _out_odd(y, x2, w, router, tm=512):
    n, dm = x2.shape
    kdim = y.shape[1]
    r_ops, r_in, r_shapes, r_out, r_scratch = _router_io(n, dm, tm, *router)
    return pl.pallas_call(
        _out_odd_kernel,
        out_shape=[jax.ShapeDtypeStruct((n, dm), F32)] + r_shapes,
        grid=(n // tm,),
        in_specs=[pl.BlockSpec((tm, kdim), lambda i: (i, 0)), pl.BlockSpec((tm, dm), lambda i: (i, 0)),
                  pl.BlockSpec(w.shape, lambda i: (0, 0))] + r_in,
        out_specs=[pl.BlockSpec((tm, dm), lambda i: (i, 0))] + r_out,
        scratch_shapes=r_scratch,
        compiler_params=_params(1, VMEM_LIMIT),
        name="out_odd",
    )(y, x2, w, *r_ops)


def _route(x, g_ref, wh_ref, wl_ref, b_ref, tri_ref, o_ref, ot_ref, cnt_ref, run_ref):
    @pl.when(pl.program_id(0) == 0)
    def _():
        run_ref[...] = jnp.zeros_like(run_ref)

    h = _rms(x, g_ref[...])
    hh, hl = _split_bf16(h)
    logits = _dot(hh, wh_ref[...]) + _dot(hh, wl_ref[...]) + _dot(hl, wh_ref[...]) + b_ref[...]
    sub = tri_ref.shape[0]
    tri = tri_ref[...]
    run = run_ref[...]
    for r0 in range(0, x.shape[0], sub):
        info, run = _route_rows(logits[r0:r0 + sub], tri, run)
        o_ref[r0:r0 + sub, :] = info
        ot_ref[:, r0:r0 + sub] = info.T[0:ROW_ALIGN, :]
    run_ref[...] = run
    cnt_ref[...] = run


def _route_rows(logits, tri, run):
    lane = lax.broadcasted_iota(jnp.int32, logits.shape, 1).astype(F32)
    neg = jnp.float32(-jnp.inf)

    def top(mask):
        val = jnp.max(jnp.where(mask, logits, neg), axis=-1, keepdims=True)
        idx = jnp.min(jnp.where(mask & (logits == val), lane, 1e6), axis=-1, keepdims=True)
        return val, idx

    gmask = lane < MOE_GROUPS
    gmax, gidx = top(gmask)
    gprob = 1.0 / jnp.sum(jnp.where(gmask, jnp.exp(logits - gmax), 0.0), axis=-1, keepdims=True)
    base = MOE_GROUPS + MOE_PER_GROUP * gidx
    emask = (lane >= base) & (lane < base + MOE_PER_GROUP)
    v1, i1 = top(emask)
    v2, i2 = top(emask & (lane != i1))
    e2 = jnp.exp(v2 - v1)
    w1 = gprob / (1.0 + e2)
    w2 = gprob * e2 / (1.0 + e2)
    a1 = i1 - base
    a2 = i2 - base
    lo = jnp.minimum(a1, a2)
    hi = jnp.maximum(a1, a2)
    pair = lo * (7.0 - lo) * 0.5 + hi - lo - 1.0
    cls = gidx * MOE_PAIRS + pair
    w_lo = jnp.where(a1 < a2, w1, w2)
    w_hi = jnp.where(a1 < a2, w2, w1)
    onehot = lane == cls
    earlier = _dot(tri, jnp.where(onehot, 1.0, 0.0).astype(BF)) + run
    rank = jnp.sum(jnp.where(onehot, earlier, 0.0), axis=-1, keepdims=True)
    info = jnp.where(lane == 0, cls, jnp.where(lane == 1, w_lo, jnp.where(lane == 2, w_hi,
                     jnp.where(lane == 3, rank, 0.0))))
    return info, run + jnp.sum(jnp.where(onehot, 1.0, 0.0), axis=0, keepdims=True)


ROUTE_ROWS = 128
N_ROUTE_IN = 5


def _router_io(n, dm, tm, gain, w_hi, w_lo, b_r):
    tri = (jnp.arange(ROUTE_ROWS)[:, None] > jnp.arange(ROUTE_ROWS)[None, :]).astype(BF)
    const = lambda shp: pl.BlockSpec(shp, lambda i: (0, 0))
    operands = [gain[None].astype(F32), w_hi, w_lo, b_r, tri]
    in_specs = [const((1, dm)), const((dm, 128)), const((dm, 128)), const((1, 128)), const(tri.shape)]
    out_shapes = [jax.ShapeDtypeStruct((n, 128), F32), jax.ShapeDtypeStruct((ROW_ALIGN, n), F32),
                  jax.ShapeDtypeStruct((1, 128), F32)]
    out_specs = [pl.BlockSpec((tm, 128), lambda i: (i, 0)), pl.BlockSpec((ROW_ALIGN, tm), lambda i: (0, i)),
                 const((1, 128))]
    return operands, in_specs, out_shapes, out_specs, [pltpu.VMEM((1, 128), F32)]


ROW_DMA_UNROLL = 8


def _start_row_dmas(n, copy_of_row):
    def issue(blk, carry):
        for k in range(ROW_DMA_UNROLL):
            copy_of_row(blk * ROW_DMA_UNROLL + k).start()
        return carry

    lax.fori_loop(0, n // ROW_DMA_UNROLL, issue, 0)


def _dispatch_kernel(fpos_ref, fon_ref, pos_ref, x_ref, info_ref, o_hbm, aug_ref, zero_ref, sem, row_sem):
    tm, dm = x_ref.shape

    @pl.when(pl.program_id(0) == 0)
    def _():
        zero_ref[...] = jnp.zeros_like(zero_ref)
        n_fill = fpos_ref.shape[0]

        def fill(k):
            return pltpu.make_async_copy(
                zero_ref, o_hbm.at[pl.ds(pl.multiple_of(fpos_ref[k], ROW_ALIGN), MOE_FILL), :], sem)

        for k in range(n_fill - 1):
            pl.when(fon_ref[k] > 0)(lambda k=k: fill(k).start())
        for k in range(n_fill - 1):
            pl.when(fon_ref[k] > 0)(lambda k=k: fill(k).wait())
        fill(n_fill - 1).start()
        fill(n_fill - 1).wait()

    i = pl.program_id(0)
    n_steps = pl.num_programs(0)
    slot = i % 2

    def wait_rows(s):
        pltpu.make_async_copy(aug_ref.at[s], o_hbm.at[pl.ds(0, tm), :], row_sem.at[s]).wait()

    pl.when(i >= 2)(lambda: wait_rows(slot))
    aug_ref[slot, :, 0:dm] = x_ref[...]
    aug_ref[slot, :, dm:] = info_ref[...]
    _start_row_dmas(tm, lambda r: pltpu.make_async_copy(
        aug_ref.at[slot, pl.ds(r, 1), :], o_hbm.at[pl.ds(pos_ref[0, 0, r], 1), :], row_sem.at[slot]))

    @pl.when(i == n_steps - 1)
    def _():
        pl.when(n_steps >= 2)(lambda: wait_rows(1 - slot))
        wait_rows(slot)


def _dispatch(x2, info, pos, first_pad, n_rows, tm=256):
    n, dm = x2.shape
    wa = dm + info.shape[1]
    class_fill = first_pad // ROW_ALIGN * ROW_ALIGN
    tail = class_fill[-1] + MOE_FILL + jnp.arange((n_rows - n) // MOE_FILL + 1, dtype=jnp.int32) * MOE_FILL
    tail_on = tail + MOE_FILL <= n_rows
    fpos = jnp.concatenate([class_fill, jnp.where(tail_on, tail, 0), jnp.array([n_rows - MOE_FILL], jnp.int32)])
    fon = jnp.concatenate([jnp.ones_like(class_fill), tail_on.astype(jnp.int32), jnp.ones((1,), jnp.int32)])
    return pl.pallas_call(
        _dispatch_kernel,
        out_shape=jax.ShapeDtypeStruct((n_rows, wa), F32),
        grid_spec=pltpu.PrefetchScalarGridSpec(
            num_scalar_prefetch=2,
            grid=(n // tm,),
            in_specs=[pl.BlockSpec((1, 1, tm), lambda i, fp, fo: (i, 0, 0), memory_space=pltpu.SMEM),
                      pl.BlockSpec((tm, dm), lambda i, fp, fo: (i, 0)),
                      pl.BlockSpec((tm, info.shape[1]), lambda i, fp, fo: (i, 0))],
            out_specs=pl.BlockSpec(memory_space=pl.ANY),
            scratch_shapes=[pltpu.VMEM((2, tm, wa), F32), pltpu.VMEM((MOE_FILL, wa), F32),
                            pltpu.SemaphoreType.DMA, pltpu.SemaphoreType.DMA((2,))]),
        compiler_params=_params(1, VMEM_LIMIT),
        name="moe_dispatch",
    )(fpos.astype(jnp.int32), fon, pos.reshape(n // tm, 1, tm), x2, info)


def _expert_kernel(ea_ref, eb_ref, act_ref, xa_ref, g_ref,
                   wga_ref, wua_ref, wda_ref, wgb_ref, wub_ref, wdb_ref, o_ref):
    i = pl.program_id(0)
    dm = o_ref.shape[1]

    @pl.when(act_ref[i] > 0)
    def _():
        h = _rms(xa_ref[:, 0:dm], g_ref[...]).astype(BF)

        def ffn(wg, wu, wd, wrow):
            gate = _dot(h, wg[0])
            hid = gate * _sigmoid(gate) * _dot(h, wu[0])
            return _dot((hid * wrow).astype(BF), wd[0])

        o_ref[...] = (ffn(wga_ref, wua_ref, wda_ref, xa_ref[:, dm + 1:dm + 2])
                      + ffn(wgb_ref, wub_ref, wdb_ref, xa_ref[:, dm + 2:dm + 3]))

    @pl.when(act_ref[i] == 0)
    def _():
        o_ref[...] = jnp.zeros_like(o_ref)


def _experts(xa_sorted, gain, tile_ea, tile_eb, tile_act, w_gate, w_up, w_down):
    dm = w_gate.shape[1]
    wa = xa_sorted.shape[1]
    n_tiles = tile_ea.shape[0]
    tm = MOE_TILE
    ff = w_gate.shape[-1]
    wsel = lambda shp, which: pl.BlockSpec((1,) + shp, lambda i, ea, eb, act: ((ea, eb)[which][i], 0, 0))
    return pl.pallas_call(
        _expert_kernel,
        out_shape=jax.ShapeDtypeStruct((n_tiles * tm, dm), F32),
        grid_spec=pltpu.PrefetchScalarGridSpec(
            num_scalar_prefetch=3,
            grid=(n_tiles,),
            in_specs=[pl.BlockSpec((tm, wa), lambda i, ea, eb, act: (i * act[i], 0)),
                      pl.BlockSpec((1, dm), lambda i, ea, eb, act: (0, 0)),
                      wsel((dm, ff), 0), wsel((dm, ff), 0), wsel((ff, dm), 0),
                      wsel((dm, ff), 1), wsel((dm, ff), 1), wsel((ff, dm), 1)],
            out_specs=pl.BlockSpec((tm, dm), lambda i, ea, eb, act: (i, 0))),
        compiler_params=_params(1, VMEM_LIMIT),
        name="moe_experts",
    )(tile_ea, tile_eb, tile_act, xa_sorted, gain, w_gate, w_up, w_down, w_gate, w_up, w_down)


def _combine_kernel(pos_ref, nxt_ref, x_ref, y_hbm, o_ref, ybuf, sem):
    i = pl.program_id(0)
    n_steps = pl.num_programs(0)
    tm = ybuf.shape[1]
    slot = i % 2

    def gather(idx_ref, s):
        _start_row_dmas(tm, lambda r: pltpu.make_async_copy(
            y_hbm.at[pl.ds(idx_ref[0, 0, r], 1), :], ybuf.at[s, pl.ds(r, 1), :], sem.at[s]))

    @pl.when(i == 0)
    def _():
        gather(pos_ref, 0)

    @pl.when(i + 1 < n_steps)
    def _():
        gather(nxt_ref, 1 - slot)

    pltpu.make_async_copy(y_hbm.at[pl.ds(0, tm), :], ybuf.at[slot], sem.at[slot]).wait()
    o_ref[...] = x_ref[...] + ybuf[slot]


def _combine(x2, y_sorted, pos, tm=256):
    n, dm = x2.shape
    n_steps = n // tm
    pos3 = pos.reshape(n_steps, 1, tm)
    return pl.pallas_call(
        _combine_kernel,
        out_shape=jax.ShapeDtypeStruct((n, dm), F32),
        grid=(n_steps,),
        in_specs=[pl.BlockSpec((1, 1, tm), lambda i: (i, 0, 0), memory_space=pltpu.SMEM),
                  pl.BlockSpec((1, 1, tm), lambda i: (jnp.minimum(i + 1, n_steps - 1), 0, 0), memory_space=pltpu.SMEM),
                  pl.BlockSpec((tm, dm), lambda i: (i, 0)),
                  pl.BlockSpec(memory_space=pl.ANY)],
        out_specs=pl.BlockSpec((tm, dm), lambda i: (i, 0)),
        scratch_shapes=[pltpu.VMEM((2, tm, dm), F32), pltpu.SemaphoreType.DMA((2,))],
        compiler_params=_params(1, VMEM_LIMIT),
        name="moe_combine",
    )(pos3, pos3, x2, y_sorted)


def _moe(x2, info, info_rows, cnt, gain, w_gate, w_up, w_down):
    n, dm = x2.shape
    tm = MOE_TILE
    gain = gain[None].astype(F32)
    cls = info_rows[0].astype(jnp.int32)
    counts = cnt[0, :MOE_CLASSES].astype(jnp.int32)
    padded = ((counts + tm - 1) // tm) * tm
    ends = jnp.cumsum(padded)
    starts = ends - padded
    pos = starts[cls] + info_rows[3].astype(jnp.int32)
    n_tiles = n // tm + MOE_CLASSES
    tile_start = jnp.arange(n_tiles, dtype=jnp.int32) * tm
    tile_cls = jnp.minimum(jnp.sum(tile_start[:, None] >= ends[None, :], axis=1), MOE_CLASSES - 1)
    tile_act = (tile_start < ends[-1]).astype(jnp.int32)
    pair_lo = jnp.array([0, 0, 0, 1, 1, 2], jnp.int32)
    pair_hi = jnp.array([1, 2, 3, 2, 3, 3], jnp.int32)
    grp = tile_cls // MOE_PAIRS
    tile_ea = (grp * MOE_PER_GROUP + pair_lo[tile_cls % MOE_PAIRS]).astype(jnp.int32)
    tile_eb = (grp * MOE_PER_GROUP + pair_hi[tile_cls % MOE_PAIRS]).astype(jnp.int32)
    xa_sorted = _dispatch(x2, info, pos, (starts + counts).astype(jnp.int32), (n_tiles + 2) * tm)
    y_sorted = _experts(xa_sorted, gain, tile_ea, tile_eb, tile_act, w_gate, w_up, w_down)
    return _combine(x2, y_sorted, pos)


def _router_weights(w_group, b_group, w_router, b_router):
    dm = w_group.shape[0]
    w = jnp.concatenate([w_group.astype(F32), w_router.astype(F32).reshape(dm, -1)], axis=1)
    b = jnp.concatenate([b_group.astype(F32), b_router.astype(F32).reshape(-1)])
    pad = 128 - w.shape[1]
    w_hi, w_lo = _split_bf16(jnp.pad(w, ((0, 0), (0, pad))))
    return w_hi, w_lo, jnp.pad(b, (0, pad))[None]


def _trunk(x, p):
    b, t, dm = x.shape
    n = b * t
    x2 = x.reshape(n, dm)

    ug, q, k, v = _in_even(x2, p["norm_mix"][0], p["w_in_even"], p["na_q_gain"], p["na_k_gain"], b)
    n_chunks = t // S5_CHUNK
    gw = S5_CHUNK * S5_GROUP
    ug = jnp.swapaxes(ug, 1, 2).reshape(S5_GROUPS, n_chunks * b, gw)
    yg = _s5_scan(ug, *p["s5_ops"], p["s5_d_rows"], n_chunks, b)
    yg = jnp.swapaxes(yg.reshape(S5_GROUPS, n_chunks, b, gw), 1, 2)
    d_att = q.shape[1]
    att = _na(q.reshape(b, t, d_att), k.reshape(b, t, d_att), v.reshape(b, t, d_att), p["na_table"])
    x2, *routing = _out_even(yg, att.reshape(n, d_att), x2, p["s5_w_glu"], p["s5_b_glu"], p["w_out_even"],
                             (p["norm_ffn"][0], *p["router"][0]))
    x2 = _moe(x2, *routing, p["norm_ffn"][0], *p["experts"][0])

    qvg, kt = _in_odd(x2, p["norm_mix"][1], p["w_in_odd"], p["w_k_t"], b)
    yr = _retention(qvg, kt, p["log_gamma"], p["ret_norm_gain"], b)
    x2, *routing = _out_odd(yr.reshape(n, -1), x2, p["w_out_odd"], (p["norm_ffn"][1], *p["router"][1]))
    x2 = _moe(x2, *routing, p["norm_ffn"][1], *p["experts"][1])
    return x2.reshape(b, t, dm)


def kernel(x_prompt, x_sample, norm_mix, norm_ffn, w_in_even, w_out_even, s5_lambda_re, s5_lambda_im, s5_log_dt, s5_b_re, s5_b_im, s5_c_re, s5_c_im, s5_d, s5_w_glu, s5_b_glu, na_q_gain, na_k_gain, na_rpb, w_in_odd, w_out_odd, ret_decay_logit, ret_norm_gain, moe_w_group, moe_b_group, moe_w_router, moe_b_router, moe_w_gate, moe_w_up, moe_w_down):
    assert norm_mix.shape[0] == 2
    dk_all = RET_HEADS * RET_DK
    w_odd = w_in_odd[0].astype(BF)
    p = {
        "norm_mix": norm_mix, "norm_ffn": norm_ffn,
        "w_in_even": w_in_even[0].astype(BF), "w_out_even": w_out_even[0].astype(BF),
        "s5_ops": _s5_operators(s5_lambda_re[0], s5_lambda_im[0], s5_log_dt[0], s5_b_re[0], s5_b_im[0],
                                s5_c_re[0], s5_c_im[0]),
        "s5_d_rows": jnp.tile(s5_d[0].astype(F32).reshape(S5_GROUPS, 1, S5_GROUP), (1, 1, S5_CHUNK)),
        "s5_w_glu": s5_w_glu[0].astype(BF), "s5_b_glu": s5_b_glu[0],
        "na_q_gain": na_q_gain[0], "na_k_gain": na_k_gain[0], "na_table": _na_bias_table(na_rpb[0]),
        "w_in_odd": w_odd, "w_k_t": w_odd[:, dk_all:2 * dk_all].T,
        "w_out_odd": w_out_odd[0].astype(BF),
        "log_gamma": jax.nn.log_sigmoid(ret_decay_logit[0].astype(F32)),
        "ret_norm_gain": ret_norm_gain[0],
        "router": [_router_weights(moe_w_group[l], moe_b_group[l], moe_w_router[l], moe_b_router[l]) for l in range(2)],
        "experts": [(moe_w_gate[l].astype(BF), moe_w_up[l].astype(BF), moe_w_down[l].astype(BF)) for l in range(2)],
    }
    return _trunk(x_prompt, p), _trunk(x_sample, p)
```

```python
import functools
import math

import jax
import jax.numpy as jnp
from jax import lax
from jax.experimental import pallas as pl
from jax.experimental.pallas import tpu as pltpu

F32 = jnp.float32
BF = jnp.bfloat16
EPS = 1e-6

D_MODEL = 1024
GRID_W = 64
S5_GROUPS = 32
S5_GROUP = 16
S5_STATE = 64
S5_CHUNK = 16
NA_HEADS = 8
NA_HEAD_DIM = 64
NA_WIN_R = 8
NA_WIN_C = 16
NA_HEAD_GROUP = 4
RET_HEADS = 8
RET_DK = 128
RET_DV = 256
RET_CHUNK = 256
ROPE_BASE = 10000.0
MOE_GROUPS = 4
MOE_PER_GROUP = 4
MOE_FF = 512
MOE_PAIRS = 6
MOE_CLASSES = MOE_GROUPS * MOE_PAIRS
MOE_TILE = 256
ROW_ALIGN = 8
MOE_FILL = MOE_TILE + ROW_ALIGN
MASK_NEG = -1e30
VMEM_LIMIT = 56 * 1024 * 1024


def _params(n_axes, vmem=None):
    return pltpu.CompilerParams(dimension_semantics=("arbitrary",) * n_axes,
                                vmem_limit_bytes=vmem)


def _rms(x, gain):
    ms = jnp.mean(x * x, axis=-1, keepdims=True)
    return x * lax.rsqrt(ms + EPS) * gain


def _sigmoid(x):
    return 1.0 / (1.0 + jnp.exp(-x))


def _dot(a, b):
    return jnp.dot(a, b, preferred_element_type=F32)


def _dot_nt(a, b):
    return lax.dot_general(a, b, (((1,), (1,)), ((), ())), preferred_element_type=F32)


def _split_bf16(x):
    hi = x.astype(BF)
    lo = (x - hi.astype(F32)).astype(BF)
    return hi, lo


def _dot3(a, b):
    ah, al = _split_bf16(a)
    bh, bl = _split_bf16(b)
    return _dot(ah, bh) + _dot(ah, bl) + _dot(al, bh)


def _chunk_major_perm(tm):
    nc = tm // S5_CHUNK
    r = jnp.arange(tm)
    src = (r % nc) * S5_CHUNK + r // nc
    return (src[:, None] == jnp.arange(tm)[None, :]).astype(BF)


def _in_even_kernel(x_ref, g_ref, w_ref, qg_ref, kg_ref, seg_ref, perm_ref, ug_ref, q_ref, k_ref, v_ref):
    h = _rms(x_ref[...], g_ref[...]).astype(BF)
    d = q_ref.shape[-1]
    nc = x_ref.shape[0] // S5_CHUNK
    up = _dot(perm_ref[...], _dot(h, w_ref[:, 0:d]).astype(BF))
    for g in range(S5_GROUPS):
        lanes = slice(g * S5_GROUP, (g + 1) * S5_GROUP)
        ug_ref[g, 0] = jnp.concatenate([up[j * nc:(j + 1) * nc, lanes] for j in range(S5_CHUNK)], axis=1).astype(BF)
    seg = seg_ref[...]

    def head_norm(z, gain):
        ssq = _dot((z * z).astype(BF), seg) * (1.0 / NA_HEAD_DIM)
        return z * lax.rsqrt(ssq + EPS) * gain

    q = _dot(h, w_ref[:, d:2 * d])
    q_ref[...] = (head_norm(q, qg_ref[...]) * (NA_HEAD_DIM ** -0.5)).astype(BF)
    k = _dot(h, w_ref[:, 2 * d:3 * d])
    k_ref[...] = head_norm(k, kg_ref[...]).astype(BF)
    v_ref[...] = _dot(h, w_ref[:, 3 * d:4 * d]).astype(BF)


def _in_even(x2, gain, w, q_gain, k_gain, batch, tm=512):
    n, dm = x2.shape
    d = w.shape[1] // 4
    t = n // batch
    nt = t // tm
    nc = tm // S5_CHUNK
    seg = (jnp.arange(d)[:, None] // NA_HEAD_DIM == jnp.arange(d)[None, :] // NA_HEAD_DIM).astype(BF)
    qg = jnp.tile(q_gain.astype(F32), NA_HEADS)[None]
    kg = jnp.tile(k_gain.astype(F32), NA_HEADS)[None]
    row = pl.BlockSpec((tm, d), lambda i: (i, 0))
    full = lambda shp: pl.BlockSpec(shp, lambda i: (0,) * len(shp))
    gw = S5_CHUNK * S5_GROUP
    perm = _chunk_major_perm(tm)
    return pl.pallas_call(
        _in_even_kernel,
        out_shape=[jax.ShapeDtypeStruct((S5_GROUPS, batch, t // S5_CHUNK, gw), BF)] + [jax.ShapeDtypeStruct((n, d), BF)] * 3,
        grid=(n // tm,),
        in_specs=[pl.BlockSpec((tm, dm), lambda i: (i, 0)), full((1, dm)), full(w.shape),
                  full((1, d)), full((1, d)), full((d, d)), full((tm, tm))],
        out_specs=[pl.BlockSpec((S5_GROUPS, 1, nc, gw), lambda i: (0, i // nt, i % nt, 0)), row, row, row],
        compiler_params=_params(1, VMEM_LIMIT),
        name="in_even",
    )(x2, gain[None].astype(F32), w, qg, kg, seg, perm)


def _s5_taps_kernel(l_ref, r_ref, o_ref):
    for g in range(l_ref.shape[0]):
        o_ref[g] = _dot3(l_ref[g], r_ref[g])


def _s5_taps(lhs, rhs, per_step=8):
    g2, m, k = lhs.shape
    nn = rhs.shape[-1]
    return pl.pallas_call(
        _s5_taps_kernel,
        out_shape=jax.ShapeDtypeStruct((g2, m, nn), F32),
        grid=(g2 // per_step,),
        in_specs=[pl.BlockSpec((per_step, m, k), lambda i: (i, 0, 0)), pl.BlockSpec((per_step, k, nn), lambda i: (i, 0, 0))],
        out_specs=pl.BlockSpec((per_step, m, nn), lambda i: (i, 0, 0)),
        compiler_params=_params(1),
        name="s5_taps",
    )(lhs, rhs)


def _s5_operators(lam_re, lam_im, log_dt, b_re, b_im, c_re, c_im):
    L, G, H, P_ = S5_CHUNK, S5_GROUPS, S5_GROUP, S5_STATE
    lr, li = lam_re.astype(F32), lam_im.astype(F32)
    dt = jnp.exp(log_dt.astype(F32))[..., None]
    kk = jnp.arange(L + 1, dtype=F32)[:, None, None, None]
    mag = jnp.exp(kk * (lr * dt)[None])
    ang = kk * (li * dt)[None]
    pw = (mag * jnp.cos(ang), mag * jnp.sin(ang))

    def cmul(x, y):
        return x[0] * y[0] - x[1] * y[1], x[0] * y[1] + x[1] * y[0]

    xr, xi = pw[0][1] - 1.0, pw[1][1]
    den = lr * lr + li * li
    coef = ((xr * lr + xi * li) / den, (xi * lr - xr * li) / den)
    bbar = cmul((coef[0][..., None], coef[1][..., None]), (b_re.astype(F32), b_im.astype(F32)))
    c = (c_re.astype(F32), c_im.astype(F32))

    def at(k_slice, direction, perm):
        return tuple(jnp.transpose(z[k_slice, direction], perm) for z in pw)

    apl = tuple(jnp.transpose(z[:L], (1, 2, 0, 3))[:, :, :, None, :] for z in pw)
    ce = cmul((c[0][:, :, None], c[1][:, :, None]), apl)
    lhs = jnp.concatenate([ce[0], -ce[1]], axis=-1).reshape(2 * G, L * H, 2 * P_)
    rhs = jnp.concatenate([bbar[0], bbar[1]], axis=2).reshape(2 * G, 2 * P_, H)
    taps = _s5_taps(lhs, rhs).reshape(2, G, L, H, H)
    taps = jnp.transpose(taps, (0, 1, 2, 4, 3))
    kf, kb = taps[0], taps[1]
    lagged = jnp.concatenate([jnp.flip(kb[:, 1:], axis=1), (kf[:, 0] + kb[:, 0])[:, None], kf[:, 1:]], axis=1)
    idx = jnp.arange(L)[None, :] - jnp.arange(L)[:, None] + (L - 1)
    m = jnp.transpose(lagged[:, idx], (0, 1, 3, 2, 4)).reshape(G, L * H, L * H)

    bt = tuple(jnp.transpose(z, (0, 1, 3, 2)) for z in bbar)
    expand_j = lambda z: z[:, :, None, :]
    pf = cmul(tuple(map(expand_j, at(slice(L - 1, None, -1), 0, (1, 0, 2)))), (bt[0][0][:, None], bt[1][0][:, None]))
    pb = cmul(tuple(map(expand_j, at(slice(0, L), 1, (1, 0, 2)))), (bt[0][1][:, None], bt[1][1][:, None]))
    p = jnp.concatenate([pf[0], pb[0], pf[1], pb[1]], axis=-1).reshape(G, L * H, 4 * P_)

    ct = tuple(jnp.transpose(z, (0, 1, 3, 2)) for z in c)
    expand_h = lambda z: z[..., None]
    gf = cmul(tuple(map(expand_h, at(slice(1, None), 0, (1, 2, 0)))), (ct[0][0][:, :, None, :], ct[1][0][:, :, None, :]))
    gb = cmul(tuple(map(expand_h, at(slice(L, 0, -1), 1, (1, 2, 0)))), (ct[0][1][:, :, None, :], ct[1][1][:, :, None, :]))
    q = jnp.concatenate([gf[0], gb[0], -gf[1], -gb[1]], axis=1).reshape(G, 4 * P_, L * H)

    a_mat = jnp.stack([jnp.concatenate([pw[0][L, 0], pw[0][L, 1]], -1),
                       jnp.concatenate([pw[1][L, 0], pw[1][L, 1]], -1)], axis=1)
    return m.astype(BF), p.astype(BF), q.astype(BF), a_mat.astype(F32)


def _s5_kernel(u_ref, m_ref, p_ref, q_ref, a_ref, d_ref, y_ref, z_ref, s_ref, *, n_chunks, batch):
    u = u_ref[0]
    z_ref[...] = _dot(u, p_ref[0])
    half = 2 * S5_STATE
    ar = a_ref[0, 0:1, :]
    ai = a_ref[0, 1:2, :]
    fwd_lane = lax.broadcasted_iota(jnp.int32, (batch, half), 1) < S5_STATE

    def step(k, carry):
        sr, si = carry
        f0 = pl.multiple_of(k * batch, batch)
        b0 = pl.multiple_of((n_chunks - 1 - k) * batch, batch)
        s_ref[pl.ds(f0, batch), 0:S5_STATE] = sr[:, 0:S5_STATE]
        s_ref[pl.ds(b0, batch), S5_STATE:half] = sr[:, S5_STATE:half]
        s_ref[pl.ds(f0, batch), half:half + S5_STATE] = si[:, 0:S5_STATE]
        s_ref[pl.ds(b0, batch), half + S5_STATE:2 * half] = si[:, S5_STATE:half]
        zr = jnp.where(fwd_lane, z_ref[pl.ds(f0, batch), 0:half], z_ref[pl.ds(b0, batch), 0:half])
        zi = jnp.where(fwd_lane, z_ref[pl.ds(f0, batch), half:2 * half], z_ref[pl.ds(b0, batch), half:2 * half])
        return ar * sr - ai * si + zr, ar * si + ai * sr + zi

    zero = jnp.zeros((batch, half), F32)
    lax.fori_loop(0, n_chunks, step, (zero, zero), unroll=8)
    y = _dot(u, m_ref[0]) + _dot(s_ref[...].astype(BF), q_ref[0]) + u.astype(F32) * d_ref[0]
    y_ref[0] = y.astype(BF)


def _s5_scan(ug, m, p, q, a_mat, d_skip, n_chunks, batch):
    g, r, w = ug.shape
    blk = lambda shp: pl.BlockSpec((1,) + shp, lambda i: (i, 0, 0))
    return pl.pallas_call(
        functools.partial(_s5_kernel, n_chunks=n_chunks, batch=batch),
        out_shape=jax.ShapeDtypeStruct((g, r, w), BF),
        grid=(g,),
        in_specs=[blk((r, w)), blk((w, w)), blk((w, w)), blk((w, w)), blk((2, w // 2)), blk((1, w))],
        out_specs=blk((r, w)),
        scratch_shapes=[pltpu.VMEM((r, w), F32), pltpu.VMEM((r, w), F32)],
        compiler_params=_params(1, VMEM_LIMIT),
        name="s5_scan",
    )(ug, m, p, q, a_mat, d_skip)


def _na_bias_table(rpb):
    qc = jnp.arange(GRID_W)[:, None]
    kc = jnp.arange(GRID_W)[None, :]
    ws = jnp.clip(qc - NA_WIN_C // 2, 0, GRID_W - NA_WIN_C)
    valid = (kc >= ws) & (kc < ws + NA_WIN_C)
    n_tap = 2 * NA_WIN_C - 1
    pad = GRID_W - NA_WIN_C
    row = n_tap + 2 * pad
    taps = jnp.pad(rpb.astype(F32), ((0, 0), (0, 0), (pad, pad)))
    skew = jnp.tile(taps, (1, 1, GRID_W))[:, :, GRID_W - 1:GRID_W - 1 + GRID_W * (row - 1)]
    bias = skew.reshape(rpb.shape[0], rpb.shape[1], GRID_W, row - 1)[..., :GRID_W]
    e = jnp.where(valid[None, None], bias, MASK_NEG)
    return jnp.concatenate([e[:, :-1], e[:, 1:]], axis=-1)


def _na_kernel(q_ref, k_ref, v_ref, e_ref, o_ref, *, rows):
    hg_w = NA_HEAD_GROUP * NA_HEAD_DIM
    lane_head = lax.broadcasted_iota(jnp.int32, (GRID_W, hg_w), 1) // NA_HEAD_DIM
    n_keys = NA_WIN_R * GRID_W

    def row_body(r, carry):
        rs = jnp.clip(r - NA_WIN_R // 2, 0, rows - NA_WIN_R)
        di = rs - r + (NA_WIN_R - 1)
        q0 = pl.multiple_of(r * GRID_W, GRID_W)
        k0 = pl.multiple_of(rs * GRID_W, GRID_W)
        for hg in range(NA_HEADS // NA_HEAD_GROUP):
            lanes = slice(hg * hg_w, (hg + 1) * hg_w)
            q4 = q_ref[0, pl.ds(q0, GRID_W), lanes]
            qs = jnp.concatenate([jnp.where(lane_head == hh, q4, jnp.zeros_like(q4))
                                  for hh in range(NA_HEAD_GROUP)], axis=0)
            k4 = k_ref[0, pl.ds(k0, n_keys), lanes]
            v4 = v_ref[0, pl.ds(k0, n_keys), lanes]
            s = _dot_nt(qs, k4)
            bias = jnp.concatenate(
                [jnp.concatenate([e_ref[hg * NA_HEAD_GROUP + hh, di + 2 * w2] for w2 in range(NA_WIN_R // 2)], axis=1)
                 for hh in range(NA_HEAD_GROUP)], axis=0)
            s = s + bias
            mx = jnp.max(s, axis=-1, keepdims=True)
            p = jnp.exp(s - mx)
            den = jnp.sum(p, axis=-1, keepdims=True)
            o = _dot(p.astype(BF), v4) * (1.0 / den)
            out = jnp.zeros((GRID_W, hg_w), F32)
            for hh in range(NA_HEAD_GROUP):
                out = jnp.where(lane_head == hh, o[hh * GRID_W:(hh + 1) * GRID_W], out)
            o_ref[0, pl.ds(q0, GRID_W), lanes] = out.astype(BF)
        return carry

    lax.fori_loop(0, rows, row_body, 0, unroll=4)


def _na(q, k, v, table):
    b, t, d = q.shape
    rows = t // GRID_W
    assert rows >= NA_WIN_R and t % GRID_W == 0
    seq = pl.BlockSpec((1, t, d), lambda i: (i, 0, 0))
    return pl.pallas_call(
        functools.partial(_na_kernel, rows=rows),
        out_shape=jax.ShapeDtypeStruct((b, t, d), BF),
        grid=(b,),
        in_specs=[seq, seq, seq, pl.BlockSpec(table.shape, lambda i: (0, 0, 0, 0))],
        out_specs=seq,
        compiler_params=_params(1, VMEM_LIMIT),
        name="na",
    )(q, k, v, table)


def _out_even_kernel(yg_ref, a_ref, x_ref, wg_ref, bg_ref, wo_ref, perm_ref, *route_refs):
    o_ref = route_refs[N_ROUTE_IN]
    rows = [jnp.concatenate([yg_ref[g, 0, :, i * S5_GROUP:(i + 1) * S5_GROUP] for g in range(S5_GROUPS)], axis=1)
            for i in range(S5_CHUNK)]
    ys = _dot(perm_ref[...], jnp.concatenate(rows, axis=0))
    c0 = math.sqrt(2.0 / math.pi)
    ys = 0.5 * ys * (1.0 + jnp.tanh(c0 * (ys + 0.044715 * (ys * ys * ys))))
    gate = _sigmoid(_dot(ys.astype(BF), wg_ref[...]) + bg_ref[...])
    a_out = (ys * gate).astype(BF)
    da = a_out.shape[-1]
    x_new = x_ref[...] + _dot(a_out, wo_ref[0:da, :]) + _dot(a_ref[...], wo_ref[da:, :])
    o_ref[...] = x_new
    _route(x_new, *route_refs[:N_ROUTE_IN], *route_refs[N_ROUTE_IN + 1:])


def _out_even(yg, att, x2, w_glu, b_glu, w_out, router, tm=512):
    n, dm = x2.shape
    d = att.shape[1]
    groups, batch, n_chunks, gw = yg.shape
    nt = (n // batch) // tm
    nc = tm // S5_CHUNK
    row = pl.BlockSpec((tm, d), lambda i: (i, 0))
    full = lambda shp: pl.BlockSpec(shp, lambda i: (0,) * len(shp))
    r_ops, r_in, r_shapes, r_out, r_scratch = _router_io(n, dm, tm, *router)
    return pl.pallas_call(
        _out_even_kernel,
        out_shape=[jax.ShapeDtypeStruct((n, dm), F32)] + r_shapes,
        grid=(n // tm,),
        in_specs=[pl.BlockSpec((groups, 1, nc, gw), lambda i: (0, i // nt, i % nt, 0)), row,
                  pl.BlockSpec((tm, dm), lambda i: (i, 0)), full((d, d)), full((1, d)), full(w_out.shape),
                  full((tm, tm))] + r_in,
        out_specs=[pl.BlockSpec((tm, dm), lambda i: (i, 0))] + r_out,
        scratch_shapes=r_scratch,
        compiler_params=_params(1, VMEM_LIMIT),
        name="out_even",
    )(yg, att, x2, w_glu, b_glu[None].astype(F32), w_out, _chunk_major_perm(tm).T, *r_ops)


def _in_odd_kernel(x_ref, g_ref, w_ref, wkt_ref, cq_ref, sq_ref, ck_ref, sk_ref, o_ref, kt_ref):
    h = _rms(x_ref[...], g_ref[...]).astype(BF)
    dk_all = RET_HEADS * RET_DK
    half = RET_DK // 2
    q = _dot(h, w_ref[:, 0:dk_all])
    cq, sq = cq_ref[...], sq_ref[...]
    for hd in range(RET_HEADS):
        lanes = slice(hd * RET_DK, (hd + 1) * RET_DK)
        qh = q[:, lanes]
        o_ref[:, lanes] = (qh * cq + pltpu.roll(qh, half, 1) * sq).astype(BF)
    kt = _dot_nt(wkt_ref[...], h)
    ck, sk = ck_ref[...], sk_ref[...]
    scale = RET_DK ** -0.5
    for hd in range(RET_HEADS):
        k1 = kt[hd * RET_DK:hd * RET_DK + half, :]
        k2 = kt[hd * RET_DK + half:(hd + 1) * RET_DK, :]
        kt_ref[0, hd * RET_DK:hd * RET_DK + half, :] = ((k1 * ck - k2 * sk) * scale).astype(BF)
        kt_ref[0, hd * RET_DK + half:(hd + 1) * RET_DK, :] = ((k1 * sk + k2 * ck) * scale).astype(BF)
    g0 = dk_all + RET_HEADS * RET_DV
    for c0 in range(dk_all, o_ref.shape[1], dk_all):
        z = _dot(h, w_ref[:, c0 + dk_all:c0 + 2 * dk_all])
        o_ref[:, c0:c0 + dk_all] = (z * _sigmoid(z) if c0 >= g0 else z).astype(BF)


def _in_odd(x2, gain, w, wkt, batch, tm=512):
    n, dm = x2.shape
    t = n // batch
    nt = t // tm
    dk_all = RET_HEADS * RET_DK
    half = RET_DK // 2
    inv = ROPE_BASE ** (-jnp.arange(half, dtype=F32) / half)
    ang = jnp.arange(t, dtype=F32)[:, None] * inv[None, :]
    cos, sin = jnp.cos(ang), jnp.sin(ang)
    cq = jnp.concatenate([cos, cos], axis=1)
    sq = jnp.concatenate([-sin, sin], axis=1)
    const = lambda shp: pl.BlockSpec(shp, lambda i: (0,) * len(shp))
    return pl.pallas_call(
        _in_odd_kernel,
        out_shape=[jax.ShapeDtypeStruct((n, w.shape[1] - dk_all), BF), jax.ShapeDtypeStruct((batch, dk_all, t), BF)],
        grid=(n // tm,),
        in_specs=[pl.BlockSpec((tm, dm), lambda i: (i, 0)), const((1, dm)), const(w.shape), const(wkt.shape),
                  pl.BlockSpec((tm, RET_DK), lambda i: (i % nt, 0)), pl.BlockSpec((tm, RET_DK), lambda i: (i % nt, 0)),
                  pl.BlockSpec((half, tm), lambda i: (0, i % nt)), pl.BlockSpec((half, tm), lambda i: (0, i % nt))],
        out_specs=[pl.BlockSpec((tm, w.shape[1] - dk_all), lambda i: (i, 0)),
                   pl.BlockSpec((1, dk_all, tm), lambda i: (i // nt, 0, i % nt))],
        compiler_params=_params(1, VMEM_LIMIT),
        name="in_odd",
    )(x2, gain[None].astype(F32), w, wkt, cq, sq, cos.T, sin.T)


def _ret_kernel(lg_ref, q_ref, kt_ref, v_ref, g_ref, gain_ref, o_ref, sb_ref, sf_ref, sbc_ref, *, n_chunks):
    c = RET_CHUNK
    hd = pl.program_id(1)
    lgf = lg_ref[0, hd]
    lgb = lg_ref[1, hd]

    ii = lax.broadcasted_iota(jnp.int32, (c, c), 0)
    jj = lax.broadcasted_iota(jnp.int32, (c, c), 1)
    dist = (ii - jj).astype(F32)
    decay = jnp.where(ii >= jj, jnp.exp(lgf * jnp.maximum(dist, 0.0)), jnp.exp(lgb * jnp.maximum(-dist, 0.0)))
    tok_col = lax.broadcasted_iota(jnp.int32, (c, 1), 0).astype(F32)
    tok_row = lax.broadcasted_iota(jnp.int32, (1, c), 1).astype(F32)
    q_dec_f = jnp.exp(lgf * (tok_col + 1.0))
    q_dec_b = jnp.exp(lgb * (c - tok_col))
    k_dec_f = jnp.exp(lgf * (c - 1.0 - tok_row))
    k_dec_b = jnp.exp(lgb * tok_row)
    chunk_f = jnp.exp(lgf * c)
    chunk_b = jnp.exp(lgb * c)

    sbc_ref[...] = jnp.zeros_like(sbc_ref)

    def back(i, carry):
        ci = n_chunks - 1 - i
        t0 = pl.multiple_of(ci * c, c)
        sb_ref[ci] = sbc_ref[...].astype(BF)
        kd = (kt_ref[0, :, pl.ds(t0, c)].astype(F32) * k_dec_b).astype(BF)
        sbc_ref[...] = sbc_ref[...] * chunk_b + _dot(kd, v_ref[0, pl.ds(t0, c), :])
        return carry

    lax.fori_loop(0, n_chunks, back, 0, unroll=4)

    sf_ref[...] = jnp.zeros_like(sf_ref)
    gain = gain_ref[...]

    def fwd(ci, carry):
        t0 = pl.multiple_of(ci * c, c)
        qc = q_ref[0, pl.ds(t0, c), :]
        kc = kt_ref[0, :, pl.ds(t0, c)]
        vc = v_ref[0, pl.ds(t0, c), :]
        s = (_dot(qc, kc) * decay).astype(BF)
        qcf = qc.astype(F32)
        qd = jnp.concatenate([(qcf * q_dec_f).astype(BF), (qcf * q_dec_b).astype(BF)], axis=1)
        st = jnp.concatenate([sf_ref[...].astype(BF), sb_ref[ci]], axis=0)
        o = _dot(s, vc) + _dot(qd, st)
        kd = (kc.astype(F32) * k_dec_f).astype(BF)
        sf_ref[...] = sf_ref[...] * chunk_f + _dot(kd, vc)
        o = o * lax.rsqrt(jnp.mean(o * o, axis=-1, keepdims=True) + EPS) * gain
        o_ref[0, pl.ds(t0, c), :] = (g_ref[0, pl.ds(t0, c), :].astype(F32) * o).astype(BF)
        return carry

    lax.fori_loop(0, n_chunks, fwd, 0, unroll=4)


def _retention(qvg, kt, log_gamma, norm_gain, batch):
    n, wq = qvg.shape
    t = n // batch
    assert t % RET_CHUNK == 0
    n_chunks = t // RET_CHUNK
    qvg3 = qvg.reshape(batch, t, wq)
    v_off = (RET_HEADS * RET_DK) // RET_DV
    g_off = v_off + RET_HEADS
    return pl.pallas_call(
        functools.partial(_ret_kernel, n_chunks=n_chunks),
        out_shape=jax.ShapeDtypeStruct((batch, t, RET_HEADS * RET_DV), BF),
        grid_spec=pltpu.PrefetchScalarGridSpec(
            num_scalar_prefetch=1,
            grid=(batch, RET_HEADS),
            in_specs=[pl.BlockSpec((1, t, RET_DK), lambda b, h, lg: (b, 0, h)),
                      pl.BlockSpec((1, RET_DK, t), lambda b, h, lg: (b, h, 0)),
                      pl.BlockSpec((1, t, RET_DV), lambda b, h, lg: (b, 0, v_off + h)),
                      pl.BlockSpec((1, t, RET_DV), lambda b, h, lg: (b, 0, g_off + h)),
                      pl.BlockSpec((1, RET_DV), lambda b, h, lg: (0, h))],
            out_specs=pl.BlockSpec((1, t, RET_DV), lambda b, h, lg: (b, 0, h)),
            scratch_shapes=[pltpu.VMEM((n_chunks, RET_DK, RET_DV), BF),
                            pltpu.VMEM((RET_DK, RET_DV), F32), pltpu.VMEM((RET_DK, RET_DV), F32)]),
        compiler_params=_params(2, VMEM_LIMIT),
        name="retention",
    )(log_gamma, qvg3, kt, qvg3, qvg3, norm_gain[None].astype(F32))


def _out_odd_kernel(y_ref, x_ref, w_ref, *route_refs):
    o_ref = route_refs[N_ROUTE_IN]
    x_new = x_ref[...] + _dot(y_ref[...], w_ref[...])
    o_ref[...] = x_new
    _route(x_new, *route_refs[:N_ROUTE_IN], *route_refs[N_ROUTE_IN + 1:])


def _out_odd(y, x2, w, router, tm=512):
    n, dm = x2.shape
    kdim = y.shape[1]
    r_ops, r_in, r_shapes, r_out, r_scratch = _router_io(n, dm, tm, *router)
    return pl.pallas_call(
        _out_odd_kernel,
        out_shape=[jax.ShapeDtypeStruct((n, dm), F32)] + r_shapes,
        grid=(n // tm,),
        in_specs=[pl.BlockSpec((tm, kdim), lambda i: (i, 0)), pl.BlockSpec((tm, dm), lambda i: (i, 0)),
                  pl.BlockSpec(w.shape, lambda i: (0, 0))] + r_in,
        out_specs=[pl.BlockSpec((tm, dm), lambda i: (i, 0))] + r_out,
        scratch_shapes=r_scratch,
        compiler_params=_params(1, VMEM_LIMIT),
        name="out_odd",
    )(y, x2, w, *r_ops)


def _route(x, g_ref, wh_ref, wl_ref, b_ref, tri_ref, o_ref, ot_ref, cnt_ref, run_ref):
    @pl.when(pl.program_id(0) == 0)
    def _():
        run_ref[...] = jnp.zeros_like(run_ref)

    h = _rms(x, g_ref[...])
    hh, hl = _split_bf16(h)
    logits = _dot(hh, wh_ref[...]) + _dot(hh, wl_ref[...]) + _dot(hl, wh_ref[...]) + b_ref[...]
    sub = tri_ref.shape[0]
    tri = tri_ref[...]
    run = run_ref[...]
    for r0 in range(0, x.shape[0], sub):
        info, run = _route_rows(logits[r0:r0 + sub], tri, run)
        o_ref[r0:r0 + sub, :] = info
        ot_ref[:, r0:r0 + sub] = info.T[0:ROW_ALIGN, :]
    run_ref[...] = run
    cnt_ref[...] = run


def _route_rows(logits, tri, run):
    lane = lax.broadcasted_iota(jnp.int32, logits.shape, 1).astype(F32)
    neg = jnp.float32(-jnp.inf)

    def top(mask):
        val = jnp.max(jnp.where(mask, logits, neg), axis=-1, keepdims=True)
        idx = jnp.min(jnp.where(mask & (logits == val), lane, 1e6), axis=-1, keepdims=True)
        return val, idx

    gmask = lane < MOE_GROUPS
    gmax, gidx = top(gmask)
    gprob = 1.0 / jnp.sum(jnp.where(gmask, jnp.exp(logits - gmax), 0.0), axis=-1, keepdims=True)
    base = MOE_GROUPS + MOE_PER_GROUP * gidx
    emask = (lane >= base) & (lane < base + MOE_PER_GROUP)
    v1, i1 = top(emask)
    v2, i2 = top(emask & (lane != i1))
    e2 = jnp.exp(v2 - v1)
    w1 = gprob / (1.0 + e2)
    w2 = gprob * e2 / (1.0 + e2)
    a1 = i1 - base
    a2 = i2 - base
    lo = jnp.minimum(a1, a2)
    hi = jnp.maximum(a1, a2)
    pair = lo * (7.0 - lo) * 0.5 + hi - lo - 1.0
    cls = gidx * MOE_PAIRS + pair
    w_lo = jnp.where(a1 < a2, w1, w2)
    w_hi = jnp.where(a1 < a2, w2, w1)
    onehot = lane == cls
    earlier = _dot(tri, jnp.where(onehot, 1.0, 0.0).astype(BF)) + run
    rank = jnp.sum(jnp.where(onehot, earlier, 0.0), axis=-1, keepdims=True)
    info = jnp.where(lane == 0, cls, jnp.where(lane == 1, w_lo, jnp.where(lane == 2, w_hi,
                     jnp.where(lane == 3, rank, 0.0))))
    return info, run + jnp.sum(jnp.where(onehot, 1.0, 0.0), axis=0, keepdims=True)


ROUTE_ROWS = 128
N_ROUTE_IN = 5


def _router_io(n, dm, tm, gain, w_hi, w_lo, b_r):
    tri = (jnp.arange(ROUTE_ROWS)[:, None] > jnp.arange(ROUTE_ROWS)[None, :]).astype(BF)
    const = lambda shp: pl.BlockSpec(shp, lambda i: (0, 0))
    operands = [gain[None].astype(F32), w_hi, w_lo, b_r, tri]
    in_specs = [const((1, dm)), const((dm, 128)), const((dm, 128)), const((1, 128)), const(tri.shape)]
    out_shapes = [jax.ShapeDtypeStruct((n, 128), F32), jax.ShapeDtypeStruct((ROW_ALIGN, n), F32),
                  jax.ShapeDtypeStruct((1, 128), F32)]
    out_specs = [pl.BlockSpec((tm, 128), lambda i: (i, 0)), pl.BlockSpec((ROW_ALIGN, tm), lambda i: (0, i)),
                 const((1, 128))]
    return operands, in_specs, out_shapes, out_specs, [pltpu.VMEM((1, 128), F32)]


def _start_row_dmas(n, copy_of_row):
    for r in range(n):
        copy_of_row(r).start()


def _dispatch_kernel(fpos_ref, fon_ref, pos_ref, x_ref, info_ref, o_hbm, aug_ref, zero_ref, sem, row_sem):
    tm, dm = x_ref.shape

    @pl.when(pl.program_id(0) == 0)
    def _():
        zero_ref[...] = jnp.zeros_like(zero_ref)
        n_fill = fpos_ref.shape[0]

        def fill(k):
            return pltpu.make_async_copy(
                zero_ref, o_hbm.at[pl.ds(pl.multiple_of(fpos_ref[k], ROW_ALIGN), MOE_FILL), :], sem)

        for k in range(n_fill - 1):
            pl.when(fon_ref[k] > 0)(lambda k=k: fill(k).start())
        for k in range(n_fill - 1):
            pl.when(fon_ref[k] > 0)(lambda k=k: fill(k).wait())
        fill(n_fill - 1).start()
        fill(n_fill - 1).wait()

    i = pl.program_id(0)
    n_steps = pl.num_programs(0)
    slot = i % 2

    def wait_rows(s):
        pltpu.make_async_copy(aug_ref.at[s], o_hbm.at[pl.ds(0, tm), :], row_sem.at[s]).wait()

    pl.when(i >= 2)(lambda: wait_rows(slot))
    aug_ref[slot, :, 0:dm] = x_ref[...]
    aug_ref[slot, :, dm:] = info_ref[...]
    _start_row_dmas(tm, lambda r: pltpu.make_async_copy(
        aug_ref.at[slot, pl.ds(r, 1), :], o_hbm.at[pl.ds(pos_ref[0, 0, r], 1), :], row_sem.at[slot]))

    @pl.when(i == n_steps - 1)
    def _():
        pl.when(n_steps >= 2)(lambda: wait_rows(1 - slot))
        wait_rows(slot)


def _dispatch(x2, info, pos, first_pad, n_rows, tm=512):
    n, dm = x2.shape
    wa = dm + info.shape[1]
    class_fill = first_pad // ROW_ALIGN * ROW_ALIGN
    tail = class_fill[-1] + MOE_FILL + jnp.arange((n_rows - n) // MOE_FILL + 1, dtype=jnp.int32) * MOE_FILL
    tail_on = tail + MOE_FILL <= n_rows
    fpos = jnp.concatenate([class_fill, jnp.where(tail_on, tail, 0), jnp.array([n_rows - MOE_FILL], jnp.int32)])
    fon = jnp.concatenate([jnp.ones_like(class_fill), tail_on.astype(jnp.int32), jnp.ones((1,), jnp.int32)])
    return pl.pallas_call(
        _dispatch_kernel,
        out_shape=jax.ShapeDtypeStruct((n_rows, wa), F32),
        grid_spec=pltpu.PrefetchScalarGridSpec(
            num_scalar_prefetch=2,
            grid=(n // tm,),
            in_specs=[pl.BlockSpec((1, 1, tm), lambda i, fp, fo: (i, 0, 0), memory_space=pltpu.SMEM),
                      pl.BlockSpec((tm, dm), lambda i, fp, fo: (i, 0)),
                      pl.BlockSpec((tm, info.shape[1]), lambda i, fp, fo: (i, 0))],
            out_specs=pl.BlockSpec(memory_space=pl.ANY),
            scratch_shapes=[pltpu.VMEM((2, tm, wa), F32), pltpu.VMEM((MOE_FILL, wa), F32),
                            pltpu.SemaphoreType.DMA, pltpu.SemaphoreType.DMA((2,))]),
        compiler_params=_params(1, VMEM_LIMIT),
        name="moe_dispatch",
    )(fpos.astype(jnp.int32), fon, pos.reshape(n // tm, 1, tm), x2, info)


def _expert_kernel(ea_ref, eb_ref, act_ref, xa_ref, g_ref,
                   wga_ref, wua_ref, wda_ref, wgb_ref, wub_ref, wdb_ref, o_ref):
    i = pl.program_id(0)
    dm = o_ref.shape[1]

    @pl.when(act_ref[i] > 0)
    def _():
        h = _rms(xa_ref[:, 0:dm], g_ref[...]).astype(BF)

        def ffn(wg, wu, wd, wrow):
            gate = _dot(h, wg[0])
            hid = gate * _sigmoid(gate) * _dot(h, wu[0])
            return _dot((hid * wrow).astype(BF), wd[0])

        o_ref[...] = (ffn(wga_ref, wua_ref, wda_ref, xa_ref[:, dm + 1:dm + 2])
                      + ffn(wgb_ref, wub_ref, wdb_ref, xa_ref[:, dm + 2:dm + 3]))

    @pl.when(act_ref[i] == 0)
    def _():
        o_ref[...] = jnp.zeros_like(o_ref)


def _experts(xa_sorted, gain, tile_ea, tile_eb, tile_act, w_gate, w_up, w_down):
    dm = w_gate.shape[1]
    wa = xa_sorted.shape[1]
    n_tiles = tile_ea.shape[0]
    tm = MOE_TILE
    ff = w_gate.shape[-1]
    wsel = lambda shp, which: pl.BlockSpec((1,) + shp, lambda i, ea, eb, act: ((ea, eb)[which][i], 0, 0))
    return pl.pallas_call(
        _expert_kernel,
        out_shape=jax.ShapeDtypeStruct((n_tiles * tm, dm), F32),
        grid_spec=pltpu.PrefetchScalarGridSpec(
            num_scalar_prefetch=3,
            grid=(n_tiles,),
            in_specs=[pl.BlockSpec((tm, wa), lambda i, ea, eb, act: (i * act[i], 0)),
                      pl.BlockSpec((1, dm), lambda i, ea, eb, act: (0, 0)),
                      wsel((dm, ff), 0), wsel((dm, ff), 0), wsel((ff, dm), 0),
                      wsel((dm, ff), 1), wsel((dm, ff), 1), wsel((ff, dm), 1)],
            out_specs=pl.BlockSpec((tm, dm), lambda i, ea, eb, act: (i, 0))),
        compiler_params=_params(1, VMEM_LIMIT),
        name="moe_experts",
    )(tile_ea, tile_eb, tile_act, xa_sorted, gain, w_gate, w_up, w_down, w_gate, w_up, w_down)


def _combine_kernel(pos_ref, nxt_ref, x_ref, y_hbm, o_ref, ybuf, sem):
    i = pl.program_id(0)
    n_steps = pl.num_programs(0)
    tm = ybuf.shape[1]
    slot = i % 2

    def gather(idx_ref, s):
        _start_row_dmas(tm, lambda r: pltpu.make_async_copy(
            y_hbm.at[pl.ds(idx_ref[0, 0, r], 1), :], ybuf.at[s, pl.ds(r, 1), :], sem.at[s]))

    @pl.when(i == 0)
    def _():
        gather(pos_ref, 0)

    @pl.when(i + 1 < n_steps)
    def _():
        gather(nxt_ref, 1 - slot)

    pltpu.make_async_copy(y_hbm.at[pl.ds(0, tm), :], ybuf.at[slot], sem.at[slot]).wait()
    o_ref[...] = x_ref[...] + ybuf[slot]


def _combine(x2, y_sorted, pos, tm=512):
    n, dm = x2.shape
    n_steps = n // tm
    pos3 = pos.reshape(n_steps, 1, tm)
    return pl.pallas_call(
        _combine_kernel,
        out_shape=jax.ShapeDtypeStruct((n, dm), F32),
        grid=(n_steps,),
        in_specs=[pl.BlockSpec((1, 1, tm), lambda i: (i, 0, 0), memory_space=pltpu.SMEM),
                  pl.BlockSpec((1, 1, tm), lambda i: (jnp.minimum(i + 1, n_steps - 1), 0, 0), memory_space=pltpu.SMEM),
                  pl.BlockSpec((tm, dm), lambda i: (i, 0)),
                  pl.BlockSpec(memory_space=pl.ANY)],
        out_specs=pl.BlockSpec((tm, dm), lambda i: (i, 0)),
        scratch_shapes=[pltpu.VMEM((2, tm, dm), F32), pltpu.SemaphoreType.DMA((2,))],
        compiler_params=_params(1, VMEM_LIMIT),
        name="moe_combine",
    )(pos3, pos3, x2, y_sorted)


def _moe(x2, info, info_rows, cnt, gain, layer, w_gate, w_up, w_down):
    n, dm = x2.shape
    tm = MOE_TILE
    gain = gain[None].astype(F32)
    cls = info_rows[0].astype(jnp.int32)
    counts = cnt[0, :MOE_CLASSES].astype(jnp.int32)
    padded = ((counts + tm - 1) // tm) * tm
    ends = jnp.cumsum(padded)
    starts = ends - padded
    pos = starts[cls] + info_rows[3].astype(jnp.int32)
    n_tiles = n // tm + MOE_CLASSES
    tile_start = jnp.arange(n_tiles, dtype=jnp.int32) * tm
    tile_cls = jnp.minimum(jnp.sum(tile_start[:, None] >= ends[None, :], axis=1), MOE_CLASSES - 1)
    tile_act = (tile_start < ends[-1]).astype(jnp.int32)
    pair_lo = jnp.array([0, 0, 0, 1, 1, 2], jnp.int32)
    pair_hi = jnp.array([1, 2, 3, 2, 3, 3], jnp.int32)
    first = layer * MOE_GROUPS * MOE_PER_GROUP + (tile_cls // MOE_PAIRS) * MOE_PER_GROUP
    tile_ea = (first + pair_lo[tile_cls % MOE_PAIRS]).astype(jnp.int32)
    tile_eb = (first + pair_hi[tile_cls % MOE_PAIRS]).astype(jnp.int32)
    xa_sorted = _dispatch(x2, info, pos, (starts + counts).astype(jnp.int32), (n_tiles + 2) * tm)
    y_sorted = _experts(xa_sorted, gain, tile_ea, tile_eb, tile_act, w_gate, w_up, w_down)
    return _combine(x2, y_sorted, pos)


def _router_weights(w_group, b_group, w_router, b_router):
    dm = w_group.shape[0]
    w = jnp.concatenate([w_group.astype(F32), w_router.astype(F32).reshape(dm, -1)], axis=1)
    b = jnp.concatenate([b_group.astype(F32), b_router.astype(F32).reshape(-1)])
    pad = 128 - w.shape[1]
    w_hi, w_lo = _split_bf16(jnp.pad(w, ((0, 0), (0, pad))))
    return w_hi, w_lo, jnp.pad(b, (0, pad))[None]


def _trunk(x, p):
    b, t, dm = x.shape
    n = b * t
    x2 = x.reshape(n, dm)

    ug, q, k, v = _in_even(x2, p["norm_mix"][0], p["w_in_even"], p["na_q_gain"], p["na_k_gain"], b)
    n_chunks = t // S5_CHUNK
    gw = S5_CHUNK * S5_GROUP
    ug = jnp.swapaxes(ug, 1, 2).reshape(S5_GROUPS, n_chunks * b, gw)
    yg = _s5_scan(ug, *p["s5_ops"], p["s5_d_rows"], n_chunks, b)
    yg = jnp.swapaxes(yg.reshape(S5_GROUPS, n_chunks, b, gw), 1, 2)
    d_att = q.shape[1]
    att = _na(q.reshape(b, t, d_att), k.reshape(b, t, d_att), v.reshape(b, t, d_att), p["na_table"])
    x2, *routing = _out_even(yg, att.reshape(n, d_att), x2, p["s5_w_glu"], p["s5_b_glu"], p["w_out_even"],
                             (p["norm_ffn"][0], *p["router"][0]))
    x2 = _moe(x2, *routing, p["norm_ffn"][0], 0, *p["experts"])

    qvg, kt = _in_odd(x2, p["norm_mix"][1], p["w_in_odd"], p["w_k_t"], b)
    yr = _retention(qvg, kt, p["log_gamma"], p["ret_norm_gain"], b)
    x2, *routing = _out_odd(yr.reshape(n, -1), x2, p["w_out_odd"], (p["norm_ffn"][1], *p["router"][1]))
    x2 = _moe(x2, *routing, p["norm_ffn"][1], 1, *p["experts"])
    return x2.reshape(b, t, dm)


def kernel(x_prompt, x_sample, norm_mix, norm_ffn, w_in_even, w_out_even, s5_lambda_re, s5_lambda_im, s5_log_dt, s5_b_re, s5_b_im, s5_c_re, s5_c_im, s5_d, s5_w_glu, s5_b_glu, na_q_gain, na_k_gain, na_rpb, w_in_odd, w_out_odd, ret_decay_logit, ret_norm_gain, moe_w_group, moe_b_group, moe_w_router, moe_b_router, moe_w_gate, moe_w_up, moe_w_down):
    assert norm_mix.shape[0] == 2
    dk_all = RET_HEADS * RET_DK
    w_odd = w_in_odd[0].astype(BF)
    p = {
        "norm_mix": norm_mix, "norm_ffn": norm_ffn,
        "w_in_even": w_in_even[0].astype(BF), "w_out_even": w_out_even[0].astype(BF),
        "s5_ops": _s5_operators(s5_lambda_re[0], s5_lambda_im[0], s5_log_dt[0], s5_b_re[0], s5_b_im[0],
                                s5_c_re[0], s5_c_im[0]),
        "s5_d_rows": jnp.tile(s5_d[0].astype(F32).reshape(S5_GROUPS, 1, S5_GROUP), (1, 1, S5_CHUNK)),
        "s5_w_glu": s5_w_glu[0].astype(BF), "s5_b_glu": s5_b_glu[0],
        "na_q_gain": na_q_gain[0], "na_k_gain": na_k_gain[0], "na_table": _na_bias_table(na_rpb[0]),
        "w_in_odd": w_odd, "w_k_t": w_in_odd[0][:, dk_all:2 * dk_all].T.astype(BF),
        "w_out_odd": w_out_odd[0].astype(BF),
        "log_gamma": jax.nn.log_sigmoid(ret_decay_logit[0].astype(F32)),
        "ret_norm_gain": ret_norm_gain[0],
        "router": [_router_weights(moe_w_group[l], moe_b_group[l], moe_w_router[l], moe_b_router[l]) for l in range(2)],
        "experts": tuple(w.astype(BF).reshape((-1,) + w.shape[2:]) for w in (moe_w_gate, moe_w_up, moe_w_down)),
    }
    return _trunk(x_prompt, p), _trunk(x_sample, p)
```

```python
import functools
import math

import jax
import jax.numpy as jnp
from jax import lax
from jax.experimental import pallas as pl
from jax.experimental.pallas import tpu as pltpu

F32 = jnp.float32
BF = jnp.bfloat16
EPS = 1e-6

D_MODEL = 1024
GRID_W = 64
S5_GROUPS = 32
S5_GROUP = 16
S5_STATE = 64
S5_CHUNK = 16
NA_HEADS = 8
NA_HEAD_DIM = 64
NA_WIN_R = 8
NA_WIN_C = 16
NA_HEAD_GROUP = 4
RET_HEADS = 8
RET_DK = 128
RET_DV = 256
RET_CHUNK = 256
ROPE_BASE = 10000.0
MOE_GROUPS = 4
MOE_PER_GROUP = 4
MOE_FF = 512
MOE_PAIRS = 6
MOE_CLASSES = MOE_GROUPS * MOE_PAIRS
MOE_TILE = 256
ROW_ALIGN = 8
MOE_FILL = MOE_TILE + ROW_ALIGN
MASK_NEG = -1e30
VMEM_LIMIT = 56 * 1024 * 1024


def _params(n_axes, vmem=None):
    return pltpu.CompilerParams(dimension_semantics=("arbitrary",) * n_axes,
                                vmem_limit_bytes=vmem)


def _rms(x, gain):
    ms = jnp.mean(x * x, axis=-1, keepdims=True)
    return x * lax.rsqrt(ms + EPS) * gain


def _sigmoid(x):
    return 1.0 / (1.0 + jnp.exp(-x))


def _dot(a, b):
    return jnp.dot(a, b, preferred_element_type=F32)


def _dot_nt(a, b):
    return lax.dot_general(a, b, (((1,), (1,)), ((), ())), preferred_element_type=F32)


def _split_bf16(x):
    hi = x.astype(BF)
    lo = (x - hi.astype(F32)).astype(BF)
    return hi, lo


def _dot3(a, b):
    ah, al = _split_bf16(a)
    bh, bl = _split_bf16(b)
    return _dot(ah, bh) + _dot(ah, bl) + _dot(al, bh)


def _chunk_major_perm(batch, tq):
    nc = tq // S5_CHUNK
    r = jnp.arange(batch * tq)
    j, c, b = r // (nc * batch), (r // batch) % nc, r % batch
    src = b * tq + c * S5_CHUNK + j
    return (src[:, None] == jnp.arange(batch * tq)[None, :]).astype(BF)


def _in_even_kernel(x_ref, g_ref, w_ref, qg_ref, kg_ref, seg_ref, perm_ref, ug_ref, q_ref, k_ref, v_ref):
    batch, tq, dm = x_ref.shape
    h = _rms(x_ref[...].reshape(batch * tq, dm), g_ref[...]).astype(BF)
    d = q_ref.shape[-1]
    nr = ug_ref.shape[1]
    up = _dot(perm_ref[...], _dot(h, w_ref[:, 0:d]).astype(BF))
    for g in range(S5_GROUPS):
        lanes = slice(g * S5_GROUP, (g + 1) * S5_GROUP)
        ug_ref[g] = jnp.concatenate([up[j * nr:(j + 1) * nr, lanes] for j in range(S5_CHUNK)], axis=1).astype(BF)
    seg = seg_ref[...]

    def head_norm(z, gain):
        ssq = _dot((z * z).astype(BF), seg) * (1.0 / NA_HEAD_DIM)
        return z * lax.rsqrt(ssq + EPS) * gain

    q = _dot(h, w_ref[:, d:2 * d])
    q_ref[...] = (head_norm(q, qg_ref[...]) * (NA_HEAD_DIM ** -0.5)).astype(BF).reshape(batch, tq, d)
    k = _dot(h, w_ref[:, 2 * d:3 * d])
    k_ref[...] = head_norm(k, kg_ref[...]).astype(BF).reshape(batch, tq, d)
    v_ref[...] = _dot(h, w_ref[:, 3 * d:4 * d]).astype(BF).reshape(batch, tq, d)


def _even_tile(batch, tm=512):
    tq = tm // batch
    assert tq * batch == tm and tq % S5_CHUNK == 0
    return tq


def _in_even(x3, gain, w, q_gain, k_gain):
    batch, t, dm = x3.shape
    d = w.shape[1] // 4
    tq = _even_tile(batch)
    nr = (tq // S5_CHUNK) * batch
    seg = (jnp.arange(d)[:, None] // NA_HEAD_DIM == jnp.arange(d)[None, :] // NA_HEAD_DIM).astype(BF)
    qg = jnp.tile(q_gain.astype(F32), NA_HEADS)[None]
    kg = jnp.tile(k_gain.astype(F32), NA_HEADS)[None]
    tok = lambda width: pl.BlockSpec((batch, tq, width), lambda i: (0, i, 0))
    full = lambda shp: pl.BlockSpec(shp, lambda i: (0,) * len(shp))
    gw = S5_CHUNK * S5_GROUP
    perm = _chunk_major_perm(batch, tq)
    return pl.pallas_call(
        _in_even_kernel,
        out_shape=[jax.ShapeDtypeStruct((S5_GROUPS, (t // S5_CHUNK) * batch, gw), BF)]
        + [jax.ShapeDtypeStruct((batch, t, d), BF)] * 3,
        grid=(t // tq,),
        in_specs=[tok(dm), full((1, dm)), full(w.shape), full((1, d)), full((1, d)), full((d, d)), full(perm.shape)],
        out_specs=[pl.BlockSpec((S5_GROUPS, nr, gw), lambda i: (0, i, 0)), tok(d), tok(d), tok(d)],
        compiler_params=_params(1, VMEM_LIMIT),
        name="in_even",
    )(x3, gain[None].astype(F32), w, qg, kg, seg, perm)


def _s5_taps_kernel(l_ref, r_ref, o_ref):
    for g in range(l_ref.shape[0]):
        o_ref[g] = _dot3(l_ref[g], r_ref[g])


def _s5_taps(lhs, rhs, per_step=8):
    g2, m, k = lhs.shape
    nn = rhs.shape[-1]
    return pl.pallas_call(
        _s5_taps_kernel,
        out_shape=jax.ShapeDtypeStruct((g2, m, nn), F32),
        grid=(g2 // per_step,),
        in_specs=[pl.BlockSpec((per_step, m, k), lambda i: (i, 0, 0)), pl.BlockSpec((per_step, k, nn), lambda i: (i, 0, 0))],
        out_specs=pl.BlockSpec((per_step, m, nn), lambda i: (i, 0, 0)),
        compiler_params=_params(1),
        name="s5_taps",
    )(lhs, rhs)


def _s5_operators(lam_re, lam_im, log_dt, b_re, b_im, c_re, c_im):
    L, G, H, P_ = S5_CHUNK, S5_GROUPS, S5_GROUP, S5_STATE
    lr, li = lam_re.astype(F32), lam_im.astype(F32)
    dt = jnp.exp(log_dt.astype(F32))[..., None]
    kk = jnp.arange(L + 1, dtype=F32)[:, None, None, None]
    mag = jnp.exp(kk * (lr * dt)[None])
    ang = kk * (li * dt)[None]
    pw = (mag * jnp.cos(ang), mag * jnp.sin(ang))

    def cmul(x, y):
        return x[0] * y[0] - x[1] * y[1], x[0] * y[1] + x[1] * y[0]

    xr, xi = pw[0][1] - 1.0, pw[1][1]
    den = lr * lr + li * li
    coef = ((xr * lr + xi * li) / den, (xi * lr - xr * li) / den)
    bbar = cmul((coef[0][..., None], coef[1][..., None]), (b_re.astype(F32), b_im.astype(F32)))
    c = (c_re.astype(F32), c_im.astype(F32))

    def at(k_slice, direction, perm):
        return tuple(jnp.transpose(z[k_slice, direction], perm) for z in pw)

    apl = tuple(jnp.transpose(z[:L], (1, 2, 0, 3))[:, :, :, None, :] for z in pw)
    ce = cmul((c[0][:, :, None], c[1][:, :, None]), apl)
    lhs = jnp.concatenate([ce[0], -ce[1]], axis=-1).reshape(2 * G, L * H, 2 * P_)
    rhs = jnp.concatenate([bbar[0], bbar[1]], axis=2).reshape(2 * G, 2 * P_, H)
    taps = _s5_taps(lhs, rhs).reshape(2, G, L, H, H)
    taps = jnp.transpose(taps, (0, 1, 2, 4, 3))
    kf, kb = taps[0], taps[1]
    lagged = jnp.concatenate([jnp.flip(kb[:, 1:], axis=1), (kf[:, 0] + kb[:, 0])[:, None], kf[:, 1:]], axis=1)
    idx = jnp.arange(L)[None, :] - jnp.arange(L)[:, None] + (L - 1)
    m = jnp.transpose(lagged[:, idx], (0, 1, 3, 2, 4)).reshape(G, L * H, L * H)

    bt = tuple(jnp.transpose(z, (0, 1, 3, 2)) for z in bbar)
    expand_j = lambda z: z[:, :, None, :]
    pf = cmul(tuple(map(expand_j, at(slice(L - 1, None, -1), 0, (1, 0, 2)))), (bt[0][0][:, None], bt[1][0][:, None]))
    pb = cmul(tuple(map(expand_j, at(slice(0, L), 1, (1, 0, 2)))), (bt[0][1][:, None], bt[1][1][:, None]))
    p = jnp.concatenate([pf[0], pb[0], pf[1], pb[1]], axis=-1).reshape(G, L * H, 4 * P_)

    ct = tuple(jnp.transpose(z, (0, 1, 3, 2)) for z in c)
    expand_h = lambda z: z[..., None]
    gf = cmul(tuple(map(expand_h, at(slice(1, None), 0, (1, 2, 0)))), (ct[0][0][:, :, None, :], ct[1][0][:, :, None, :]))
    gb = cmul(tuple(map(expand_h, at(slice(L, 0, -1), 1, (1, 2, 0)))), (ct[0][1][:, :, None, :], ct[1][1][:, :, None, :]))
    q = jnp.concatenate([gf[0], gb[0], -gf[1], -gb[1]], axis=1).reshape(G, 4 * P_, L * H)

    a_mat = jnp.stack([jnp.concatenate([pw[0][L, 0], pw[0][L, 1]], -1),
                       jnp.concatenate([pw[1][L, 0], pw[1][L, 1]], -1)], axis=1)
    return m.astype(BF), p.astype(BF), q.astype(BF), a_mat.astype(F32)


def _s5_kernel(u_ref, m_ref, p_ref, q_ref, a_ref, d_ref, y_ref, z_ref, s_ref, *, n_chunks, batch):
    u = u_ref[0]
    z_ref[...] = _dot(u, p_ref[0])
    half = 2 * S5_STATE
    ar = a_ref[0, 0:1, :]
    ai = a_ref[0, 1:2, :]
    fwd_lane = lax.broadcasted_iota(jnp.int32, (batch, half), 1) < S5_STATE

    def step(k, carry):
        sr, si = carry
        f0 = pl.multiple_of(k * batch, batch)
        b0 = pl.multiple_of((n_chunks - 1 - k) * batch, batch)
        s_ref[pl.ds(f0, batch), 0:S5_STATE] = sr[:, 0:S5_STATE]
        s_ref[pl.ds(b0, batch), S5_STATE:half] = sr[:, S5_STATE:half]
        s_ref[pl.ds(f0, batch), half:half + S5_STATE] = si[:, 0:S5_STATE]
        s_ref[pl.ds(b0, batch), half + S5_STATE:2 * half] = si[:, S5_STATE:half]
        zr = jnp.where(fwd_lane, z_ref[pl.ds(f0, batch), 0:half], z_ref[pl.ds(b0, batch), 0:half])
        zi = jnp.where(fwd_lane, z_ref[pl.ds(f0, batch), half:2 * half], z_ref[pl.ds(b0, batch), half:2 * half])
        return ar * sr - ai * si + zr, ar * si + ai * sr + zi

    zero = jnp.zeros((batch, half), F32)
    lax.fori_loop(0, n_chunks, step, (zero, zero), unroll=8)
    y = _dot(u, m_ref[0]) + _dot(s_ref[...].astype(BF), q_ref[0]) + u.astype(F32) * d_ref[0]
    y_ref[0] = y.astype(BF)


def _s5_scan(ug, m, p, q, a_mat, d_skip, n_chunks, batch):
    g, r, w = ug.shape
    blk = lambda shp: pl.BlockSpec((1,) + shp, lambda i: (i, 0, 0))
    return pl.pallas_call(
        functools.partial(_s5_kernel, n_chunks=n_chunks, batch=batch),
        out_shape=jax.ShapeDtypeStruct((g, r, w), BF),
        grid=(g,),
        in_specs=[blk((r, w)), blk((w, w)), blk((w, w)), blk((w, w)), blk((2, w // 2)), blk((1, w))],
        out_specs=blk((r, w)),
        scratch_shapes=[pltpu.VMEM((r, w), F32), pltpu.VMEM((r, w), F32)],
        compiler_params=_params(1, VMEM_LIMIT),
        name="s5_scan",
    )(ug, m, p, q, a_mat, d_skip)


def _na_bias_table(rpb):
    qc = jnp.arange(GRID_W)[:, None]
    kc = jnp.arange(GRID_W)[None, :]
    ws = jnp.clip(qc - NA_WIN_C // 2, 0, GRID_W - NA_WIN_C)
    valid = (kc >= ws) & (kc < ws + NA_WIN_C)
    n_tap = 2 * NA_WIN_C - 1
    pad = GRID_W - NA_WIN_C
    row = n_tap + 2 * pad
    taps = jnp.pad(rpb.astype(F32), ((0, 0), (0, 0), (pad, pad)))
    skew = jnp.tile(taps, (1, 1, GRID_W))[:, :, GRID_W - 1:GRID_W - 1 + GRID_W * (row - 1)]
    bias = skew.reshape(rpb.shape[0], rpb.shape[1], GRID_W, row - 1)[..., :GRID_W]
    e = jnp.where(valid[None, None], bias, MASK_NEG)
    return jnp.concatenate([e[:, :-1], e[:, 1:]], axis=-1)


def _na_kernel(q_ref, k_ref, v_ref, e_ref, o_ref, *, rows):
    hg_w = NA_HEAD_GROUP * NA_HEAD_DIM
    lane_head = lax.broadcasted_iota(jnp.int32, (GRID_W, hg_w), 1) // NA_HEAD_DIM
    n_keys = NA_WIN_R * GRID_W

    def row_body(r, carry):
        rs = jnp.clip(r - NA_WIN_R // 2, 0, rows - NA_WIN_R)
        di = rs - r + (NA_WIN_R - 1)
        q0 = pl.multiple_of(r * GRID_W, GRID_W)
        k0 = pl.multiple_of(rs * GRID_W, GRID_W)
        for hg in range(NA_HEADS // NA_HEAD_GROUP):
            lanes = slice(hg * hg_w, (hg + 1) * hg_w)
            q4 = q_ref[0, pl.ds(q0, GRID_W), lanes]
            qs = jnp.concatenate([jnp.where(lane_head == hh, q4, jnp.zeros_like(q4))
                                  for hh in range(NA_HEAD_GROUP)], axis=0)
            k4 = k_ref[0, pl.ds(k0, n_keys), lanes]
            v4 = v_ref[0, pl.ds(k0, n_keys), lanes]
            s = _dot_nt(qs, k4)
            bias = jnp.concatenate(
                [jnp.concatenate([e_ref[hg * NA_HEAD_GROUP + hh, di + 2 * w2] for w2 in range(NA_WIN_R // 2)], axis=1)
                 for hh in range(NA_HEAD_GROUP)], axis=0)
            s = s + bias
            mx = jnp.max(s, axis=-1, keepdims=True)
            p = jnp.exp(s - mx)
            den = jnp.sum(p, axis=-1, keepdims=True)
            o = _dot(p.astype(BF), v4) * (1.0 / den)
            out = jnp.zeros((GRID_W, hg_w), F32)
            for hh in range(NA_HEAD_GROUP):
                out = jnp.where(lane_head == hh, o[hh * GRID_W:(hh + 1) * GRID_W], out)
            o_ref[0, pl.ds(q0, GRID_W), lanes] = out.astype(BF)
        return carry

    lax.fori_loop(0, rows, row_body, 0, unroll=4)


def _na(q, k, v, table):
    b, t, d = q.shape
    rows = t // GRID_W
    assert rows >= NA_WIN_R and t % GRID_W == 0
    seq = pl.BlockSpec((1, t, d), lambda i: (i, 0, 0))
    return pl.pallas_call(
        functools.partial(_na_kernel, rows=rows),
        out_shape=jax.ShapeDtypeStruct((b, t, d), BF),
        grid=(b,),
        in_specs=[seq, seq, seq, pl.BlockSpec(table.shape, lambda i: (0, 0, 0, 0))],
        out_specs=seq,
        compiler_params=_params(1, VMEM_LIMIT),
        name="na",
    )(q, k, v, table)


def _out_even_kernel(yg_ref, a_ref, x_ref, wg_ref, bg_ref, wo_ref, perm_ref, *route_refs):
    o_ref = route_refs[N_ROUTE_IN]
    batch, tq, dm = x_ref.shape
    rows = [jnp.concatenate([yg_ref[g, :, i * S5_GROUP:(i + 1) * S5_GROUP] for g in range(S5_GROUPS)], axis=1)
            for i in range(S5_CHUNK)]
    ys = _dot(perm_ref[...], jnp.concatenate(rows, axis=0))
    c0 = math.sqrt(2.0 / math.pi)
    ys = 0.5 * ys * (1.0 + jnp.tanh(c0 * (ys + 0.044715 * (ys * ys * ys))))
    gate = _sigmoid(_dot(ys.astype(BF), wg_ref[...]) + bg_ref[...])
    a_out = (ys * gate).astype(BF)
    da = a_out.shape[-1]
    att = a_ref[...].reshape(batch * tq, da)
    x_new = x_ref[...].reshape(batch * tq, dm) + _dot(a_out, wo_ref[0:da, :]) + _dot(att, wo_ref[da:, :])
    o_ref[...] = x_new.reshape(batch, tq, dm)
    _route(x_new, *route_refs[:N_ROUTE_IN], *route_refs[N_ROUTE_IN + 1:])


def _out_even(yg, att, x3, w_glu, b_glu, w_out, router):
    batch, t, dm = x3.shape
    d = att.shape[-1]
    groups, _, gw = yg.shape
    tq = _even_tile(batch)
    tm = batch * tq
    nr = (tq // S5_CHUNK) * batch
    tok = lambda width: pl.BlockSpec((batch, tq, width), lambda i: (0, i, 0))
    full = lambda shp: pl.BlockSpec(shp, lambda i: (0,) * len(shp))
    r_ops, r_in, r_shapes, r_out, r_scratch = _router_io(batch * t, dm, tm, *router)
    perm = _chunk_major_perm(batch, tq).T
    return pl.pallas_call(
        _out_even_kernel,
        out_shape=[jax.ShapeDtypeStruct((batch, t, dm), F32)] + r_shapes,
        grid=(t // tq,),
        in_specs=[pl.BlockSpec((groups, nr, gw), lambda i: (0, i, 0)), tok(d), tok(dm),
                  full((d, d)), full((1, d)), full(w_out.shape), full(perm.shape)] + r_in,
        out_specs=[tok(dm)] + r_out,
        scratch_shapes=r_scratch,
        compiler_params=_params(1, VMEM_LIMIT),
        name="out_even",
    )(yg, att, x3, w_glu, b_glu[None].astype(F32), w_out, perm, *r_ops)


def _in_odd_kernel(x_ref, g_ref, w_ref, wkt_ref, cq_ref, sq_ref, ck_ref, sk_ref, ng_ref, o_ref, kt_ref):
    h = _rms(x_ref[...], g_ref[...]).astype(BF)
    dk_all = RET_HEADS * RET_DK
    half = RET_DK // 2
    q = _dot(h, w_ref[:, 0:dk_all])
    cq, sq = cq_ref[...], sq_ref[...]
    for hd in range(RET_HEADS):
        lanes = slice(hd * RET_DK, (hd + 1) * RET_DK)
        qh = q[:, lanes]
        o_ref[:, lanes] = (qh * cq + pltpu.roll(qh, half, 1) * sq).astype(BF)
    kt = _dot_nt(wkt_ref[...], h)
    ck, sk = ck_ref[...], sk_ref[...]
    scale = RET_DK ** -0.5
    for hd in range(RET_HEADS):
        k1 = kt[hd * RET_DK:hd * RET_DK + half, :]
        k2 = kt[hd * RET_DK + half:(hd + 1) * RET_DK, :]
        kt_ref[0, hd * RET_DK:hd * RET_DK + half, :] = ((k1 * ck - k2 * sk) * scale).astype(BF)
        kt_ref[0, hd * RET_DK + half:(hd + 1) * RET_DK, :] = ((k1 * sk + k2 * ck) * scale).astype(BF)
    g0 = dk_all + RET_HEADS * RET_DV
    for c0 in range(dk_all, o_ref.shape[1], dk_all):
        z = _dot(h, w_ref[:, c0 + dk_all:c0 + 2 * dk_all])
        if c0 >= g0:
            z = z * _sigmoid(z) * ng_ref[:, c0 - g0:c0 - g0 + dk_all]
        o_ref[:, c0:c0 + dk_all] = z.astype(BF)


def _in_odd(x2, gain, w, wkt, norm_gain, batch, tm=512):
    n, dm = x2.shape
    t = n // batch
    nt = t // tm
    dk_all = RET_HEADS * RET_DK
    half = RET_DK // 2
    inv = ROPE_BASE ** (-jnp.arange(half, dtype=F32) / half)
    ang = jnp.arange(t, dtype=F32)[:, None] * inv[None, :]
    cos, sin = jnp.cos(ang), jnp.sin(ang)
    cq = jnp.concatenate([cos, cos], axis=1)
    sq = jnp.concatenate([-sin, sin], axis=1)
    const = lambda shp: pl.BlockSpec(shp, lambda i: (0,) * len(shp))
    return pl.pallas_call(
        _in_odd_kernel,
        out_shape=[jax.ShapeDtypeStruct((n, w.shape[1] - dk_all), BF), jax.ShapeDtypeStruct((batch, dk_all, t), BF)],
        grid=(n // tm,),
        in_specs=[pl.BlockSpec((tm, dm), lambda i: (i, 0)), const((1, dm)), const(w.shape), const(wkt.shape),
                  pl.BlockSpec((tm, RET_DK), lambda i: (i % nt, 0)), pl.BlockSpec((tm, RET_DK), lambda i: (i % nt, 0)),
                  pl.BlockSpec((half, tm), lambda i: (0, i % nt)), pl.BlockSpec((half, tm), lambda i: (0, i % nt)),
                  const((1, norm_gain.shape[0]))],
        out_specs=[pl.BlockSpec((tm, w.shape[1] - dk_all), lambda i: (i, 0)),
                   pl.BlockSpec((1, dk_all, tm), lambda i: (i // nt, 0, i % nt))],
        compiler_params=_params(1, VMEM_LIMIT),
        name="in_odd",
    )(x2, gain[None].astype(F32), w, wkt, cq, sq, cos.T, sin.T, norm_gain[None].astype(F32))


def _ret_kernel(lg_ref, q_ref, kt_ref, v_ref, g_ref, o_ref, sb_ref, sf_ref, sbc_ref, *, n_chunks):
    c = RET_CHUNK
    hd = pl.program_id(1)
    lgf = lg_ref[0, hd]
    lgb = lg_ref[1, hd]

    ii = lax.broadcasted_iota(jnp.int32, (c, c), 0)
    jj = lax.broadcasted_iota(jnp.int32, (c, c), 1)
    dist = (ii - jj).astype(F32)
    decay = jnp.where(ii >= jj, jnp.exp(lgf * jnp.maximum(dist, 0.0)),
                      jnp.exp(lgb * jnp.maximum(-dist, 0.0))).astype(BF)
    tok_col = lax.broadcasted_iota(jnp.int32, (c, RET_DK), 0).astype(F32)
    tok_row = lax.broadcasted_iota(jnp.int32, (RET_DK, c), 1).astype(F32)
    q_dec_f = jnp.exp(lgf * (tok_col + 1.0)).astype(BF)
    q_dec_b = jnp.exp(lgb * (c - tok_col)).astype(BF)
    k_dec_f = jnp.exp(lgf * (c - 1.0 - tok_row)).astype(BF)
    k_dec_b = jnp.exp(lgb * tok_row).astype(BF)
    chunk_f = jnp.exp(lgf * c)
    chunk_b = jnp.exp(lgb * c)

    sbc_ref[...] = jnp.zeros_like(sbc_ref)

    def back(i, carry):
        ci = n_chunks - 1 - i
        t0 = pl.multiple_of(ci * c, c)
        sb_ref[ci] = sbc_ref[...].astype(BF)
        kd = kt_ref[0, :, pl.ds(t0, c)] * k_dec_b
        sbc_ref[...] = sbc_ref[...] * chunk_b + _dot(kd, v_ref[0, pl.ds(t0, c), :])
        return carry

    lax.fori_loop(0, n_chunks, back, 0, unroll=4)

    sf_ref[...] = jnp.zeros_like(sf_ref)

    def fwd(ci, carry):
        t0 = pl.multiple_of(ci * c, c)
        qc = q_ref[0, pl.ds(t0, c), :]
        kc = kt_ref[0, :, pl.ds(t0, c)]
        vc = v_ref[0, pl.ds(t0, c), :]
        s = _dot(qc, kc).astype(BF) * decay
        qd = jnp.concatenate([qc * q_dec_f, qc * q_dec_b], axis=1)
        st = jnp.concatenate([sf_ref[...].astype(BF), sb_ref[ci]], axis=0)
        o = _dot(s, vc) + _dot(qd, st)
        sf_ref[...] = sf_ref[...] * chunk_f + _dot(kc * k_dec_f, vc)
        o = o * lax.rsqrt(jnp.mean(o * o, axis=-1, keepdims=True) + EPS)
        o_ref[0, pl.ds(t0, c), :] = o.astype(BF) * g_ref[0, pl.ds(t0, c), :]
        return carry

    lax.fori_loop(0, n_chunks, fwd, 0, unroll=4)


def _retention(qvg, kt, log_gamma, batch):
    n, wq = qvg.shape
    t = n // batch
    assert t % RET_CHUNK == 0
    n_chunks = t // RET_CHUNK
    qvg3 = qvg.reshape(batch, t, wq)
    v_off = (RET_HEADS * RET_DK) // RET_DV
    g_off = v_off + RET_HEADS
    return pl.pallas_call(
        functools.partial(_ret_kernel, n_chunks=n_chunks),
        out_shape=jax.ShapeDtypeStruct((batch, t, RET_HEADS * RET_DV), BF),
        grid_spec=pltpu.PrefetchScalarGridSpec(
            num_scalar_prefetch=1,
            grid=(batch, RET_HEADS),
            in_specs=[pl.BlockSpec((1, t, RET_DK), lambda b, h, lg: (b, 0, h)),
                      pl.BlockSpec((1, RET_DK, t), lambda b, h, lg: (b, h, 0)),
                      pl.BlockSpec((1, t, RET_DV), lambda b, h, lg: (b, 0, v_off + h)),
                      pl.BlockSpec((1, t, RET_DV), lambda b, h, lg: (b, 0, g_off + h))],
            out_specs=pl.BlockSpec((1, t, RET_DV), lambda b, h, lg: (b, 0, h)),
            scratch_shapes=[pltpu.VMEM((n_chunks, RET_DK, RET_DV), BF),
                            pltpu.VMEM((RET_DK, RET_DV), F32), pltpu.VMEM((RET_DK, RET_DV), F32)]),
        compiler_params=_params(2, VMEM_LIMIT),
        name="retention",
    )(log_gamma, qvg3, kt, qvg3, qvg3)


def _out_odd_kernel(y_ref, x_ref, w_ref, *route_refs):
    o_ref = route_refs[N_ROUTE_IN]
    x_new = x_ref[...] + _dot(y_ref[...], w_ref[...])
    o_ref[...] = x_new
    _route(x_new, *route_refs[:N_ROUTE_IN], *route_refs[N_ROUTE_IN + 1:])


def _out_odd(y, x2, w, router, tm=512):
    n, dm = x2.shape
    kdim = y.shape[1]
    r_ops, r_in, r_shapes, r_out, r_scratch = _router_io(n, dm, tm, *router)
    return pl.pallas_call(
        _out_odd_kernel,
        out_shape=[jax.ShapeDtypeStruct((n, dm), F32)] + r_shapes,
        grid=(n // tm,),
        in_specs=[pl.BlockSpec((tm, kdim), lambda i: (i, 0)), pl.BlockSpec((tm, dm), lambda i: (i, 0)),
                  pl.BlockSpec(w.shape, lambda i: (0, 0))] + r_in,
        out_specs=[pl.BlockSpec((tm, dm), lambda i: (i, 0))] + r_out,
        scratch_shapes=r_scratch,
        compiler_params=_params(1, VMEM_LIMIT),
        name="out_odd",
    )(y, x2, w, *r_ops)


def _route(x, g_ref, wh_ref, wl_ref, b_ref, tri_ref, o_ref, ot_ref, cnt_ref, run_ref):
    @pl.when(pl.program_id(0) == 0)
    def _():
        run_ref[...] = jnp.zeros_like(run_ref)

    h = _rms(x, g_ref[...])
    hh, hl = _split_bf16(h)
    logits = _dot(hh, wh_ref[...]) + _dot(hh, wl_ref[...]) + _dot(hl, wh_ref[...]) + b_ref[...]
    sub = tri_ref.shape[0]
    tri = tri_ref[...]
    run = run_ref[...]
    for r0 in range(0, x.shape[0], sub):
        info, run = _route_rows(logits[r0:r0 + sub], tri, run)
        o_ref[r0:r0 + sub, :] = info
        ot_ref[:, r0:r0 + sub] = info.T[0:ROW_ALIGN, :]
    run_ref[...] = run
    cnt_ref[...] = run


def _route_rows(logits, tri, run):
    lane = lax.broadcasted_iota(jnp.int32, logits.shape, 1).astype(F32)
    neg = jnp.float32(-jnp.inf)

    def top(mask):
        val = jnp.max(jnp.where(mask, logits, neg), axis=-1, keepdims=True)
        idx = jnp.min(jnp.where(mask & (logits == val), lane, 1e6), axis=-1, keepdims=True)
        return val, idx

    gmask = lane < MOE_GROUPS
    gmax, gidx = top(gmask)
    gprob = 1.0 / jnp.sum(jnp.where(gmask, jnp.exp(logits - gmax), 0.0), axis=-1, keepdims=True)
    base = MOE_GROUPS + MOE_PER_GROUP * gidx
    emask = (lane >= base) & (lane < base + MOE_PER_GROUP)
    v1, i1 = top(emask)
    v2, i2 = top(emask & (lane != i1))
    e2 = jnp.exp(v2 - v1)
    w1 = gprob / (1.0 + e2)
    w2 = gprob * e2 / (1.0 + e2)
    a1 = i1 - base
    a2 = i2 - base
    lo = jnp.minimum(a1, a2)
    hi = jnp.maximum(a1, a2)
    pair = lo * (7.0 - lo) * 0.5 + hi - lo - 1.0
    cls = gidx * MOE_PAIRS + pair
    w_lo = jnp.where(a1 < a2, w1, w2)
    w_hi = jnp.where(a1 < a2, w2, w1)
    onehot = lane == cls
    earlier = _dot(tri, jnp.where(onehot, 1.0, 0.0).astype(BF)) + run
    rank = jnp.sum(jnp.where(onehot, earlier, 0.0), axis=-1, keepdims=True)
    info = jnp.where(lane == 0, cls, jnp.where(lane == 1, w_lo, jnp.where(lane == 2, w_hi,
                     jnp.where(lane == 3, rank, 0.0))))
    return info, run + jnp.sum(jnp.where(onehot, 1.0, 0.0), axis=0, keepdims=True)


ROUTE_ROWS = 128
N_ROUTE_IN = 5


def _router_io(n, dm, tm, gain, w_hi, w_lo, b_r):
    tri = (jnp.arange(ROUTE_ROWS)[:, None] > jnp.arange(ROUTE_ROWS)[None, :]).astype(BF)
    const = lambda shp: pl.BlockSpec(shp, lambda i: (0, 0))
    operands = [gain[None].astype(F32), w_hi, w_lo, b_r, tri]
    in_specs = [const((1, dm)), const((dm, 128)), const((dm, 128)), const((1, 128)), const(tri.shape)]
    out_shapes = [jax.ShapeDtypeStruct((n, 128), F32), jax.ShapeDtypeStruct((ROW_ALIGN, n), F32),
                  jax.ShapeDtypeStruct((1, 128), F32)]
    out_specs = [pl.BlockSpec((tm, 128), lambda i: (i, 0)), pl.BlockSpec((ROW_ALIGN, tm), lambda i: (0, i)),
                 const((1, 128))]
    return operands, in_specs, out_shapes, out_specs, [pltpu.VMEM((1, 128), F32)]


def _start_row_dmas(n, copy_of_row):
    for r in range(n):
        copy_of_row(r).start()


def _dispatch_kernel(fpos_ref, fon_ref, pos_ref, x_ref, info_ref, o_hbm, aug_ref, zero_ref, sem, row_sem):
    dm = x_ref.shape[-1]
    tm = info_ref.shape[0]

    @pl.when(pl.program_id(0) == 0)
    def _():
        zero_ref[...] = jnp.zeros_like(zero_ref)
        n_fill = fpos_ref.shape[0]

        def fill(k):
            return pltpu.make_async_copy(
                zero_ref, o_hbm.at[pl.ds(pl.multiple_of(fpos_ref[k], ROW_ALIGN), MOE_FILL), :], sem)

        for k in range(n_fill - 1):
            pl.when(fon_ref[k] > 0)(lambda k=k: fill(k).start())
        for k in range(n_fill - 1):
            pl.when(fon_ref[k] > 0)(lambda k=k: fill(k).wait())
        fill(n_fill - 1).start()
        fill(n_fill - 1).wait()

    i = pl.program_id(0)
    n_steps = pl.num_programs(0)
    slot = i % 2

    def wait_rows(s):
        pltpu.make_async_copy(aug_ref.at[s], o_hbm.at[pl.ds(0, tm), :], row_sem.at[s]).wait()

    pl.when(i >= 2)(lambda: wait_rows(slot))
    aug_ref[slot, :, 0:dm] = x_ref[...].reshape(tm, dm)
    aug_ref[slot, :, dm:] = info_ref[...]
    _start_row_dmas(tm, lambda r: pltpu.make_async_copy(
        aug_ref.at[slot, pl.ds(r, 1), :], o_hbm.at[pl.ds(pos_ref[0, 0, r], 1), :], row_sem.at[slot]))

    @pl.when(i == n_steps - 1)
    def _():
        pl.when(n_steps >= 2)(lambda: wait_rows(1 - slot))
        wait_rows(slot)


def _token_tiles(x3, nb, tq):
    if nb == x3.shape[0]:
        return (lambda *shape_tail: pl.BlockSpec((nb, tq) + shape_tail, lambda i, *_: (0, i) + (0,) * len(shape_tail)),
                x3.shape[1] // tq)
    assert nb == 1 and tq == x3.shape[1]
    return (lambda *shape_tail: pl.BlockSpec((1, tq) + shape_tail, lambda i, *_: (i, 0) + (0,) * len(shape_tail)),
            x3.shape[0])


def _dispatch(x3, nb, tq, info, pos, first_pad, n_rows):
    dm = x3.shape[-1]
    n = info.shape[0]
    tm = nb * tq
    wa = dm + info.shape[1]
    spec, n_steps = _token_tiles(x3, nb, tq)
    class_fill = first_pad // ROW_ALIGN * ROW_ALIGN
    tail = class_fill[-1] + MOE_FILL + jnp.arange((n_rows - n) // MOE_FILL + 1, dtype=jnp.int32) * MOE_FILL
    tail_on = tail + MOE_FILL <= n_rows
    fpos = jnp.concatenate([class_fill, jnp.where(tail_on, tail, 0), jnp.array([n_rows - MOE_FILL], jnp.int32)])
    fon = jnp.concatenate([jnp.ones_like(class_fill), tail_on.astype(jnp.int32), jnp.ones((1,), jnp.int32)])
    return pl.pallas_call(
        _dispatch_kernel,
        out_shape=jax.ShapeDtypeStruct((n_rows, wa), F32),
        grid_spec=pltpu.PrefetchScalarGridSpec(
            num_scalar_prefetch=2,
            grid=(n_steps,),
            in_specs=[pl.BlockSpec((1, 1, tm), lambda i, fp, fo: (i, 0, 0), memory_space=pltpu.SMEM),
                      spec(dm),
                      pl.BlockSpec((tm, info.shape[1]), lambda i, fp, fo: (i, 0))],
            out_specs=pl.BlockSpec(memory_space=pl.ANY),
            scratch_shapes=[pltpu.VMEM((2, tm, wa), F32), pltpu.VMEM((MOE_FILL, wa), F32),
                            pltpu.SemaphoreType.DMA, pltpu.SemaphoreType.DMA((2,))]),
        compiler_params=_params(1, VMEM_LIMIT),
        name="moe_dispatch",
    )(fpos.astype(jnp.int32), fon, pos.reshape(n_steps, 1, tm), x3, info)


def _expert_kernel(ea_ref, eb_ref, act_ref, xa_ref, g_ref,
                   wga_ref, wua_ref, wda_ref, wgb_ref, wub_ref, wdb_ref, o_ref):
    i = pl.program_id(0)
    dm = o_ref.shape[1]

    @pl.when(act_ref[i] > 0)
    def _():
        h = _rms(xa_ref[:, 0:dm], g_ref[...]).astype(BF)

        def ffn(wg, wu, wd, wrow):
            gate = _dot(h, wg[0])
            hid = gate * _sigmoid(gate) * _dot(h, wu[0])
            return _dot((hid * wrow).astype(BF), wd[0])

        o_ref[...] = (ffn(wga_ref, wua_ref, wda_ref, xa_ref[:, dm + 1:dm + 2])
                      + ffn(wgb_ref, wub_ref, wdb_ref, xa_ref[:, dm + 2:dm + 3]))

    @pl.when(act_ref[i] == 0)
    def _():
        o_ref[...] = jnp.zeros_like(o_ref)


def _experts(xa_sorted, gain, tile_ea, tile_eb, tile_act, w_gate, w_up, w_down):
    dm = w_gate.shape[1]
    wa = xa_sorted.shape[1]
    n_tiles = tile_ea.shape[0]
    tm = MOE_TILE
    ff = w_gate.shape[-1]
    wsel = lambda shp, which: pl.BlockSpec((1,) + shp, lambda i, ea, eb, act: ((ea, eb)[which][i], 0, 0))
    return pl.pallas_call(
        _expert_kernel,
        out_shape=jax.ShapeDtypeStruct((n_tiles * tm, dm), F32),
        grid_spec=pltpu.PrefetchScalarGridSpec(
            num_scalar_prefetch=3,
            grid=(n_tiles,),
            in_specs=[pl.BlockSpec((tm, wa), lambda i, ea, eb, act: (i * act[i], 0)),
                      pl.BlockSpec((1, dm), lambda i, ea, eb, act: (0, 0)),
                      wsel((dm, ff), 0), wsel((dm, ff), 0), wsel((ff, dm), 0),
                      wsel((dm, ff), 1), wsel((dm, ff), 1), wsel((ff, dm), 1)],
            out_specs=pl.BlockSpec((tm, dm), lambda i, ea, eb, act: (i, 0))),
        compiler_params=_params(1, VMEM_LIMIT),
        name="moe_experts",
    )(tile_ea, tile_eb, tile_act, xa_sorted, gain, w_gate, w_up, w_down, w_gate, w_up, w_down)


def _combine_kernel(pos_ref, nxt_ref, x_ref, y_hbm, o_ref, ybuf, sem):
    i = pl.program_id(0)
    n_steps = pl.num_programs(0)
    tm = ybuf.shape[1]
    slot = i % 2

    def gather(idx_ref, s):
        _start_row_dmas(tm, lambda r: pltpu.make_async_copy(
            y_hbm.at[pl.ds(idx_ref[0, 0, r], 1), :], ybuf.at[s, pl.ds(r, 1), :], sem.at[s]))

    @pl.when(i == 0)
    def _():
        gather(pos_ref, 0)

    @pl.when(i + 1 < n_steps)
    def _():
        gather(nxt_ref, 1 - slot)

    pltpu.make_async_copy(y_hbm.at[pl.ds(0, tm), :], ybuf.at[slot], sem.at[slot]).wait()
    o_ref[...] = x_ref[...] + ybuf[slot].reshape(x_ref.shape)


def _combine(x3, nb, tq, y_sorted, pos):
    dm = x3.shape[-1]
    tm = nb * tq
    spec, n_steps = _token_tiles(x3, nb, tq)
    pos3 = pos.reshape(n_steps, 1, tm)
    return pl.pallas_call(
        _combine_kernel,
        out_shape=jax.ShapeDtypeStruct(x3.shape, F32),
        grid=(n_steps,),
        in_specs=[pl.BlockSpec((1, 1, tm), lambda i: (i, 0, 0), memory_space=pltpu.SMEM),
                  pl.BlockSpec((1, 1, tm), lambda i: (jnp.minimum(i + 1, n_steps - 1), 0, 0), memory_space=pltpu.SMEM),
                  spec(dm),
                  pl.BlockSpec(memory_space=pl.ANY)],
        out_specs=spec(dm),
        scratch_shapes=[pltpu.VMEM((2, tm, dm), F32), pltpu.SemaphoreType.DMA((2,))],
        compiler_params=_params(1, VMEM_LIMIT),
        name="moe_combine",
    )(pos3, pos3, x3, y_sorted)


def _moe(x3, nb, tq, info, info_rows, cnt, gain, layer, w_gate, w_up, w_down):
    n = info.shape[0]
    tm = MOE_TILE
    gain = gain[None].astype(F32)
    cls = info_rows[0].astype(jnp.int32)
    counts = cnt[0, :MOE_CLASSES].astype(jnp.int32)
    padded = ((counts + tm - 1) // tm) * tm
    ends = jnp.cumsum(padded)
    starts = ends - padded
    pos = starts[cls] + info_rows[3].astype(jnp.int32)
    n_tiles = n // tm + MOE_CLASSES
    tile_start = jnp.arange(n_tiles, dtype=jnp.int32) * tm
    tile_cls = jnp.minimum(jnp.sum(tile_start[:, None] >= ends[None, :], axis=1), MOE_CLASSES - 1)
    tile_act = (tile_start < ends[-1]).astype(jnp.int32)
    pair_lo = jnp.array([0, 0, 0, 1, 1, 2], jnp.int32)
    pair_hi = jnp.array([1, 2, 3, 2, 3, 3], jnp.int32)
    first = layer * MOE_GROUPS * MOE_PER_GROUP + (tile_cls // MOE_PAIRS) * MOE_PER_GROUP
    tile_ea = (first + pair_lo[tile_cls % MOE_PAIRS]).astype(jnp.int32)
    tile_eb = (first + pair_hi[tile_cls % MOE_PAIRS]).astype(jnp.int32)
    xa_sorted = _dispatch(x3, nb, tq, info, pos, (starts + counts).astype(jnp.int32), (n_tiles + 2) * tm)
    y_sorted = _experts(xa_sorted, gain, tile_ea, tile_eb, tile_act, w_gate, w_up, w_down)
    return _combine(x3, nb, tq, y_sorted, pos)


def _router_weights(w_group, b_group, w_router, b_router):
    dm = w_group.shape[0]
    w = jnp.concatenate([w_group.astype(F32), w_router.astype(F32).reshape(dm, -1)], axis=1)
    b = jnp.concatenate([b_group.astype(F32), b_router.astype(F32).reshape(-1)])
    pad = 128 - w.shape[1]
    w_hi, w_lo = _split_bf16(jnp.pad(w, ((0, 0), (0, pad))))
    return w_hi, w_lo, jnp.pad(b, (0, pad))[None]


def _trunk(x, p):
    b, t, dm = x.shape
    n = b * t

    tq = _even_tile(b)
    ug, q, k, v = _in_even(x, p["norm_mix"][0], p["w_in_even"], p["na_q_gain"], p["na_k_gain"])
    yg = _s5_scan(ug, *p["s5_ops"], p["s5_d_rows"], t // S5_CHUNK, b)
    att = _na(q, k, v, p["na_table"])
    x3, *routing = _out_even(yg, att, x, p["s5_w_glu"], p["s5_b_glu"], p["w_out_even"],
                             (p["norm_ffn"][0], *p["router"][0]))
    x3 = _moe(x3, b, tq, *routing, p["norm_ffn"][0], 0, *p["experts"])

    x2 = x3.reshape(n, dm)
    qvg, kt = _in_odd(x2, p["norm_mix"][1], p["w_in_odd"], p["w_k_t"], p["ret_norm_gain"], b)
    yr = _retention(qvg, kt, p["log_gamma"], b)
    tm = 512
    x2, *routing = _out_odd(yr.reshape(n, -1), x2, p["w_out_odd"], (p["norm_ffn"][1], *p["router"][1]), tm)
    x3 = _moe(x2.reshape(n // tm, tm, dm), 1, tm, *routing, p["norm_ffn"][1], 1, *p["experts"])
    return x3.reshape(b, t, dm)


def kernel(x_prompt, x_sample, norm_mix, norm_ffn, w_in_even, w_out_even, s5_lambda_re, s5_lambda_im, s5_log_dt, s5_b_re, s5_b_im, s5_c_re, s5_c_im, s5_d, s5_w_glu, s5_b_glu, na_q_gain, na_k_gain, na_rpb, w_in_odd, w_out_odd, ret_decay_logit, ret_norm_gain, moe_w_group, moe_b_group, moe_w_router, moe_b_router, moe_w_gate, moe_w_up, moe_w_down):
    assert norm_mix.shape[0] == 2
    dk_all = RET_HEADS * RET_DK
    w_odd = w_in_odd[0].astype(BF)
    p = {
        "norm_mix": norm_mix, "norm_ffn": norm_ffn,
        "w_in_even": w_in_even[0].astype(BF), "w_out_even": w_out_even[0].astype(BF),
        "s5_ops": _s5_operators(s5_lambda_re[0], s5_lambda_im[0], s5_log_dt[0], s5_b_re[0], s5_b_im[0],
                                s5_c_re[0], s5_c_im[0]),
        "s5_d_rows": jnp.tile(s5_d[0].astype(F32).reshape(S5_GROUPS, 1, S5_GROUP), (1, 1, S5_CHUNK)),
        "s5_w_glu": s5_w_glu[0].astype(BF), "s5_b_glu": s5_b_glu[0],
        "na_q_gain": na_q_gain[0], "na_k_gain": na_k_gain[0], "na_table": _na_bias_table(na_rpb[0]),
        "w_in_odd": w_odd, "w_k_t": w_in_odd[0][:, dk_all:2 * dk_all].T.astype(BF),
        "w_out_odd": w_out_odd[0].astype(BF),
        "log_gamma": jax.nn.log_sigmoid(ret_decay_logit[0].astype(F32)),
        "ret_norm_gain": ret_norm_gain[0],
        "router": [_router_weights(moe_w_group[l], moe_b_group[l], moe_w_router[l], moe_b_router[l]) for l in range(2)],
        "experts": tuple(w.astype(BF).reshape((-1,) + w.shape[2:]) for w in (moe_w_gate, moe_w_up, moe_w_down)),
    }
    return _trunk(x_prompt, p), _trunk(x_sample, p)
```

```python
import functools
import math

import jax
import jax.numpy as jnp
from jax import lax
from jax.experimental import pallas as pl
from jax.experimental.pallas import tpu as pltpu

F32 = jnp.float32
BF = jnp.bfloat16
EPS = 1e-6

D_MODEL = 1024
GRID_W = 64
S5_GROUPS = 32
S5_GROUP = 16
S5_STATE = 64
S5_CHUNK = 16
NA_HEADS = 8
NA_HEAD_DIM = 64
NA_WIN_R = 8
NA_WIN_C = 16
NA_HEAD_GROUP = 4
RET_HEADS = 8
RET_DK = 128
RET_DV = 256
RET_CHUNK = 256
ROPE_BASE = 10000.0
MOE_GROUPS = 4
MOE_PER_GROUP = 4
MOE_FF = 512
MOE_PAIRS = 6
MOE_CLASSES = MOE_GROUPS * MOE_PAIRS
MOE_TILE = 256
ROW_ALIGN = 8
MOE_FILL = MOE_TILE + ROW_ALIGN
MASK_NEG = -1e30
VMEM_LIMIT = 56 * 1024 * 1024


def _params(n_axes, vmem=None):
    return pltpu.CompilerParams(dimension_semantics=("arbitrary",) * n_axes,
                                vmem_limit_bytes=vmem)


def _rms(x, gain):
    ms = jnp.mean(x * x, axis=-1, keepdims=True)
    return x * lax.rsqrt(ms + EPS) * gain


def _sigmoid(x):
    return 1.0 / (1.0 + jnp.exp(-x))


def _dot(a, b):
    return jnp.dot(a, b, preferred_element_type=F32)


def _dot_nt(a, b):
    return lax.dot_general(a, b, (((1,), (1,)), ((), ())), preferred_element_type=F32)


def _split_bf16(x):
    hi = x.astype(BF)
    lo = (x - hi.astype(F32)).astype(BF)
    return hi, lo


def _dot3(a, b):
    ah, al = _split_bf16(a)
    bh, bl = _split_bf16(b)
    return _dot(ah, bh) + _dot(ah, bl) + _dot(al, bh)


def _chunk_major_perm(batch, tq):
    nc = tq // S5_CHUNK
    r = jnp.arange(batch * tq)
    j, c, b = r // (nc * batch), (r // batch) % nc, r % batch
    src = b * tq + c * S5_CHUNK + j
    return (src[:, None] == jnp.arange(batch * tq)[None, :]).astype(BF)


def _in_even_kernel(x_ref, g_ref, w_ref, qg_ref, kg_ref, seg_ref, perm_ref, ug_ref, q_ref, k_ref, v_ref):
    batch, tq, dm = x_ref.shape
    h = _rms(x_ref[...].reshape(batch * tq, dm), g_ref[...]).astype(BF)
    d = q_ref.shape[-1]
    nr = ug_ref.shape[1]
    up = _dot(perm_ref[...], _dot(h, w_ref[:, 0:d]).astype(BF))
    for g in range(S5_GROUPS):
        lanes = slice(g * S5_GROUP, (g + 1) * S5_GROUP)
        ug_ref[g] = jnp.concatenate([up[j * nr:(j + 1) * nr, lanes] for j in range(S5_CHUNK)], axis=1).astype(BF)
    seg = seg_ref[...]

    def head_norm(z, gain):
        ssq = _dot((z * z).astype(BF), seg) * (1.0 / NA_HEAD_DIM)
        return z * lax.rsqrt(ssq + EPS) * gain

    q = _dot(h, w_ref[:, d:2 * d])
    q_ref[...] = (head_norm(q, qg_ref[...]) * (NA_HEAD_DIM ** -0.5)).astype(BF).reshape(batch, tq, d)
    k = _dot(h, w_ref[:, 2 * d:3 * d])
    k_ref[...] = head_norm(k, kg_ref[...]).astype(BF).reshape(batch, tq, d)
    v_ref[...] = _dot(h, w_ref[:, 3 * d:4 * d]).astype(BF).reshape(batch, tq, d)


def _even_tile(batch, tm=512):
    tq = tm // batch
    assert tq * batch == tm and tq % S5_CHUNK == 0
    return tq


def _in_even(x3, gain, w, q_gain, k_gain):
    batch, t, dm = x3.shape
    d = w.shape[1] // 4
    tq = _even_tile(batch)
    nr = (tq // S5_CHUNK) * batch
    seg = (jnp.arange(d)[:, None] // NA_HEAD_DIM == jnp.arange(d)[None, :] // NA_HEAD_DIM).astype(BF)
    qg = jnp.tile(q_gain.astype(F32), NA_HEADS)[None]
    kg = jnp.tile(k_gain.astype(F32), NA_HEADS)[None]
    tok = lambda width: pl.BlockSpec((batch, tq, width), lambda i: (0, i, 0))
    full = lambda shp: pl.BlockSpec(shp, lambda i: (0,) * len(shp))
    gw = S5_CHUNK * S5_GROUP
    perm = _chunk_major_perm(batch, tq)
    return pl.pallas_call(
        _in_even_kernel,
        out_shape=[jax.ShapeDtypeStruct((S5_GROUPS, (t // S5_CHUNK) * batch, gw), BF)]
        + [jax.ShapeDtypeStruct((batch, t, d), BF)] * 3,
        grid=(t // tq,),
        in_specs=[tok(dm), full((1, dm)), full(w.shape), full((1, d)), full((1, d)), full((d, d)), full(perm.shape)],
        out_specs=[pl.BlockSpec((S5_GROUPS, nr, gw), lambda i: (0, i, 0)), tok(d), tok(d), tok(d)],
        compiler_params=_params(1, VMEM_LIMIT),
        name="in_even",
    )(x3, gain[None].astype(F32), w, qg, kg, seg, perm)


def _s5_taps_kernel(l_ref, r_ref, o_ref):
    for g in range(l_ref.shape[0]):
        o_ref[g] = _dot3(l_ref[g], r_ref[g])


def _s5_taps(lhs, rhs, per_step=8):
    g2, m, k = lhs.shape
    nn = rhs.shape[-1]
    return pl.pallas_call(
        _s5_taps_kernel,
        out_shape=jax.ShapeDtypeStruct((g2, m, nn), F32),
        grid=(g2 // per_step,),
        in_specs=[pl.BlockSpec((per_step, m, k), lambda i: (i, 0, 0)), pl.BlockSpec((per_step, k, nn), lambda i: (i, 0, 0))],
        out_specs=pl.BlockSpec((per_step, m, nn), lambda i: (i, 0, 0)),
        compiler_params=_params(1),
        name="s5_taps",
    )(lhs, rhs)


def _s5_operators(lam_re, lam_im, log_dt, b_re, b_im, c_re, c_im):
    L, G, H, P_ = S5_CHUNK, S5_GROUPS, S5_GROUP, S5_STATE
    lr, li = lam_re.astype(F32), lam_im.astype(F32)
    dt = jnp.exp(log_dt.astype(F32))[..., None]
    kk = jnp.arange(L + 1, dtype=F32)[:, None, None, None]
    mag = jnp.exp(kk * (lr * dt)[None])
    ang = kk * (li * dt)[None]
    pw = (mag * jnp.cos(ang), mag * jnp.sin(ang))

    def cmul(x, y):
        return x[0] * y[0] - x[1] * y[1], x[0] * y[1] + x[1] * y[0]

    xr, xi = pw[0][1] - 1.0, pw[1][1]
    den = lr * lr + li * li
    coef = ((xr * lr + xi * li) / den, (xi * lr - xr * li) / den)
    bbar = cmul((coef[0][..., None], coef[1][..., None]), (b_re.astype(F32), b_im.astype(F32)))
    c = (c_re.astype(F32), c_im.astype(F32))

    def at(k_slice, direction, perm):
        return tuple(jnp.transpose(z[k_slice, direction], perm) for z in pw)

    apl = tuple(jnp.transpose(z[:L], (1, 2, 0, 3))[:, :, :, None, :] for z in pw)
    ce = cmul((c[0][:, :, None], c[1][:, :, None]), apl)
    lhs = jnp.concatenate([ce[0], -ce[1]], axis=-1).reshape(2 * G, L * H, 2 * P_)
    rhs = jnp.concatenate([bbar[0], bbar[1]], axis=2).reshape(2 * G, 2 * P_, H)
    taps = _s5_taps(lhs, rhs).reshape(2, G, L, H, H)
    taps = jnp.transpose(taps, (0, 1, 2, 4, 3))
    kf, kb = taps[0], taps[1]
    lagged = jnp.concatenate([jnp.flip(kb[:, 1:], axis=1), (kf[:, 0] + kb[:, 0])[:, None], kf[:, 1:]], axis=1)
    idx = jnp.arange(L)[None, :] - jnp.arange(L)[:, None] + (L - 1)
    m = jnp.transpose(lagged[:, idx], (0, 1, 3, 2, 4)).reshape(G, L * H, L * H)

    bt = tuple(jnp.transpose(z, (0, 1, 3, 2)) for z in bbar)
    expand_j = lambda z: z[:, :, None, :]
    pf = cmul(tuple(map(expand_j, at(slice(L - 1, None, -1), 0, (1, 0, 2)))), (bt[0][0][:, None], bt[1][0][:, None]))
    pb = cmul(tuple(map(expand_j, at(slice(0, L), 1, (1, 0, 2)))), (bt[0][1][:, None], bt[1][1][:, None]))
    p = jnp.concatenate([pf[0], pb[0], pf[1], pb[1]], axis=-1).reshape(G, L * H, 4 * P_)

    ct = tuple(jnp.transpose(z, (0, 1, 3, 2)) for z in c)
    expand_h = lambda z: z[..., None]
    gf = cmul(tuple(map(expand_h, at(slice(1, None), 0, (1, 2, 0)))), (ct[0][0][:, :, None, :], ct[1][0][:, :, None, :]))
    gb = cmul(tuple(map(expand_h, at(slice(L, 0, -1), 1, (1, 2, 0)))), (ct[0][1][:, :, None, :], ct[1][1][:, :, None, :]))
    q = jnp.concatenate([gf[0], gb[0], -gf[1], -gb[1]], axis=1).reshape(G, 4 * P_, L * H)

    a_mat = jnp.stack([jnp.concatenate([pw[0][L, 0], pw[0][L, 1]], -1),
                       jnp.concatenate([pw[1][L, 0], pw[1][L, 1]], -1)], axis=1)
    return m.astype(BF), p.astype(BF), q.astype(BF), a_mat.astype(F32)


def _s5_kernel(u_ref, m_ref, p_ref, q_ref, a_ref, d_ref, y_ref, z_ref, s_ref, *, n_chunks, batch):
    u = u_ref[0]
    z_ref[...] = _dot(u, p_ref[0])
    half = 2 * S5_STATE
    ar = a_ref[0, 0:1, :]
    ai = a_ref[0, 1:2, :]
    fwd_lane = lax.broadcasted_iota(jnp.int32, (batch, half), 1) < S5_STATE

    def step(k, carry):
        sr, si = carry
        f0 = pl.multiple_of(k * batch, batch)
        b0 = pl.multiple_of((n_chunks - 1 - k) * batch, batch)
        s_ref[pl.ds(f0, batch), 0:S5_STATE] = sr[:, 0:S5_STATE]
        s_ref[pl.ds(b0, batch), S5_STATE:half] = sr[:, S5_STATE:half]
        s_ref[pl.ds(f0, batch), half:half + S5_STATE] = si[:, 0:S5_STATE]
        s_ref[pl.ds(b0, batch), half + S5_STATE:2 * half] = si[:, S5_STATE:half]
        zr = jnp.where(fwd_lane, z_ref[pl.ds(f0, batch), 0:half], z_ref[pl.ds(b0, batch), 0:half])
        zi = jnp.where(fwd_lane, z_ref[pl.ds(f0, batch), half:2 * half], z_ref[pl.ds(b0, batch), half:2 * half])
        return ar * sr - ai * si + zr, ar * si + ai * sr + zi

    zero = jnp.zeros((batch, half), F32)
    lax.fori_loop(0, n_chunks, step, (zero, zero), unroll=8)
    y = _dot(u, m_ref[0]) + _dot(s_ref[...].astype(BF), q_ref[0]) + u.astype(F32) * d_ref[0]
    y_ref[0] = y.astype(BF)


def _s5_scan(ug, m, p, q, a_mat, d_skip, n_chunks, batch):
    g, r, w = ug.shape
    blk = lambda shp: pl.BlockSpec((1,) + shp, lambda i: (i, 0, 0))
    return pl.pallas_call(
        functools.partial(_s5_kernel, n_chunks=n_chunks, batch=batch),
        out_shape=jax.ShapeDtypeStruct((g, r, w), BF),
        grid=(g,),
        in_specs=[blk((r, w)), blk((w, w)), blk((w, w)), blk((w, w)), blk((2, w // 2)), blk((1, w))],
        out_specs=blk((r, w)),
        scratch_shapes=[pltpu.VMEM((r, w), F32), pltpu.VMEM((r, w), F32)],
        compiler_params=_params(1, VMEM_LIMIT),
        name="s5_scan",
    )(ug, m, p, q, a_mat, d_skip)


def _na_bias_table(rpb):
    qc = jnp.arange(GRID_W)[:, None]
    kc = jnp.arange(GRID_W)[None, :]
    ws = jnp.clip(qc - NA_WIN_C // 2, 0, GRID_W - NA_WIN_C)
    valid = (kc >= ws) & (kc < ws + NA_WIN_C)
    n_tap = 2 * NA_WIN_C - 1
    pad = GRID_W - NA_WIN_C
    row = n_tap + 2 * pad
    taps = jnp.pad(rpb.astype(F32), ((0, 0), (0, 0), (pad, pad)))
    skew = jnp.tile(taps, (1, 1, GRID_W))[:, :, GRID_W - 1:GRID_W - 1 + GRID_W * (row - 1)]
    bias = skew.reshape(rpb.shape[0], rpb.shape[1], GRID_W, row - 1)[..., :GRID_W]
    e = jnp.where(valid[None, None], bias, MASK_NEG)
    return jnp.concatenate([e[:, :-1], e[:, 1:]], axis=-1)


def _na_kernel(q_ref, k_ref, v_ref, e_ref, o_ref, *, rows):
    hg_w = NA_HEAD_GROUP * NA_HEAD_DIM
    lane_head = lax.broadcasted_iota(jnp.int32, (GRID_W, hg_w), 1) // NA_HEAD_DIM
    n_keys = NA_WIN_R * GRID_W

    def row_body(r, carry):
        rs = jnp.clip(r - NA_WIN_R // 2, 0, rows - NA_WIN_R)
        di = rs - r + (NA_WIN_R - 1)
        q0 = pl.multiple_of(r * GRID_W, GRID_W)
        k0 = pl.multiple_of(rs * GRID_W, GRID_W)
        for hg in range(NA_HEADS // NA_HEAD_GROUP):
            lanes = slice(hg * hg_w, (hg + 1) * hg_w)
            q4 = q_ref[0, pl.ds(q0, GRID_W), lanes]
            qs = jnp.concatenate([jnp.where(lane_head == hh, q4, jnp.zeros_like(q4))
                                  for hh in range(NA_HEAD_GROUP)], axis=0)
            k4 = k_ref[0, pl.ds(k0, n_keys), lanes]
            v4 = v_ref[0, pl.ds(k0, n_keys), lanes]
            s = _dot_nt(qs, k4)
            bias = jnp.concatenate(
                [jnp.concatenate([e_ref[hg * NA_HEAD_GROUP + hh, di + 2 * w2] for w2 in range(NA_WIN_R // 2)], axis=1)
                 for hh in range(NA_HEAD_GROUP)], axis=0)
            s = s + bias
            mx = jnp.max(s, axis=-1, keepdims=True)
            p = jnp.exp(s - mx)
            den = jnp.sum(p, axis=-1, keepdims=True)
            o = _dot(p.astype(BF), v4) * (1.0 / den)
            out = jnp.zeros((GRID_W, hg_w), F32)
            for hh in range(NA_HEAD_GROUP):
                out = jnp.where(lane_head == hh, o[hh * GRID_W:(hh + 1) * GRID_W], out)
            o_ref[0, pl.ds(q0, GRID_W), lanes] = out.astype(BF)
        return carry

    lax.fori_loop(0, rows, row_body, 0, unroll=8)


def _na(q, k, v, table):
    b, t, d = q.shape
    rows = t // GRID_W
    assert rows >= NA_WIN_R and t % GRID_W == 0
    seq = pl.BlockSpec((1, t, d), lambda i: (i, 0, 0))
    return pl.pallas_call(
        functools.partial(_na_kernel, rows=rows),
        out_shape=jax.ShapeDtypeStruct((b, t, d), BF),
        grid=(b,),
        in_specs=[seq, seq, seq, pl.BlockSpec(table.shape, lambda i: (0, 0, 0, 0))],
        out_specs=seq,
        compiler_params=_params(1, VMEM_LIMIT),
        name="na",
    )(q, k, v, table)


def _out_even_kernel(yg_ref, a_ref, x_ref, wg_ref, bg_ref, wo_ref, perm_ref, *route_refs):
    o_ref = route_refs[N_ROUTE_IN]
    batch, tq, dm = x_ref.shape
    rows = [jnp.concatenate([yg_ref[g, :, i * S5_GROUP:(i + 1) * S5_GROUP] for g in range(S5_GROUPS)], axis=1)
            for i in range(S5_CHUNK)]
    ys = _dot(perm_ref[...], jnp.concatenate(rows, axis=0))
    c0 = math.sqrt(2.0 / math.pi)
    ys = 0.5 * ys * (1.0 + jnp.tanh(c0 * (ys + 0.044715 * (ys * ys * ys))))
    gate = _sigmoid(_dot(ys.astype(BF), wg_ref[...]) + bg_ref[...])
    a_out = (ys * gate).astype(BF)
    da = a_out.shape[-1]
    att = a_ref[...].reshape(batch * tq, da)
    x_new = x_ref[...].reshape(batch * tq, dm) + _dot(a_out, wo_ref[0:da, :]) + _dot(att, wo_ref[da:, :])
    o_ref[...] = x_new.reshape(batch, tq, dm)
    _route(x_new, *route_refs[:N_ROUTE_IN], *route_refs[N_ROUTE_IN + 1:])


def _out_even(yg, att, x3, w_glu, b_glu, w_out, router):
    batch, t, dm = x3.shape
    d = att.shape[-1]
    groups, _, gw = yg.shape
    tq = _even_tile(batch)
    tm = batch * tq
    nr = (tq // S5_CHUNK) * batch
    tok = lambda width: pl.BlockSpec((batch, tq, width), lambda i: (0, i, 0))
    full = lambda shp: pl.BlockSpec(shp, lambda i: (0,) * len(shp))
    r_ops, r_in, r_shapes, r_out, r_scratch = _router_io(batch * t, dm, tm, *router)
    perm = _chunk_major_perm(batch, tq).T
    return pl.pallas_call(
        _out_even_kernel,
        out_shape=[jax.ShapeDtypeStruct((batch, t, dm), F32)] + r_shapes,
        grid=(t // tq,),
        in_specs=[pl.BlockSpec((groups, nr, gw), lambda i: (0, i, 0)), tok(d), tok(dm),
                  full((d, d)), full((1, d)), full(w_out.shape), full(perm.shape)] + r_in,
        out_specs=[tok(dm)] + r_out,
        scratch_shapes=r_scratch,
        compiler_params=_params(1, VMEM_LIMIT),
        name="out_even",
    )(yg, att, x3, w_glu, b_glu[None].astype(F32), w_out, perm, *r_ops)


def _in_odd_kernel(x_ref, g_ref, w_ref, wkt_ref, cq_ref, sq_ref, ck_ref, sk_ref, ng_ref, o_ref, kt_ref):
    h = _rms(x_ref[...], g_ref[...]).astype(BF)
    dk_all = RET_HEADS * RET_DK
    half = RET_DK // 2
    q = _dot(h, w_ref[:, 0:dk_all])
    cq, sq = cq_ref[...], sq_ref[...]
    for hd in range(RET_HEADS):
        lanes = slice(hd * RET_DK, (hd + 1) * RET_DK)
        qh = q[:, lanes]
        o_ref[:, lanes] = (qh * cq + pltpu.roll(qh, half, 1) * sq).astype(BF)
    kt = _dot_nt(wkt_ref[...], h)
    ck, sk = ck_ref[...], sk_ref[...]
    scale = RET_DK ** -0.5
    for hd in range(RET_HEADS):
        k1 = kt[hd * RET_DK:hd * RET_DK + half, :]
        k2 = kt[hd * RET_DK + half:(hd + 1) * RET_DK, :]
        kt_ref[0, hd * RET_DK:hd * RET_DK + half, :] = ((k1 * ck - k2 * sk) * scale).astype(BF)
        kt_ref[0, hd * RET_DK + half:(hd + 1) * RET_DK, :] = ((k1 * sk + k2 * ck) * scale).astype(BF)
    g0 = dk_all + RET_HEADS * RET_DV
    for c0 in range(dk_all, o_ref.shape[1], dk_all):
        z = _dot(h, w_ref[:, c0 + dk_all:c0 + 2 * dk_all])
        if c0 >= g0:
            z = z * _sigmoid(z) * ng_ref[:, c0 - g0:c0 - g0 + dk_all]
        o_ref[:, c0:c0 + dk_all] = z.astype(BF)


def _in_odd(x2, gain, w, wkt, norm_gain, batch, tm=512):
    n, dm = x2.shape
    t = n // batch
    nt = t // tm
    dk_all = RET_HEADS * RET_DK
    half = RET_DK // 2
    inv = ROPE_BASE ** (-jnp.arange(half, dtype=F32) / half)
    ang = jnp.arange(t, dtype=F32)[:, None] * inv[None, :]
    cos, sin = jnp.cos(ang), jnp.sin(ang)
    cq = jnp.concatenate([cos, cos], axis=1)
    sq = jnp.concatenate([-sin, sin], axis=1)
    const = lambda shp: pl.BlockSpec(shp, lambda i: (0,) * len(shp))
    return pl.pallas_call(
        _in_odd_kernel,
        out_shape=[jax.ShapeDtypeStruct((n, w.shape[1] - dk_all), BF), jax.ShapeDtypeStruct((batch, dk_all, t), BF)],
        grid=(n // tm,),
        in_specs=[pl.BlockSpec((tm, dm), lambda i: (i, 0)), const((1, dm)), const(w.shape), const(wkt.shape),
                  pl.BlockSpec((tm, RET_DK), lambda i: (i % nt, 0)), pl.BlockSpec((tm, RET_DK), lambda i: (i % nt, 0)),
                  pl.BlockSpec((half, tm), lambda i: (0, i % nt)), pl.BlockSpec((half, tm), lambda i: (0, i % nt)),
                  const((1, norm_gain.shape[0]))],
        out_specs=[pl.BlockSpec((tm, w.shape[1] - dk_all), lambda i: (i, 0)),
                   pl.BlockSpec((1, dk_all, tm), lambda i: (i // nt, 0, i % nt))],
        compiler_params=_params(1, VMEM_LIMIT),
        name="in_odd",
    )(x2, gain[None].astype(F32), w, wkt, cq, sq, cos.T, sin.T, norm_gain[None].astype(F32))


def _ret_kernel(lg_ref, q_ref, kt_ref, v_ref, g_ref, o_ref, sb_ref, sf_ref, sbc_ref, *, n_chunks):
    c = RET_CHUNK
    hd = pl.program_id(1)
    lgf = lg_ref[0, hd]
    lgb = lg_ref[1, hd]

    ii = lax.broadcasted_iota(jnp.int32, (c, c), 0)
    jj = lax.broadcasted_iota(jnp.int32, (c, c), 1)
    dist = (ii - jj).astype(F32)
    decay = jnp.where(ii >= jj, jnp.exp(lgf * jnp.maximum(dist, 0.0)),
                      jnp.exp(lgb * jnp.maximum(-dist, 0.0))).astype(BF)
    tok_col = lax.broadcasted_iota(jnp.int32, (c, RET_DK), 0).astype(F32)
    tok_row = lax.broadcasted_iota(jnp.int32, (RET_DK, c), 1).astype(F32)
    q_dec_f = jnp.exp(lgf * (tok_col + 1.0)).astype(BF)
    q_dec_b = jnp.exp(lgb * (c - tok_col)).astype(BF)
    k_dec_f = jnp.exp(lgf * (c - 1.0 - tok_row)).astype(BF)
    k_dec_b = jnp.exp(lgb * tok_row).astype(BF)
    chunk_f = jnp.exp(lgf * c)
    chunk_b = jnp.exp(lgb * c)

    sbc_ref[...] = jnp.zeros_like(sbc_ref)

    def back(i, carry):
        ci = n_chunks - 1 - i
        t0 = pl.multiple_of(ci * c, c)
        sb_ref[ci] = sbc_ref[...].astype(BF)
        kd = kt_ref[0, :, pl.ds(t0, c)] * k_dec_b
        sbc_ref[...] = sbc_ref[...] * chunk_b + _dot(kd, v_ref[0, pl.ds(t0, c), :])
        return carry

    lax.fori_loop(0, n_chunks, back, 0, unroll=4)

    sf_ref[...] = jnp.zeros_like(sf_ref)

    def fwd(ci, carry):
        t0 = pl.multiple_of(ci * c, c)
        qc = q_ref[0, pl.ds(t0, c), :]
        kc = kt_ref[0, :, pl.ds(t0, c)]
        vc = v_ref[0, pl.ds(t0, c), :]
        s = _dot(qc, kc).astype(BF) * decay
        qd = jnp.concatenate([qc * q_dec_f, qc * q_dec_b], axis=1)
        st = jnp.concatenate([sf_ref[...].astype(BF), sb_ref[ci]], axis=0)
        o = _dot(s, vc) + _dot(qd, st)
        sf_ref[...] = sf_ref[...] * chunk_f + _dot(kc * k_dec_f, vc)
        o = o * lax.rsqrt(jnp.mean(o * o, axis=-1, keepdims=True) + EPS)
        o_ref[0, pl.ds(t0, c), :] = o.astype(BF) * g_ref[0, pl.ds(t0, c), :]
        return carry

    lax.fori_loop(0, n_chunks, fwd, 0, unroll=4)


def _retention(qvg, kt, log_gamma, batch):
    n, wq = qvg.shape
    t = n // batch
    assert t % RET_CHUNK == 0
    n_chunks = t // RET_CHUNK
    qvg3 = qvg.reshape(batch, t, wq)
    v_off = (RET_HEADS * RET_DK) // RET_DV
    g_off = v_off + RET_HEADS
    return pl.pallas_call(
        functools.partial(_ret_kernel, n_chunks=n_chunks),
        out_shape=jax.ShapeDtypeStruct((batch, t, RET_HEADS * RET_DV), BF),
        grid_spec=pltpu.PrefetchScalarGridSpec(
            num_scalar_prefetch=1,
            grid=(batch, RET_HEADS),
            in_specs=[pl.BlockSpec((1, t, RET_DK), lambda b, h, lg: (b, 0, h)),
                      pl.BlockSpec((1, RET_DK, t), lambda b, h, lg: (b, h, 0)),
                      pl.BlockSpec((1, t, RET_DV), lambda b, h, lg: (b, 0, v_off + h)),
                      pl.BlockSpec((1, t, RET_DV), lambda b, h, lg: (b, 0, g_off + h))],
            out_specs=pl.BlockSpec((1, t, RET_DV), lambda b, h, lg: (b, 0, h)),
            scratch_shapes=[pltpu.VMEM((n_chunks, RET_DK, RET_DV), BF),
                            pltpu.VMEM((RET_DK, RET_DV), F32), pltpu.VMEM((RET_DK, RET_DV), F32)]),
        compiler_params=_params(2, VMEM_LIMIT),
        name="retention",
    )(log_gamma, qvg3, kt, qvg3, qvg3)


def _out_odd_kernel(y_ref, x_ref, w_ref, *route_refs):
    o_ref = route_refs[N_ROUTE_IN]
    x_new = x_ref[...] + _dot(y_ref[...], w_ref[...])
    o_ref[...] = x_new
    _route(x_new, *route_refs[:N_ROUTE_IN], *route_refs[N_ROUTE_IN + 1:])


def _out_odd(y, x2, w, router, tm=512):
    n, dm = x2.shape
    kdim = y.shape[1]
    r_ops, r_in, r_shapes, r_out, r_scratch = _router_io(n, dm, tm, *router)
    return pl.pallas_call(
        _out_odd_kernel,
        out_shape=[jax.ShapeDtypeStruct((n, dm), F32)] + r_shapes,
        grid=(n // tm,),
        in_specs=[pl.BlockSpec((tm, kdim), lambda i: (i, 0)), pl.BlockSpec((tm, dm), lambda i: (i, 0)),
                  pl.BlockSpec(w.shape, lambda i: (0, 0))] + r_in,
        out_specs=[pl.BlockSpec((tm, dm), lambda i: (i, 0))] + r_out,
        scratch_shapes=r_scratch,
        compiler_params=_params(1, VMEM_LIMIT),
        name="out_odd",
    )(y, x2, w, *r_ops)


def _route(x, g_ref, wc_ref, wh_ref, b_ref, tri_ref, o_ref, ot_ref, cnt_ref, run_ref):
    @pl.when(pl.program_id(0) == 0)
    def _():
        run_ref[...] = jnp.zeros_like(run_ref)

    h = _rms(x, g_ref[...])
    hh, hl = _split_bf16(h)
    both = _dot(hh, wc_ref[...])
    nl = wh_ref.shape[1]
    logits = both[:, 0:nl] + both[:, nl:] + _dot(hl, wh_ref[...]) + b_ref[...]
    sub = tri_ref.shape[0]
    tri = tri_ref[...]
    run = run_ref[...]
    for r0 in range(0, x.shape[0], sub):
        info, run = _route_rows(logits[r0:r0 + sub], tri, run)
        o_ref[r0:r0 + sub, :] = info
        ot_ref[:, r0:r0 + sub] = info.T[0:ROW_ALIGN, :]
    run_ref[...] = run
    cnt_ref[...] = run


def _route_rows(logits, tri, run):
    lane = lax.broadcasted_iota(jnp.int32, logits.shape, 1).astype(F32)
    neg = jnp.float32(-jnp.inf)

    def top(mask):
        val = jnp.max(jnp.where(mask, logits, neg), axis=-1, keepdims=True)
        idx = jnp.min(jnp.where(mask & (logits == val), lane, 1e6), axis=-1, keepdims=True)
        return val, idx

    gmask = lane < MOE_GROUPS
    gmax, gidx = top(gmask)
    gprob = 1.0 / jnp.sum(jnp.where(gmask, jnp.exp(logits - gmax), 0.0), axis=-1, keepdims=True)
    base = MOE_GROUPS + MOE_PER_GROUP * gidx
    emask = (lane >= base) & (lane < base + MOE_PER_GROUP)
    v1, i1 = top(emask)
    v2, i2 = top(emask & (lane != i1))
    e2 = jnp.exp(v2 - v1)
    w1 = gprob / (1.0 + e2)
    w2 = gprob * e2 / (1.0 + e2)
    a1 = i1 - base
    a2 = i2 - base
    lo = jnp.minimum(a1, a2)
    hi = jnp.maximum(a1, a2)
    pair = lo * (7.0 - lo) * 0.5 + hi - lo - 1.0
    cls = gidx * MOE_PAIRS + pair
    w_lo = jnp.where(a1 < a2, w1, w2)
    w_hi = jnp.where(a1 < a2, w2, w1)
    onehot = lane == cls
    earlier = _dot(tri, jnp.where(onehot, 1.0, 0.0).astype(BF)) + run
    rank = jnp.sum(jnp.where(onehot, earlier, 0.0), axis=-1, keepdims=True)
    info = jnp.where(lane == 0, cls, jnp.where(lane == 1, w_lo, jnp.where(lane == 2, w_hi,
                     jnp.where(lane == 3, rank, 0.0))))
    return info, run + jnp.sum(jnp.where(onehot, 1.0, 0.0), axis=0, keepdims=True)


ROUTE_ROWS = 128
N_ROUTE_IN = 5


def _router_io(n, dm, tm, gain, w_cat, w_hi, b_r):
    tri = (jnp.arange(ROUTE_ROWS)[:, None] > jnp.arange(ROUTE_ROWS)[None, :]).astype(BF)
    const = lambda shp: pl.BlockSpec(shp, lambda i: (0, 0))
    operands = [gain[None].astype(F32), w_cat, w_hi, b_r, tri]
    in_specs = [const((1, dm)), const(w_cat.shape), const(w_hi.shape), const((1, 128)), const(tri.shape)]
    out_shapes = [jax.ShapeDtypeStruct((n, 128), F32), jax.ShapeDtypeStruct((ROW_ALIGN, n), F32),
                  jax.ShapeDtypeStruct((1, 128), F32)]
    out_specs = [pl.BlockSpec((tm, 128), lambda i: (i, 0)), pl.BlockSpec((ROW_ALIGN, tm), lambda i: (0, i)),
                 const((1, 128))]
    return operands, in_specs, out_shapes, out_specs, [pltpu.VMEM((1, 128), F32)]


def _start_row_dmas(n, copy_of_row):
    for r in range(n):
        copy_of_row(r).start()


def _dispatch_kernel(fpos_ref, fon_ref, pos_ref, x_ref, info_ref, o_hbm, aug_ref, zero_ref, sem, row_sem):
    dm = x_ref.shape[-1]
    tm = info_ref.shape[0]

    @pl.when(pl.program_id(0) == 0)
    def _():
        zero_ref[...] = jnp.zeros_like(zero_ref)
        n_fill = fpos_ref.shape[0]

        def fill(k):
            return pltpu.make_async_copy(
                zero_ref, o_hbm.at[pl.ds(pl.multiple_of(fpos_ref[k], ROW_ALIGN), MOE_FILL), :], sem)

        for k in range(n_fill - 1):
            pl.when(fon_ref[k] > 0)(lambda k=k: fill(k).start())
        for k in range(n_fill - 1):
            pl.when(fon_ref[k] > 0)(lambda k=k: fill(k).wait())
        fill(n_fill - 1).start()
        fill(n_fill - 1).wait()

    i = pl.program_id(0)
    n_steps = pl.num_programs(0)
    slot = i % 2

    def wait_rows(s):
        pltpu.make_async_copy(aug_ref.at[s], o_hbm.at[pl.ds(0, tm), :], row_sem.at[s]).wait()

    pl.when(i >= 2)(lambda: wait_rows(slot))
    aug_ref[slot, :, 0:dm] = x_ref[...].reshape(tm, dm)
    aug_ref[slot, :, dm:] = info_ref[...]
    _start_row_dmas(tm, lambda r: pltpu.make_async_copy(
        aug_ref.at[slot, pl.ds(r, 1), :], o_hbm.at[pl.ds(pos_ref[0, 0, r], 1), :], row_sem.at[slot]))

    @pl.when(i == n_steps - 1)
    def _():
        pl.when(n_steps >= 2)(lambda: wait_rows(1 - slot))
        wait_rows(slot)


def _token_tiles(x3, nb, tq):
    if nb == x3.shape[0]:
        return (lambda *shape_tail: pl.BlockSpec((nb, tq) + shape_tail, lambda i, *_: (0, i) + (0,) * len(shape_tail)),
                x3.shape[1] // tq)
    assert nb == 1 and tq == x3.shape[1]
    return (lambda *shape_tail: pl.BlockSpec((1, tq) + shape_tail, lambda i, *_: (i, 0) + (0,) * len(shape_tail)),
            x3.shape[0])


def _dispatch(x3, nb, tq, info, pos, first_pad, n_rows):
    dm = x3.shape[-1]
    n = info.shape[0]
    tm = nb * tq
    wa = dm + info.shape[1]
    spec, n_steps = _token_tiles(x3, nb, tq)
    class_fill = first_pad // ROW_ALIGN * ROW_ALIGN
    tail = class_fill[-1] + MOE_FILL + jnp.arange((n_rows - n) // MOE_FILL + 1, dtype=jnp.int32) * MOE_FILL
    tail_on = tail + MOE_FILL <= n_rows
    fpos = jnp.concatenate([class_fill, jnp.where(tail_on, tail, 0), jnp.array([n_rows - MOE_FILL], jnp.int32)])
    fon = jnp.concatenate([jnp.ones_like(class_fill), tail_on.astype(jnp.int32), jnp.ones((1,), jnp.int32)])
    return pl.pallas_call(
        _dispatch_kernel,
        out_shape=jax.ShapeDtypeStruct((n_rows, wa), F32),
        grid_spec=pltpu.PrefetchScalarGridSpec(
            num_scalar_prefetch=2,
            grid=(n_steps,),
            in_specs=[pl.BlockSpec((1, 1, tm), lambda i, fp, fo: (i, 0, 0), memory_space=pltpu.SMEM),
                      spec(dm),
                      pl.BlockSpec((tm, info.shape[1]), lambda i, fp, fo: (i, 0))],
            out_specs=pl.BlockSpec(memory_space=pl.ANY),
            scratch_shapes=[pltpu.VMEM((2, tm, wa), F32), pltpu.VMEM((MOE_FILL, wa), F32),
                            pltpu.SemaphoreType.DMA, pltpu.SemaphoreType.DMA((2,))]),
        compiler_params=_params(1, VMEM_LIMIT),
        name="moe_dispatch",
    )(fpos.astype(jnp.int32), fon, pos.reshape(n_steps, 1, tm), x3, info)


EXPERT_TILES = 2


def _expert_kernel(ea_ref, eb_ref, act_ref, xa_ref, g_ref, *refs):
    i = pl.program_id(0)
    o_ref = refs[-1]
    dm = o_ref.shape[1]
    tm = MOE_TILE

    @pl.when(act_ref[i * EXPERT_TILES] > 0)
    def _():
        for t in range(EXPERT_TILES):
            wga, wua, wda, wgb, wub, wdb = refs[6 * t:6 * t + 6]
            rows = slice(t * tm, (t + 1) * tm)
            h = _rms(xa_ref[rows, 0:dm], g_ref[...]).astype(BF)

            def ffn(wg, wu, wd, wrow):
                gate = _dot(h, wg[0])
                hid = gate * _sigmoid(gate) * _dot(h, wu[0])
                return _dot((hid * wrow).astype(BF), wd[0])

            o_ref[rows, :] = (ffn(wga, wua, wda, xa_ref[rows, dm + 1:dm + 2])
                              + ffn(wgb, wub, wdb, xa_ref[rows, dm + 2:dm + 3]))

    @pl.when(act_ref[i * EXPERT_TILES] == 0)
    def _():
        o_ref[...] = jnp.zeros_like(o_ref)


def _experts(xa_sorted, gain, tile_ea, tile_eb, tile_act, w_gate, w_up, w_down):
    dm = w_gate.shape[1]
    wa = xa_sorted.shape[1]
    n_tiles = tile_ea.shape[0]
    assert n_tiles % EXPERT_TILES == 0
    tm = MOE_TILE * EXPERT_TILES
    ff = w_gate.shape[-1]

    def wsel(shp, which, t):
        return pl.BlockSpec((1,) + shp, lambda i, ea, eb, act: ((ea, eb)[which][i * EXPERT_TILES + t], 0, 0))

    w_specs, w_ops = [], []
    for t in range(EXPERT_TILES):
        for which in (0, 1):
            w_specs += [wsel((dm, ff), which, t), wsel((dm, ff), which, t), wsel((ff, dm), which, t)]
            w_ops += [w_gate, w_up, w_down]
    return pl.pallas_call(
        _expert_kernel,
        out_shape=jax.ShapeDtypeStruct((n_tiles * MOE_TILE, dm), F32),
        grid_spec=pltpu.PrefetchScalarGridSpec(
            num_scalar_prefetch=3,
            grid=(n_tiles // EXPERT_TILES,),
            in_specs=[pl.BlockSpec((tm, wa), lambda i, ea, eb, act: (i * act[i * EXPERT_TILES], 0)),
                      pl.BlockSpec((1, dm), lambda i, ea, eb, act: (0, 0))] + w_specs,
            out_specs=pl.BlockSpec((tm, dm), lambda i, ea, eb, act: (i, 0))),
        compiler_params=_params(1, VMEM_LIMIT),
        name="moe_experts",
    )(tile_ea, tile_eb, tile_act, xa_sorted, gain, *w_ops)


def _combine_kernel(pos_ref, nxt_ref, x_ref, y_hbm, o_ref, ybuf, sem):
    i = pl.program_id(0)
    n_steps = pl.num_programs(0)
    tm = ybuf.shape[1]
    slot = i % 2

    def gather(idx_ref, s):
        _start_row_dmas(tm, lambda r: pltpu.make_async_copy(
            y_hbm.at[pl.ds(idx_ref[0, 0, r], 1), :], ybuf.at[s, pl.ds(r, 1), :], sem.at[s]))

    @pl.when(i == 0)
    def _():
        gather(pos_ref, 0)

    @pl.when(i + 1 < n_steps)
    def _():
        gather(nxt_ref, 1 - slot)

    pltpu.make_async_copy(y_hbm.at[pl.ds(0, tm), :], ybuf.at[slot], sem.at[slot]).wait()
    o_ref[...] = x_ref[...] + ybuf[slot].reshape(x_ref.shape)


def _combine(x3, nb, tq, y_sorted, pos):
    dm = x3.shape[-1]
    tm = nb * tq
    spec, n_steps = _token_tiles(x3, nb, tq)
    pos3 = pos.reshape(n_steps, 1, tm)
    return pl.pallas_call(
        _combine_kernel,
        out_shape=jax.ShapeDtypeStruct(x3.shape, F32),
        grid=(n_steps,),
        in_specs=[pl.BlockSpec((1, 1, tm), lambda i: (i, 0, 0), memory_space=pltpu.SMEM),
                  pl.BlockSpec((1, 1, tm), lambda i: (jnp.minimum(i + 1, n_steps - 1), 0, 0), memory_space=pltpu.SMEM),
                  spec(dm),
                  pl.BlockSpec(memory_space=pl.ANY)],
        out_specs=spec(dm),
        scratch_shapes=[pltpu.VMEM((2, tm, dm), F32), pltpu.SemaphoreType.DMA((2,))],
        compiler_params=_params(1, VMEM_LIMIT),
        name="moe_combine",
    )(pos3, pos3, x3, y_sorted)


def _moe(x3, nb, tq, info, info_rows, cnt, gain, layer, w_gate, w_up, w_down):
    n = info.shape[0]
    tm = MOE_TILE
    gain = gain[None].astype(F32)
    cls = info_rows[0].astype(jnp.int32)
    counts = cnt[0, :MOE_CLASSES].astype(jnp.int32)
    padded = ((counts + tm - 1) // tm) * tm
    ends = jnp.cumsum(padded)
    starts = ends - padded
    pos = starts[cls] + info_rows[3].astype(jnp.int32)
    n_tiles = -(-(n // tm + MOE_CLASSES) // EXPERT_TILES) * EXPERT_TILES
    tile_start = jnp.arange(n_tiles, dtype=jnp.int32) * tm
    tile_cls = jnp.minimum(jnp.sum(tile_start[:, None] >= ends[None, :], axis=1), MOE_CLASSES - 1)
    tile_act = (tile_start < ends[-1]).astype(jnp.int32)
    pair_lo = jnp.array([0, 0, 0, 1, 1, 2], jnp.int32)
    pair_hi = jnp.array([1, 2, 3, 2, 3, 3], jnp.int32)
    first = layer * MOE_GROUPS * MOE_PER_GROUP + (tile_cls // MOE_PAIRS) * MOE_PER_GROUP
    tile_ea = (first + pair_lo[tile_cls % MOE_PAIRS]).astype(jnp.int32)
    tile_eb = (first + pair_hi[tile_cls % MOE_PAIRS]).astype(jnp.int32)
    xa_sorted = _dispatch(x3, nb, tq, info, pos, (starts + counts).astype(jnp.int32), (n_tiles + 2) * tm)
    y_sorted = _experts(xa_sorted, gain, tile_ea, tile_eb, tile_act, w_gate, w_up, w_down)
    return _combine(x3, nb, tq, y_sorted, pos)


def _router_weights(w_group, b_group, w_router, b_router):
    dm = w_group.shape[0]
    w = jnp.concatenate([w_group.astype(F32), w_router.astype(F32).reshape(dm, -1)], axis=1)
    b = jnp.concatenate([b_group.astype(F32), b_router.astype(F32).reshape(-1)])
    pad = 128 - w.shape[1]
    w_hi, w_lo = _split_bf16(jnp.pad(w, ((0, 0), (0, pad))))
    return jnp.concatenate([w_hi, w_lo], axis=1), w_hi, jnp.pad(b, (0, pad))[None]


def _trunk(x, p):
    b, t, dm = x.shape
    n = b * t

    tq = _even_tile(b)
    ug, q, k, v = _in_even(x, p["norm_mix"][0], p["w_in_even"], p["na_q_gain"], p["na_k_gain"])
    yg = _s5_scan(ug, *p["s5_ops"], p["s5_d_rows"], t // S5_CHUNK, b)
    att = _na(q, k, v, p["na_table"])
    x3, *routing = _out_even(yg, att, x, p["s5_w_glu"], p["s5_b_glu"], p["w_out_even"],
                             (p["norm_ffn"][0], *p["router"][0]))
    x3 = _moe(x3, b, tq, *routing, p["norm_ffn"][0], 0, *p["experts"])

    x2 = x3.reshape(n, dm)
    qvg, kt = _in_odd(x2, p["norm_mix"][1], p["w_in_odd"], p["w_k_t"], p["ret_norm_gain"], b)
    yr = _retention(qvg, kt, p["log_gamma"], b)
    tm = 512
    x2, *routing = _out_odd(yr.reshape(n, -1), x2, p["w_out_odd"], (p["norm_ffn"][1], *p["router"][1]), tm)
    x3 = _moe(x2.reshape(n // tm, tm, dm), 1, tm, *routing, p["norm_ffn"][1], 1, *p["experts"])
    return x3.reshape(b, t, dm)


def kernel(x_prompt, x_sample, norm_mix, norm_ffn, w_in_even, w_out_even, s5_lambda_re, s5_lambda_im, s5_log_dt, s5_b_re, s5_b_im, s5_c_re, s5_c_im, s5_d, s5_w_glu, s5_b_glu, na_q_gain, na_k_gain, na_rpb, w_in_odd, w_out_odd, ret_decay_logit, ret_norm_gain, moe_w_group, moe_b_group, moe_w_router, moe_b_router, moe_w_gate, moe_w_up, moe_w_down):
    assert norm_mix.shape[0] == 2
    dk_all = RET_HEADS * RET_DK
    w_odd = w_in_odd[0].astype(BF)
    p = {
        "norm_mix": norm_mix, "norm_ffn": norm_ffn,
        "w_in_even": w_in_even[0].astype(BF), "w_out_even": w_out_even[0].astype(BF),
        "s5_ops": _s5_operators(s5_lambda_re[0], s5_lambda_im[0], s5_log_dt[0], s5_b_re[0], s5_b_im[0],
                                s5_c_re[0], s5_c_im[0]),
        "s5_d_rows": jnp.tile(s5_d[0].astype(F32).reshape(S5_GROUPS, 1, S5_GROUP), (1, 1, S5_CHUNK)),
        "s5_w_glu": s5_w_glu[0].astype(BF), "s5_b_glu": s5_b_glu[0],
        "na_q_gain": na_q_gain[0], "na_k_gain": na_k_gain[0], "na_table": _na_bias_table(na_rpb[0]),
        "w_in_odd": w_odd, "w_k_t": w_in_odd[0][:, dk_all:2 * dk_all].T.astype(BF),
        "w_out_odd": w_out_odd[0].astype(BF),
        "log_gamma": jax.nn.log_sigmoid(ret_decay_logit[0].astype(F32)),
        "ret_norm_gain": ret_norm_gain[0],
        "router": [_router_weights(moe_w_group[l], moe_b_group[l], moe_w_router[l], moe_b_router[l]) for l in range(2)],
        "experts": tuple(w.astype(BF).reshape((-1,) + w.shape[2:]) for w in (moe_w_gate, moe_w_up, moe_w_down)),
    }
    return _trunk(x_prompt, p), _trunk(x_sample, p)
```

```python
import functools
import math

import jax
import jax.numpy as jnp
from jax import lax
from jax.experimental import pallas as pl
from jax.experimental.pallas import tpu as pltpu

F32 = jnp.float32
BF = jnp.bfloat16
EPS = 1e-6

D_MODEL = 1024
GRID_W = 64
S5_GROUPS = 32
S5_GROUP = 16
S5_STATE = 64
S5_CHUNK = 16
NA_HEADS = 8
NA_HEAD_DIM = 64
NA_WIN_R = 8
NA_WIN_C = 16
NA_HEAD_GROUP = 4
RET_HEADS = 8
RET_DK = 128
RET_DV = 256
RET_CHUNK = 256
ROPE_BASE = 10000.0
MOE_GROUPS = 4
MOE_PER_GROUP = 4
MOE_FF = 512
MOE_PAIRS = 6
MOE_CLASSES = MOE_GROUPS * MOE_PAIRS
MOE_TILE = 256
ROW_ALIGN = 8
MOE_FILL = MOE_TILE + ROW_ALIGN
MASK_NEG = -1e30
VMEM_LIMIT = 56 * 1024 * 1024


def _params(n_axes, vmem=None):
    return pltpu.CompilerParams(dimension_semantics=("arbitrary",) * n_axes,
                                vmem_limit_bytes=vmem)


def _rms(x, gain):
    ms = jnp.mean(x * x, axis=-1, keepdims=True)
    return x * lax.rsqrt(ms + EPS) * gain


def _sigmoid(x):
    return 1.0 / (1.0 + jnp.exp(-x))


def _dot(a, b):
    return jnp.dot(a, b, preferred_element_type=F32)


def _dot_nt(a, b):
    return lax.dot_general(a, b, (((1,), (1,)), ((), ())), preferred_element_type=F32)


def _split_bf16(x):
    hi = x.astype(BF)
    lo = (x - hi.astype(F32)).astype(BF)
    return hi, lo


def _dot3(a, b):
    ah, al = _split_bf16(a)
    bh, bl = _split_bf16(b)
    return _dot(ah, bh) + _dot(ah, bl) + _dot(al, bh)


def _chunk_major_perm(batch, tq):
    nc = tq // S5_CHUNK
    r = jnp.arange(batch * tq)
    j, c, b = r // (nc * batch), (r // batch) % nc, r % batch
    src = b * tq + c * S5_CHUNK + j
    return (src[:, None] == jnp.arange(batch * tq)[None, :]).astype(BF)


def _in_even_kernel(x_ref, g_ref, w_ref, qg_ref, kg_ref, seg_ref, perm_ref, ug_ref, q_ref, k_ref, v_ref):
    batch, tq, dm = x_ref.shape
    h = _rms(x_ref[...].reshape(batch * tq, dm), g_ref[...]).astype(BF)
    d = q_ref.shape[-1]
    nr = ug_ref.shape[1]
    up = _dot(perm_ref[...], _dot(h, w_ref[:, 0:d]).astype(BF))
    for g in range(S5_GROUPS):
        lanes = slice(g * S5_GROUP, (g + 1) * S5_GROUP)
        ug_ref[g] = jnp.concatenate([up[j * nr:(j + 1) * nr, lanes] for j in range(S5_CHUNK)], axis=1).astype(BF)
    seg = seg_ref[...]

    def head_norm(z, gain):
        ssq = _dot((z * z).astype(BF), seg) * (1.0 / NA_HEAD_DIM)
        return z * lax.rsqrt(ssq + EPS) * gain

    q = _dot(h, w_ref[:, d:2 * d])
    q_ref[...] = (head_norm(q, qg_ref[...]) * (NA_HEAD_DIM ** -0.5)).astype(BF).reshape(batch, tq, d)
    k = _dot(h, w_ref[:, 2 * d:3 * d])
    k_ref[...] = head_norm(k, kg_ref[...]).astype(BF).reshape(batch, tq, d)
    v_ref[...] = _dot(h, w_ref[:, 3 * d:4 * d]).astype(BF).reshape(batch, tq, d)


def _even_tile(batch, tm=512):
    tq = tm // batch
    assert tq * batch == tm and tq % S5_CHUNK == 0
    return tq


def _in_even(x3, gain, w, q_gain, k_gain):
    batch, t, dm = x3.shape
    d = w.shape[1] // 4
    tq = _even_tile(batch)
    nr = (tq // S5_CHUNK) * batch
    seg = (jnp.arange(d)[:, None] // NA_HEAD_DIM == jnp.arange(d)[None, :] // NA_HEAD_DIM).astype(BF)
    qg = jnp.tile(q_gain.astype(F32), NA_HEADS)[None]
    kg = jnp.tile(k_gain.astype(F32), NA_HEADS)[None]
    tok = lambda width: pl.BlockSpec((batch, tq, width), lambda i: (0, i, 0))
    full = lambda shp: pl.BlockSpec(shp, lambda i: (0,) * len(shp))
    gw = S5_CHUNK * S5_GROUP
    perm = _chunk_major_perm(batch, tq)
    return pl.pallas_call(
        _in_even_kernel,
        out_shape=[jax.ShapeDtypeStruct((S5_GROUPS, (t // S5_CHUNK) * batch, gw), BF)]
        + [jax.ShapeDtypeStruct((batch, t, d), BF)] * 3,
        grid=(t // tq,),
        in_specs=[tok(dm), full((1, dm)), full(w.shape), full((1, d)), full((1, d)), full((d, d)), full(perm.shape)],
        out_specs=[pl.BlockSpec((S5_GROUPS, nr, gw), lambda i: (0, i, 0)), tok(d), tok(d), tok(d)],
        compiler_params=_params(1, VMEM_LIMIT),
        name="in_even",
    )(x3, gain[None].astype(F32), w, qg, kg, seg, perm)


def _s5_taps_kernel(l_ref, r_ref, o_ref):
    for g in range(l_ref.shape[0]):
        o_ref[g] = _dot3(l_ref[g], r_ref[g])


def _s5_taps(lhs, rhs, per_step=8):
    g2, m, k = lhs.shape
    nn = rhs.shape[-1]
    return pl.pallas_call(
        _s5_taps_kernel,
        out_shape=jax.ShapeDtypeStruct((g2, m, nn), F32),
        grid=(g2 // per_step,),
        in_specs=[pl.BlockSpec((per_step, m, k), lambda i: (i, 0, 0)), pl.BlockSpec((per_step, k, nn), lambda i: (i, 0, 0))],
        out_specs=pl.BlockSpec((per_step, m, nn), lambda i: (i, 0, 0)),
        compiler_params=_params(1),
        name="s5_taps",
    )(lhs, rhs)


def _s5_operators(lam_re, lam_im, log_dt, b_re, b_im, c_re, c_im):
    L, G, H, P_ = S5_CHUNK, S5_GROUPS, S5_GROUP, S5_STATE
    lr, li = lam_re.astype(F32), lam_im.astype(F32)
    dt = jnp.exp(log_dt.astype(F32))[..., None]
    kk = jnp.arange(L + 1, dtype=F32)[:, None, None, None]
    mag = jnp.exp(kk * (lr * dt)[None])
    ang = kk * (li * dt)[None]
    pw = (mag * jnp.cos(ang), mag * jnp.sin(ang))

    def cmul(x, y):
        return x[0] * y[0] - x[1] * y[1], x[0] * y[1] + x[1] * y[0]

    xr, xi = pw[0][1] - 1.0, pw[1][1]
    den = lr * lr + li * li
    coef = ((xr * lr + xi * li) / den, (xi * lr - xr * li) / den)
    bbar = cmul((coef[0][..., None], coef[1][..., None]), (b_re.astype(F32), b_im.astype(F32)))
    c = (c_re.astype(F32), c_im.astype(F32))

    def at(k_slice, direction, perm):
        return tuple(jnp.transpose(z[k_slice, direction], perm) for z in pw)

    apl = tuple(jnp.transpose(z[:L], (1, 2, 0, 3))[:, :, :, None, :] for z in pw)
    ce = cmul((c[0][:, :, None], c[1][:, :, None]), apl)
    lhs = jnp.concatenate([ce[0], -ce[1]], axis=-1).reshape(2 * G, L * H, 2 * P_)
    rhs = jnp.concatenate([bbar[0], bbar[1]], axis=2).reshape(2 * G, 2 * P_, H)
    taps = _s5_taps(lhs, rhs).reshape(2, G, L, H, H)
    taps = jnp.transpose(taps, (0, 1, 2, 4, 3))
    kf, kb = taps[0], taps[1]
    lagged = jnp.concatenate([jnp.flip(kb[:, 1:], axis=1), (kf[:, 0] + kb[:, 0])[:, None], kf[:, 1:]], axis=1)
    idx = jnp.arange(L)[None, :] - jnp.arange(L)[:, None] + (L - 1)
    m = jnp.transpose(lagged[:, idx], (0, 1, 3, 2, 4)).reshape(G, L * H, L * H)

    bt = tuple(jnp.transpose(z, (0, 1, 3, 2)) for z in bbar)
    expand_j = lambda z: z[:, :, None, :]
    pf = cmul(tuple(map(expand_j, at(slice(L - 1, None, -1), 0, (1, 0, 2)))), (bt[0][0][:, None], bt[1][0][:, None]))
    pb = cmul(tuple(map(expand_j, at(slice(0, L), 1, (1, 0, 2)))), (bt[0][1][:, None], bt[1][1][:, None]))
    p = jnp.concatenate([pf[0], pb[0], pf[1], pb[1]], axis=-1).reshape(G, L * H, 4 * P_)

    ct = tuple(jnp.transpose(z, (0, 1, 3, 2)) for z in c)
    expand_h = lambda z: z[..., None]
    gf = cmul(tuple(map(expand_h, at(slice(1, None), 0, (1, 2, 0)))), (ct[0][0][:, :, None, :], ct[1][0][:, :, None, :]))
    gb = cmul(tuple(map(expand_h, at(slice(L, 0, -1), 1, (1, 2, 0)))), (ct[0][1][:, :, None, :], ct[1][1][:, :, None, :]))
    q = jnp.concatenate([gf[0], gb[0], -gf[1], -gb[1]], axis=1).reshape(G, 4 * P_, L * H)

    a_mat = jnp.stack([jnp.concatenate([pw[0][L, 0], pw[0][L, 1]], -1),
                       jnp.concatenate([pw[1][L, 0], pw[1][L, 1]], -1)], axis=1)
    return m.astype(BF), p.astype(BF), q.astype(BF), a_mat.astype(F32)


def _s5_kernel(u_ref, m_ref, p_ref, q_ref, a_ref, d_ref, y_ref, z_ref, s_ref, *, n_chunks, batch):
    u = u_ref[0]
    z_ref[...] = _dot(u, p_ref[0])
    half = 2 * S5_STATE
    ar = a_ref[0, 0:1, :]
    ai = a_ref[0, 1:2, :]
    fwd_lane = lax.broadcasted_iota(jnp.int32, (batch, half), 1) < S5_STATE

    def step(k, carry):
        sr, si = carry
        f0 = pl.multiple_of(k * batch, batch)
        b0 = pl.multiple_of((n_chunks - 1 - k) * batch, batch)
        s_ref[pl.ds(f0, batch), 0:S5_STATE] = sr[:, 0:S5_STATE]
        s_ref[pl.ds(b0, batch), S5_STATE:half] = sr[:, S5_STATE:half]
        s_ref[pl.ds(f0, batch), half:half + S5_STATE] = si[:, 0:S5_STATE]
        s_ref[pl.ds(b0, batch), half + S5_STATE:2 * half] = si[:, S5_STATE:half]
        zr = jnp.where(fwd_lane, z_ref[pl.ds(f0, batch), 0:half], z_ref[pl.ds(b0, batch), 0:half])
        zi = jnp.where(fwd_lane, z_ref[pl.ds(f0, batch), half:2 * half], z_ref[pl.ds(b0, batch), half:2 * half])
        return ar * sr - ai * si + zr, ar * si + ai * sr + zi

    zero = jnp.zeros((batch, half), F32)
    lax.fori_loop(0, n_chunks, step, (zero, zero), unroll=8)
    y = _dot(u, m_ref[0]) + _dot(s_ref[...].astype(BF), q_ref[0]) + u.astype(F32) * d_ref[0]
    y_ref[0] = y.astype(BF)


def _s5_scan(ug, m, p, q, a_mat, d_skip, n_chunks, batch):
    g, r, w = ug.shape
    blk = lambda shp: pl.BlockSpec((1,) + shp, lambda i: (i, 0, 0))
    return pl.pallas_call(
        functools.partial(_s5_kernel, n_chunks=n_chunks, batch=batch),
        out_shape=jax.ShapeDtypeStruct((g, r, w), BF),
        grid=(g,),
        in_specs=[blk((r, w)), blk((w, w)), blk((w, w)), blk((w, w)), blk((2, w // 2)), blk((1, w))],
        out_specs=blk((r, w)),
        scratch_shapes=[pltpu.VMEM((r, w), F32), pltpu.VMEM((r, w), F32)],
        compiler_params=_params(1, VMEM_LIMIT),
        name="s5_scan",
    )(ug, m, p, q, a_mat, d_skip)


def _na_bias_table(rpb):
    qc = jnp.arange(GRID_W)[:, None]
    kc = jnp.arange(GRID_W)[None, :]
    ws = jnp.clip(qc - NA_WIN_C // 2, 0, GRID_W - NA_WIN_C)
    valid = (kc >= ws) & (kc < ws + NA_WIN_C)
    n_tap = 2 * NA_WIN_C - 1
    pad = GRID_W - NA_WIN_C
    row = n_tap + 2 * pad
    taps = jnp.pad(rpb.astype(F32), ((0, 0), (0, 0), (pad, pad)))
    skew = jnp.tile(taps, (1, 1, GRID_W))[:, :, GRID_W - 1:GRID_W - 1 + GRID_W * (row - 1)]
    bias = skew.reshape(rpb.shape[0], rpb.shape[1], GRID_W, row - 1)[..., :GRID_W]
    e = jnp.where(valid[None, None], bias, MASK_NEG)
    return jnp.concatenate([e[:, :-1], e[:, 1:]], axis=-1)


def _na_kernel(q_ref, k_ref, v_ref, e_ref, o_ref, *, rows):
    hg_w = NA_HEAD_GROUP * NA_HEAD_DIM
    lane_head = lax.broadcasted_iota(jnp.int32, (GRID_W, hg_w), 1) // NA_HEAD_DIM
    n_keys = NA_WIN_R * GRID_W

    def row_body(r, carry):
        rs = jnp.clip(r - NA_WIN_R // 2, 0, rows - NA_WIN_R)
        di = rs - r + (NA_WIN_R - 1)
        q0 = pl.multiple_of(r * GRID_W, GRID_W)
        k0 = pl.multiple_of(rs * GRID_W, GRID_W)
        for hg in range(NA_HEADS // NA_HEAD_GROUP):
            lanes = slice(hg * hg_w, (hg + 1) * hg_w)
            q4 = q_ref[0, pl.ds(q0, GRID_W), lanes]
            qs = jnp.concatenate([jnp.where(lane_head == hh, q4, jnp.zeros_like(q4))
                                  for hh in range(NA_HEAD_GROUP)], axis=0)
            k4 = k_ref[0, pl.ds(k0, n_keys), lanes]
            v4 = v_ref[0, pl.ds(k0, n_keys), lanes]
            s = _dot_nt(qs, k4)
            bias = jnp.concatenate(
                [jnp.concatenate([e_ref[hg * NA_HEAD_GROUP + hh, di + 2 * w2] for w2 in range(NA_WIN_R // 2)], axis=1)
                 for hh in range(NA_HEAD_GROUP)], axis=0)
            s = s + bias
            mx = jnp.max(s, axis=-1, keepdims=True)
            p = jnp.exp(s - mx)
            den = jnp.sum(p, axis=-1, keepdims=True)
            o = _dot(p.astype(BF), v4) * (1.0 / den)
            out = jnp.zeros((GRID_W, hg_w), F32)
            for hh in range(NA_HEAD_GROUP):
                out = jnp.where(lane_head == hh, o[hh * GRID_W:(hh + 1) * GRID_W], out)
            o_ref[0, pl.ds(q0, GRID_W), lanes] = out.astype(BF)
        return carry

    lax.fori_loop(0, rows, row_body, 0, unroll=8)


def _na(q, k, v, table):
    b, t, d = q.shape
    rows = t // GRID_W
    assert rows >= NA_WIN_R and t % GRID_W == 0
    seq = pl.BlockSpec((1, t, d), lambda i: (i, 0, 0))
    return pl.pallas_call(
        functools.partial(_na_kernel, rows=rows),
        out_shape=jax.ShapeDtypeStruct((b, t, d), BF),
        grid=(b,),
        in_specs=[seq, seq, seq, pl.BlockSpec(table.shape, lambda i: (0, 0, 0, 0))],
        out_specs=seq,
        compiler_params=_params(1, VMEM_LIMIT),
        name="na",
    )(q, k, v, table)


def _out_even_kernel(yg_ref, a_ref, x_ref, wg_ref, bg_ref, wo_ref, perm_ref, *route_refs):
    o_ref = route_refs[N_ROUTE_IN]
    batch, tq, dm = x_ref.shape
    rows = [jnp.concatenate([yg_ref[g, :, i * S5_GROUP:(i + 1) * S5_GROUP] for g in range(S5_GROUPS)], axis=1)
            for i in range(S5_CHUNK)]
    ys = _dot(perm_ref[...], jnp.concatenate(rows, axis=0))
    c0 = math.sqrt(2.0 / math.pi)
    ys = 0.5 * ys * (1.0 + jnp.tanh(c0 * (ys + 0.044715 * (ys * ys * ys))))
    gate = _sigmoid(_dot(ys.astype(BF), wg_ref[...]) + bg_ref[...])
    a_out = (ys * gate).astype(BF)
    da = a_out.shape[-1]
    att = a_ref[...].reshape(batch * tq, da)
    x_new = x_ref[...].reshape(batch * tq, dm) + _dot(a_out, wo_ref[0:da, :]) + _dot(att, wo_ref[da:, :])
    o_ref[...] = x_new.reshape(batch, tq, dm)
    _route(x_new, *route_refs[:N_ROUTE_IN], *route_refs[N_ROUTE_IN + 1:])


def _out_even(yg, att, x3, w_glu, b_glu, w_out, router):
    batch, t, dm = x3.shape
    d = att.shape[-1]
    groups, _, gw = yg.shape
    tq = _even_tile(batch)
    tm = batch * tq
    nr = (tq // S5_CHUNK) * batch
    tok = lambda width: pl.BlockSpec((batch, tq, width), lambda i: (0, i, 0))
    full = lambda shp: pl.BlockSpec(shp, lambda i: (0,) * len(shp))
    r_ops, r_in, r_shapes, r_out, r_scratch = _router_io(batch * t, dm, tm, *router)
    perm = _chunk_major_perm(batch, tq).T
    return pl.pallas_call(
        _out_even_kernel,
        out_shape=[jax.ShapeDtypeStruct((batch, t, dm), F32)] + r_shapes,
        grid=(t // tq,),
        in_specs=[pl.BlockSpec((groups, nr, gw), lambda i: (0, i, 0)), tok(d), tok(dm),
                  full((d, d)), full((1, d)), full(w_out.shape), full(perm.shape)] + r_in,
        out_specs=[tok(dm)] + r_out,
        scratch_shapes=r_scratch,
        compiler_params=_params(1, VMEM_LIMIT),
        name="out_even",
    )(yg, att, x3, w_glu, b_glu[None].astype(F32), w_out, perm, *r_ops)


def _in_odd_kernel(x_ref, g_ref, w_ref, wkt_ref, cq_ref, sq_ref, ck_ref, sk_ref, ng_ref, o_ref, kt_ref):
    h = _rms(x_ref[...], g_ref[...]).astype(BF)
    dk_all = RET_HEADS * RET_DK
    half = RET_DK // 2
    q = _dot(h, w_ref[:, 0:dk_all])
    cq, sq = cq_ref[...], sq_ref[...]
    for hd in range(RET_HEADS):
        lanes = slice(hd * RET_DK, (hd + 1) * RET_DK)
        qh = q[:, lanes]
        o_ref[:, lanes] = (qh * cq + pltpu.roll(qh, half, 1) * sq).astype(BF)
    kt = _dot_nt(wkt_ref[...], h)
    ck, sk = ck_ref[...], sk_ref[...]
    scale = RET_DK ** -0.5
    for hd in range(RET_HEADS):
        k1 = kt[hd * RET_DK:hd * RET_DK + half, :]
        k2 = kt[hd * RET_DK + half:(hd + 1) * RET_DK, :]
        kt_ref[0, hd * RET_DK:hd * RET_DK + half, :] = ((k1 * ck - k2 * sk) * scale).astype(BF)
        kt_ref[0, hd * RET_DK + half:(hd + 1) * RET_DK, :] = ((k1 * sk + k2 * ck) * scale).astype(BF)
    g0 = dk_all + RET_HEADS * RET_DV
    for c0 in range(dk_all, o_ref.shape[1], dk_all):
        z = _dot(h, w_ref[:, c0 + dk_all:c0 + 2 * dk_all])
        if c0 >= g0:
            z = z * _sigmoid(z) * ng_ref[:, c0 - g0:c0 - g0 + dk_all]
        o_ref[:, c0:c0 + dk_all] = z.astype(BF)


def _in_odd(x2, gain, w, wkt, norm_gain, batch, tm=512):
    n, dm = x2.shape
    t = n // batch
    nt = t // tm
    dk_all = RET_HEADS * RET_DK
    half = RET_DK // 2
    inv = ROPE_BASE ** (-jnp.arange(half, dtype=F32) / half)
    ang = jnp.arange(t, dtype=F32)[:, None] * inv[None, :]
    cos, sin = jnp.cos(ang), jnp.sin(ang)
    cq = jnp.concatenate([cos, cos], axis=1)
    sq = jnp.concatenate([-sin, sin], axis=1)
    const = lambda shp: pl.BlockSpec(shp, lambda i: (0,) * len(shp))
    return pl.pallas_call(
        _in_odd_kernel,
        out_shape=[jax.ShapeDtypeStruct((n, w.shape[1] - dk_all), BF), jax.ShapeDtypeStruct((batch, dk_all, t), BF)],
        grid=(n // tm,),
        in_specs=[pl.BlockSpec((tm, dm), lambda i: (i, 0)), const((1, dm)), const(w.shape), const(wkt.shape),
                  pl.BlockSpec((tm, RET_DK), lambda i: (i % nt, 0)), pl.BlockSpec((tm, RET_DK), lambda i: (i % nt, 0)),
                  pl.BlockSpec((half, tm), lambda i: (0, i % nt)), pl.BlockSpec((half, tm), lambda i: (0, i % nt)),
                  const((1, norm_gain.shape[0]))],
        out_specs=[pl.BlockSpec((tm, w.shape[1] - dk_all), lambda i: (i, 0)),
                   pl.BlockSpec((1, dk_all, tm), lambda i: (i // nt, 0, i % nt))],
        compiler_params=_params(1, VMEM_LIMIT),
        name="in_odd",
    )(x2, gain[None].astype(F32), w, wkt, cq, sq, cos.T, sin.T, norm_gain[None].astype(F32))


def _ret_kernel(lg_ref, q_ref, kt_ref, v_ref, g_ref, o_ref, sb_ref, sf_ref, sbc_ref, *, n_chunks):
    c = RET_CHUNK
    hd = pl.program_id(1)
    lgf = lg_ref[0, hd]
    lgb = lg_ref[1, hd]

    ii = lax.broadcasted_iota(jnp.int32, (c, c), 0)
    jj = lax.broadcasted_iota(jnp.int32, (c, c), 1)
    dist = (ii - jj).astype(F32)
    decay = jnp.where(ii >= jj, jnp.exp(lgf * jnp.maximum(dist, 0.0)),
                      jnp.exp(lgb * jnp.maximum(-dist, 0.0))).astype(BF)
    tok_col = lax.broadcasted_iota(jnp.int32, (c, RET_DK), 0).astype(F32)
    tok_row = lax.broadcasted_iota(jnp.int32, (RET_DK, c), 1).astype(F32)
    q_dec_f = jnp.exp(lgf * (tok_col + 1.0)).astype(BF)
    q_dec_b = jnp.exp(lgb * (c - tok_col)).astype(BF)
    k_dec_f = jnp.exp(lgf * (c - 1.0 - tok_row)).astype(BF)
    k_dec_b = jnp.exp(lgb * tok_row).astype(BF)
    chunk_f = jnp.exp(lgf * c)
    chunk_b = jnp.exp(lgb * c)

    sbc_ref[...] = jnp.zeros_like(sbc_ref)

    def back(i, carry):
        ci = n_chunks - 1 - i
        t0 = pl.multiple_of(ci * c, c)
        sb_ref[ci] = sbc_ref[...].astype(BF)
        kd = kt_ref[0, :, pl.ds(t0, c)] * k_dec_b
        sbc_ref[...] = sbc_ref[...] * chunk_b + _dot(kd, v_ref[0, pl.ds(t0, c), :])
        return carry

    lax.fori_loop(0, n_chunks, back, 0, unroll=8)

    sf_ref[...] = jnp.zeros_like(sf_ref)

    def fwd(ci, carry):
        t0 = pl.multiple_of(ci * c, c)
        qc = q_ref[0, pl.ds(t0, c), :]
        kc = kt_ref[0, :, pl.ds(t0, c)]
        vc = v_ref[0, pl.ds(t0, c), :]
        s = _dot(qc, kc).astype(BF) * decay
        qd = jnp.concatenate([qc * q_dec_f, qc * q_dec_b], axis=1)
        st = jnp.concatenate([sf_ref[...].astype(BF), sb_ref[ci]], axis=0)
        o = _dot(s, vc) + _dot(qd, st)
        sf_ref[...] = sf_ref[...] * chunk_f + _dot(kc * k_dec_f, vc)
        o = o * lax.rsqrt(jnp.mean(o * o, axis=-1, keepdims=True) + EPS)
        o_ref[0, pl.ds(t0, c), :] = o.astype(BF) * g_ref[0, pl.ds(t0, c), :]
        return carry

    lax.fori_loop(0, n_chunks, fwd, 0, unroll=8)


def _retention(qvg, kt, log_gamma, batch):
    n, wq = qvg.shape
    t = n // batch
    assert t % RET_CHUNK == 0
    n_chunks = t // RET_CHUNK
    qvg3 = qvg.reshape(batch, t, wq)
    v_off = (RET_HEADS * RET_DK) // RET_DV
    g_off = v_off + RET_HEADS
    return pl.pallas_call(
        functools.partial(_ret_kernel, n_chunks=n_chunks),
        out_shape=jax.ShapeDtypeStruct((batch, t, RET_HEADS * RET_DV), BF),
        grid_spec=pltpu.PrefetchScalarGridSpec(
            num_scalar_prefetch=1,
            grid=(batch, RET_HEADS),
            in_specs=[pl.BlockSpec((1, t, RET_DK), lambda b, h, lg: (b, 0, h)),
                      pl.BlockSpec((1, RET_DK, t), lambda b, h, lg: (b, h, 0)),
                      pl.BlockSpec((1, t, RET_DV), lambda b, h, lg: (b, 0, v_off + h)),
                      pl.BlockSpec((1, t, RET_DV), lambda b, h, lg: (b, 0, g_off + h))],
            out_specs=pl.BlockSpec((1, t, RET_DV), lambda b, h, lg: (b, 0, h)),
            scratch_shapes=[pltpu.VMEM((n_chunks, RET_DK, RET_DV), BF),
                            pltpu.VMEM((RET_DK, RET_DV), F32), pltpu.VMEM((RET_DK, RET_DV), F32)]),
        compiler_params=_params(2, VMEM_LIMIT),
        name="retention",
    )(log_gamma, qvg3, kt, qvg3, qvg3)


def _out_odd_kernel(y_ref, x_ref, w_ref, *route_refs):
    o_ref = route_refs[N_ROUTE_IN]
    x_new = x_ref[...] + _dot(y_ref[...], w_ref[...])
    o_ref[...] = x_new
    _route(x_new, *route_refs[:N_ROUTE_IN], *route_refs[N_ROUTE_IN + 1:])


def _out_odd(y, x2, w, router, tm=512):
    n, dm = x2.shape
    kdim = y.shape[1]
    r_ops, r_in, r_shapes, r_out, r_scratch = _router_io(n, dm, tm, *router)
    return pl.pallas_call(
        _out_odd_kernel,
        out_shape=[jax.ShapeDtypeStruct((n, dm), F32)] + r_shapes,
        grid=(n // tm,),
        in_specs=[pl.BlockSpec((tm, kdim), lambda i: (i, 0)), pl.BlockSpec((tm, dm), lambda i: (i, 0)),
                  pl.BlockSpec(w.shape, lambda i: (0, 0))] + r_in,
        out_specs=[pl.BlockSpec((tm, dm), lambda i: (i, 0))] + r_out,
        scratch_shapes=r_scratch,
        compiler_params=_params(1, VMEM_LIMIT),
        name="out_odd",
    )(y, x2, w, *r_ops)


def _route(x, g_ref, wc_ref, wh_ref, b_ref, tri_ref, o_ref, ot_ref, cnt_ref, run_ref):
    @pl.when(pl.program_id(0) == 0)
    def _():
        run_ref[...] = jnp.zeros_like(run_ref)

    h = _rms(x, g_ref[...])
    hh, hl = _split_bf16(h)
    both = _dot(hh, wc_ref[...])
    nl = wh_ref.shape[1]
    logits = both[:, 0:nl] + both[:, nl:] + _dot(hl, wh_ref[...]) + b_ref[...]
    sub = tri_ref.shape[0]
    tri = tri_ref[...]
    run = run_ref[...]
    for r0 in range(0, x.shape[0], sub):
        info, run = _route_rows(logits[r0:r0 + sub], tri, run)
        o_ref[r0:r0 + sub, :] = info
        ot_ref[:, r0:r0 + sub] = info.T[0:ROW_ALIGN, :]
    run_ref[...] = run
    cnt_ref[...] = run


def _route_rows(logits, tri, run):
    lane = lax.broadcasted_iota(jnp.int32, logits.shape, 1).astype(F32)
    neg = jnp.float32(-jnp.inf)

    def top(mask):
        val = jnp.max(jnp.where(mask, logits, neg), axis=-1, keepdims=True)
        idx = jnp.min(jnp.where(mask & (logits == val), lane, 1e6), axis=-1, keepdims=True)
        return val, idx

    gmask = lane < MOE_GROUPS
    gmax, gidx = top(gmask)
    gprob = 1.0 / jnp.sum(jnp.where(gmask, jnp.exp(logits - gmax), 0.0), axis=-1, keepdims=True)
    base = MOE_GROUPS + MOE_PER_GROUP * gidx
    emask = (lane >= base) & (lane < base + MOE_PER_GROUP)
    v1, i1 = top(emask)
    v2, i2 = top(emask & (lane != i1))
    e2 = jnp.exp(v2 - v1)
    w1 = gprob / (1.0 + e2)
    w2 = gprob * e2 / (1.0 + e2)
    a1 = i1 - base
    a2 = i2 - base
    lo = jnp.minimum(a1, a2)
    hi = jnp.maximum(a1, a2)
    pair = lo * (7.0 - lo) * 0.5 + hi - lo - 1.0
    cls = gidx * MOE_PAIRS + pair
    w_lo = jnp.where(a1 < a2, w1, w2)
    w_hi = jnp.where(a1 < a2, w2, w1)
    onehot = lane == cls
    earlier = _dot(tri, jnp.where(onehot, 1.0, 0.0).astype(BF)) + run
    rank = jnp.sum(jnp.where(onehot, earlier, 0.0), axis=-1, keepdims=True)
    info = jnp.where(lane == 0, cls, jnp.where(lane == 1, w_lo, jnp.where(lane == 2, w_hi,
                     jnp.where(lane == 3, rank, 0.0))))
    return info, run + jnp.sum(jnp.where(onehot, 1.0, 0.0), axis=0, keepdims=True)


ROUTE_ROWS = 128
N_ROUTE_IN = 5


def _router_io(n, dm, tm, gain, w_cat, w_hi, b_r):
    tri = (jnp.arange(ROUTE_ROWS)[:, None] > jnp.arange(ROUTE_ROWS)[None, :]).astype(BF)
    const = lambda shp: pl.BlockSpec(shp, lambda i: (0, 0))
    operands = [gain[None].astype(F32), w_cat, w_hi, b_r, tri]
    in_specs = [const((1, dm)), const(w_cat.shape), const(w_hi.shape), const((1, 128)), const(tri.shape)]
    out_shapes = [jax.ShapeDtypeStruct((n, 128), F32), jax.ShapeDtypeStruct((ROW_ALIGN, n), F32),
                  jax.ShapeDtypeStruct((1, 128), F32)]
    out_specs = [pl.BlockSpec((tm, 128), lambda i: (i, 0)), pl.BlockSpec((ROW_ALIGN, tm), lambda i: (0, i)),
                 const((1, 128))]
    return operands, in_specs, out_shapes, out_specs, [pltpu.VMEM((1, 128), F32)]


def _start_row_dmas(n, copy_of_row):
    for r in range(n):
        copy_of_row(r).start()


def _dispatch_kernel(fpos_ref, fon_ref, pos_ref, x_ref, info_ref, o_hbm, aug_ref, zero_ref, sem, row_sem):
    dm = x_ref.shape[-1]
    tm = info_ref.shape[0]

    @pl.when(pl.program_id(0) == 0)
    def _():
        zero_ref[...] = jnp.zeros_like(zero_ref)
        n_fill = fpos_ref.shape[0]

        def fill(k):
            return pltpu.make_async_copy(
                zero_ref, o_hbm.at[pl.ds(pl.multiple_of(fpos_ref[k], ROW_ALIGN), MOE_FILL), :], sem)

        for k in range(n_fill - 1):
            pl.when(fon_ref[k] > 0)(lambda k=k: fill(k).start())
        for k in range(n_fill - 1):
            pl.when(fon_ref[k] > 0)(lambda k=k: fill(k).wait())
        fill(n_fill - 1).start()
        fill(n_fill - 1).wait()

    i = pl.program_id(0)
    n_steps = pl.num_programs(0)
    slot = i % 2

    def wait_rows(s):
        pltpu.make_async_copy(aug_ref.at[s], o_hbm.at[pl.ds(0, tm), :], row_sem.at[s]).wait()

    pl.when(i >= 2)(lambda: wait_rows(slot))
    aug_ref[slot, :, 0:dm] = x_ref[...].reshape(tm, dm)
    aug_ref[slot, :, dm:] = info_ref[...]
    _start_row_dmas(tm, lambda r: pltpu.make_async_copy(
        aug_ref.at[slot, pl.ds(r, 1), :], o_hbm.at[pl.ds(pos_ref[0, 0, r], 1), :], row_sem.at[slot]))

    @pl.when(i == n_steps - 1)
    def _():
        pl.when(n_steps >= 2)(lambda: wait_rows(1 - slot))
        wait_rows(slot)


def _token_tiles(x3, nb, tq):
    if nb == x3.shape[0]:
        return (lambda *shape_tail: pl.BlockSpec((nb, tq) + shape_tail, lambda i, *_: (0, i) + (0,) * len(shape_tail)),
                x3.shape[1] // tq)
    assert nb == 1 and tq == x3.shape[1]
    return (lambda *shape_tail: pl.BlockSpec((1, tq) + shape_tail, lambda i, *_: (i, 0) + (0,) * len(shape_tail)),
            x3.shape[0])


def _dispatch(x3, nb, tq, info, pos, first_pad, n_rows):
    dm = x3.shape[-1]
    n = info.shape[0]
    tm = nb * tq
    wa = dm + info.shape[1]
    spec, n_steps = _token_tiles(x3, nb, tq)
    class_fill = first_pad // ROW_ALIGN * ROW_ALIGN
    tail = class_fill[-1] + MOE_FILL + jnp.arange((n_rows - n) // MOE_FILL + 1, dtype=jnp.int32) * MOE_FILL
    tail_on = tail + MOE_FILL <= n_rows
    fpos = jnp.concatenate([class_fill, jnp.where(tail_on, tail, 0), jnp.array([n_rows - MOE_FILL], jnp.int32)])
    fon = jnp.concatenate([jnp.ones_like(class_fill), tail_on.astype(jnp.int32), jnp.ones((1,), jnp.int32)])
    return pl.pallas_call(
        _dispatch_kernel,
        out_shape=jax.ShapeDtypeStruct((n_rows, wa), F32),
        grid_spec=pltpu.PrefetchScalarGridSpec(
            num_scalar_prefetch=2,
            grid=(n_steps,),
            in_specs=[pl.BlockSpec((1, 1, tm), lambda i, fp, fo: (i, 0, 0), memory_space=pltpu.SMEM),
                      spec(dm),
                      pl.BlockSpec((tm, info.shape[1]), lambda i, fp, fo: (i, 0))],
            out_specs=pl.BlockSpec(memory_space=pl.ANY),
            scratch_shapes=[pltpu.VMEM((2, tm, wa), F32), pltpu.VMEM((MOE_FILL, wa), F32),
                            pltpu.SemaphoreType.DMA, pltpu.SemaphoreType.DMA((2,))]),
        compiler_params=_params(1, VMEM_LIMIT),
        name="moe_dispatch",
    )(fpos.astype(jnp.int32), fon, pos.reshape(n_steps, 1, tm), x3, info)


EXPERT_TILES = 2


def _expert_kernel(ea_ref, eb_ref, act_ref, xa_ref, g_ref, *refs):
    i = pl.program_id(0)
    o_ref = refs[-1]
    dm = o_ref.shape[1]
    tm = MOE_TILE

    @pl.when(act_ref[i * EXPERT_TILES] > 0)
    def _():
        for t in range(EXPERT_TILES):
            wga, wua, wda, wgb, wub, wdb = refs[6 * t:6 * t + 6]
            rows = slice(t * tm, (t + 1) * tm)
            h = _rms(xa_ref[rows, 0:dm], g_ref[...]).astype(BF)

            def ffn(wg, wu, wd, wrow):
                gate = _dot(h, wg[0])
                hid = gate * _sigmoid(gate) * _dot(h, wu[0])
                return _dot((hid * wrow).astype(BF), wd[0])

            o_ref[rows, :] = (ffn(wga, wua, wda, xa_ref[rows, dm + 1:dm + 2])
                              + ffn(wgb, wub, wdb, xa_ref[rows, dm + 2:dm + 3]))

    @pl.when(act_ref[i * EXPERT_TILES] == 0)
    def _():
        o_ref[...] = jnp.zeros_like(o_ref)


def _experts(xa_sorted, gain, tile_ea, tile_eb, tile_act, w_gate, w_up, w_down):
    dm = w_gate.shape[1]
    wa = xa_sorted.shape[1]
    n_tiles = tile_ea.shape[0]
    assert n_tiles % EXPERT_TILES == 0
    tm = MOE_TILE * EXPERT_TILES
    ff = w_gate.shape[-1]

    def wsel(shp, which, t):
        return pl.BlockSpec((1,) + shp, lambda i, ea, eb, act: ((ea, eb)[which][i * EXPERT_TILES + t], 0, 0))

    w_specs, w_ops = [], []
    for t in range(EXPERT_TILES):
        for which in (0, 1):
            w_specs += [wsel((dm, ff), which, t), wsel((dm, ff), which, t), wsel((ff, dm), which, t)]
            w_ops += [w_gate, w_up, w_down]
    return pl.pallas_call(
        _expert_kernel,
        out_shape=jax.ShapeDtypeStruct((n_tiles * MOE_TILE, dm), F32),
        grid_spec=pltpu.PrefetchScalarGridSpec(
            num_scalar_prefetch=3,
            grid=(n_tiles // EXPERT_TILES,),
            in_specs=[pl.BlockSpec((tm, wa), lambda i, ea, eb, act: (i * act[i * EXPERT_TILES], 0)),
                      pl.BlockSpec((1, dm), lambda i, ea, eb, act: (0, 0))] + w_specs,
            out_specs=pl.BlockSpec((tm, dm), lambda i, ea, eb, act: (i, 0))),
        compiler_params=_params(1, VMEM_LIMIT),
        name="moe_experts",
    )(tile_ea, tile_eb, tile_act, xa_sorted, gain, *w_ops)


def _combine_kernel(pos_ref, nxt_ref, x_ref, y_hbm, o_ref, ybuf, sem):
    i = pl.program_id(0)
    n_steps = pl.num_programs(0)
    tm = ybuf.shape[1]
    slot = i % 2

    def gather(idx_ref, s):
        _start_row_dmas(tm, lambda r: pltpu.make_async_copy(
            y_hbm.at[pl.ds(idx_ref[0, 0, r], 1), :], ybuf.at[s, pl.ds(r, 1), :], sem.at[s]))

    @pl.when(i == 0)
    def _():
        gather(pos_ref, 0)

    @pl.when(i + 1 < n_steps)
    def _():
        gather(nxt_ref, 1 - slot)

    pltpu.make_async_copy(y_hbm.at[pl.ds(0, tm), :], ybuf.at[slot], sem.at[slot]).wait()
    o_ref[...] = x_ref[...] + ybuf[slot].reshape(x_ref.shape)


def _combine(x3, nb, tq, y_sorted, pos):
    dm = x3.shape[-1]
    tm = nb * tq
    spec, n_steps = _token_tiles(x3, nb, tq)
    pos3 = pos.reshape(n_steps, 1, tm)
    return pl.pallas_call(
        _combine_kernel,
        out_shape=jax.ShapeDtypeStruct(x3.shape, F32),
        grid=(n_steps,),
        in_specs=[pl.BlockSpec((1, 1, tm), lambda i: (i, 0, 0), memory_space=pltpu.SMEM),
                  pl.BlockSpec((1, 1, tm), lambda i: (jnp.minimum(i + 1, n_steps - 1), 0, 0), memory_space=pltpu.SMEM),
                  spec(dm),
                  pl.BlockSpec(memory_space=pl.ANY)],
        out_specs=spec(dm),
        scratch_shapes=[pltpu.VMEM((2, tm, dm), F32), pltpu.SemaphoreType.DMA((2,))],
        compiler_params=_params(1, VMEM_LIMIT),
        name="moe_combine",
    )(pos3, pos3, x3, y_sorted)


def _slots_kernel(starts_ref, rows_ref, o_ref):
    cls = rows_ref[0:1, :]
    base = jnp.zeros_like(cls)
    for c in range(MOE_CLASSES):
        base = jnp.where(cls == c, starts_ref[c].astype(F32), base)
    o_ref[...] = (base + rows_ref[3:4, :]).astype(jnp.int32)


def _slots(info_rows, starts, tn=2048):
    n = info_rows.shape[1]
    tn = min(tn, n)
    return pl.pallas_call(
        _slots_kernel,
        out_shape=jax.ShapeDtypeStruct((1, n), jnp.int32),
        grid_spec=pltpu.PrefetchScalarGridSpec(
            num_scalar_prefetch=1,
            grid=(n // tn,),
            in_specs=[pl.BlockSpec((ROW_ALIGN, tn), lambda i, s: (0, i))],
            out_specs=pl.BlockSpec((1, tn), lambda i, s: (0, i))),
        compiler_params=_params(1),
        name="moe_slots",
    )(starts.astype(jnp.int32), info_rows)


def _moe(x3, nb, tq, info, info_rows, cnt, gain, layer, w_gate, w_up, w_down):
    n = info.shape[0]
    tm = MOE_TILE
    gain = gain[None].astype(F32)
    counts = cnt[0, :MOE_CLASSES].astype(jnp.int32)
    padded = ((counts + tm - 1) // tm) * tm
    ends = jnp.cumsum(padded)
    starts = ends - padded
    pos = _slots(info_rows, starts)
    n_tiles = -(-(n // tm + MOE_CLASSES) // EXPERT_TILES) * EXPERT_TILES
    tile_start = jnp.arange(n_tiles, dtype=jnp.int32) * tm
    tile_cls = jnp.minimum(jnp.sum(tile_start[:, None] >= ends[None, :], axis=1), MOE_CLASSES - 1)
    tile_act = (tile_start < ends[-1]).astype(jnp.int32)
    pair_lo = jnp.array([0, 0, 0, 1, 1, 2], jnp.int32)
    pair_hi = jnp.array([1, 2, 3, 2, 3, 3], jnp.int32)
    first = layer * MOE_GROUPS * MOE_PER_GROUP + (tile_cls // MOE_PAIRS) * MOE_PER_GROUP
    tile_ea = (first + pair_lo[tile_cls % MOE_PAIRS]).astype(jnp.int32)
    tile_eb = (first + pair_hi[tile_cls % MOE_PAIRS]).astype(jnp.int32)
    xa_sorted = _dispatch(x3, nb, tq, info, pos, (starts + counts).astype(jnp.int32), (n_tiles + 2) * tm)
    y_sorted = _experts(xa_sorted, gain, tile_ea, tile_eb, tile_act, w_gate, w_up, w_down)
    return _combine(x3, nb, tq, y_sorted, pos)


def _router_weights(w_group, b_group, w_router, b_router):
    dm = w_group.shape[0]
    w = jnp.concatenate([w_group.astype(F32), w_router.astype(F32).reshape(dm, -1)], axis=1)
    b = jnp.concatenate([b_group.astype(F32), b_router.astype(F32).reshape(-1)])
    pad = 128 - w.shape[1]
    w_hi, w_lo = _split_bf16(jnp.pad(w, ((0, 0), (0, pad))))
    return jnp.concatenate([w_hi, w_lo], axis=1), w_hi, jnp.pad(b, (0, pad))[None]


def _trunk(x, p):
    b, t, dm = x.shape
    n = b * t

    tq = _even_tile(b)
    ug, q, k, v = _in_even(x, p["norm_mix"][0], p["w_in_even"], p["na_q_gain"], p["na_k_gain"])
    yg = _s5_scan(ug, *p["s5_ops"], p["s5_d_rows"], t // S5_CHUNK, b)
    att = _na(q, k, v, p["na_table"])
    x3, *routing = _out_even(yg, att, x, p["s5_w_glu"], p["s5_b_glu"], p["w_out_even"],
                             (p["norm_ffn"][0], *p["router"][0]))
    x3 = _moe(x3, b, tq, *routing, p["norm_ffn"][0], 0, *p["experts"])

    x2 = x3.reshape(n, dm)
    qvg, kt = _in_odd(x2, p["norm_mix"][1], p["w_in_odd"], p["w_k_t"], p["ret_norm_gain"], b)
    yr = _retention(qvg, kt, p["log_gamma"], b)
    tm = 512
    x2, *routing = _out_odd(yr.reshape(n, -1), x2, p["w_out_odd"], (p["norm_ffn"][1], *p["router"][1]), tm)
    x3 = _moe(x2.reshape(n // tm, tm, dm), 1, tm, *routing, p["norm_ffn"][1], 1, *p["experts"])
    return x3.reshape(b, t, dm)


def kernel(x_prompt, x_sample, norm_mix, norm_ffn, w_in_even, w_out_even, s5_lambda_re, s5_lambda_im, s5_log_dt, s5_b_re, s5_b_im, s5_c_re, s5_c_im, s5_d, s5_w_glu, s5_b_glu, na_q_gain, na_k_gain, na_rpb, w_in_odd, w_out_odd, ret_decay_logit, ret_norm_gain, moe_w_group, moe_b_group, moe_w_router, moe_b_router, moe_w_gate, moe_w_up, moe_w_down):
    assert norm_mix.shape[0] == 2
    dk_all = RET_HEADS * RET_DK
    w_odd = w_in_odd[0].astype(BF)
    p = {
        "norm_mix": norm_mix, "norm_ffn": norm_ffn,
        "w_in_even": w_in_even[0].astype(BF), "w_out_even": w_out_even[0].astype(BF),
        "s5_ops": _s5_operators(s5_lambda_re[0], s5_lambda_im[0], s5_log_dt[0], s5_b_re[0], s5_b_im[0],
                                s5_c_re[0], s5_c_im[0]),
        "s5_d_rows": jnp.tile(s5_d[0].astype(F32).reshape(S5_GROUPS, 1, S5_GROUP), (1, 1, S5_CHUNK)),
        "s5_w_glu": s5_w_glu[0].astype(BF), "s5_b_glu": s5_b_glu[0],
        "na_q_gain": na_q_gain[0], "na_k_gain": na_k_gain[0], "na_table": _na_bias_table(na_rpb[0]),
        "w_in_odd": w_odd, "w_k_t": w_in_odd[0][:, dk_all:2 * dk_all].T.astype(BF),
        "w_out_odd": w_out_odd[0].astype(BF),
        "log_gamma": jax.nn.log_sigmoid(ret_decay_logit[0].astype(F32)),
        "ret_norm_gain": ret_norm_gain[0],
        "router": [_router_weights(moe_w_group[l], moe_b_group[l], moe_w_router[l], moe_b_router[l]) for l in range(2)],
        "experts": tuple(w.astype(BF).reshape((-1,) + w.shape[2:]) for w in (moe_w_gate, moe_w_up, moe_w_down)),
    }
    return _trunk(x_prompt, p), _trunk(x_sample, p)
```

```python
import functools
import math

import jax
import jax.numpy as jnp
from jax import lax
from jax.experimental import pallas as pl
from jax.experimental.pallas import tpu as pltpu

F32 = jnp.float32
BF = jnp.bfloat16
EPS = 1e-6

D_MODEL = 1024
GRID_W = 64
S5_GROUPS = 32
S5_GROUP = 16
S5_STATE = 64
S5_CHUNK = 16
NA_HEADS = 8
NA_HEAD_DIM = 64
NA_WIN_R = 8
NA_WIN_C = 16
NA_HEAD_GROUP = 4
RET_HEADS = 8
RET_DK = 128
RET_DV = 256
RET_CHUNK = 256
ROPE_BASE = 10000.0
MOE_GROUPS = 4
MOE_PER_GROUP = 4
MOE_FF = 512
MOE_PAIRS = 6
MOE_CLASSES = MOE_GROUPS * MOE_PAIRS
MOE_TILE = 256
ROW_ALIGN = 8
MOE_FILL = MOE_TILE + ROW_ALIGN
MASK_NEG = -1e30
VMEM_LIMIT = 56 * 1024 * 1024


def _params(n_axes, vmem=None):
    return pltpu.CompilerParams(dimension_semantics=("arbitrary",) * n_axes,
                                vmem_limit_bytes=vmem)


def _rms(x, gain):
    ms = jnp.mean(x * x, axis=-1, keepdims=True)
    return x * lax.rsqrt(ms + EPS) * gain


def _sigmoid(x):
    return 1.0 / (1.0 + jnp.exp(-x))


def _dot(a, b):
    return jnp.dot(a, b, preferred_element_type=F32)


def _dot_nt(a, b):
    return lax.dot_general(a, b, (((1,), (1,)), ((), ())), preferred_element_type=F32)


def _split_bf16(x):
    hi = x.astype(BF)
    lo = (x - hi.astype(F32)).astype(BF)
    return hi, lo


def _dot3(a, b):
    ah, al = _split_bf16(a)
    bh, bl = _split_bf16(b)
    return _dot(ah, bh) + _dot(ah, bl) + _dot(al, bh)


def _chunk_major_perm(batch, tq):
    nc = tq // S5_CHUNK
    r = jnp.arange(batch * tq)
    j, c, b = r // (nc * batch), (r // batch) % nc, r % batch
    src = b * tq + c * S5_CHUNK + j
    return (src[:, None] == jnp.arange(batch * tq)[None, :]).astype(BF)


def _in_even_kernel(x_ref, g_ref, w_ref, qg_ref, kg_ref, seg_ref, perm_ref, ug_ref, q_ref, k_ref, v_ref):
    batch, tq, dm = x_ref.shape
    h = _rms(x_ref[...].reshape(batch * tq, dm), g_ref[...]).astype(BF)
    d = q_ref.shape[-1]
    nr = ug_ref.shape[1]
    up = _dot(perm_ref[...], _dot(h, w_ref[:, 0:d]).astype(BF))
    for g in range(S5_GROUPS):
        lanes = slice(g * S5_GROUP, (g + 1) * S5_GROUP)
        ug_ref[g] = jnp.concatenate([up[j * nr:(j + 1) * nr, lanes] for j in range(S5_CHUNK)], axis=1).astype(BF)
    seg = seg_ref[...]

    def head_norm(z, gain):
        ssq = _dot((z * z).astype(BF), seg) * (1.0 / NA_HEAD_DIM)
        return z * lax.rsqrt(ssq + EPS) * gain

    q = _dot(h, w_ref[:, d:2 * d])
    q_ref[...] = (head_norm(q, qg_ref[...]) * (NA_HEAD_DIM ** -0.5)).astype(BF).reshape(batch, tq, d)
    k = _dot(h, w_ref[:, 2 * d:3 * d])
    k_ref[...] = head_norm(k, kg_ref[...]).astype(BF).reshape(batch, tq, d)
    v_ref[...] = _dot(h, w_ref[:, 3 * d:4 * d]).astype(BF).reshape(batch, tq, d)


def _even_tile(batch, tm=512):
    tq = tm // batch
    assert tq * batch == tm and tq % S5_CHUNK == 0
    return tq


def _in_even(x3, gain, w, q_gain, k_gain):
    batch, t, dm = x3.shape
    d = w.shape[1] // 4
    tq = _even_tile(batch)
    nr = (tq // S5_CHUNK) * batch
    seg = (jnp.arange(d)[:, None] // NA_HEAD_DIM == jnp.arange(d)[None, :] // NA_HEAD_DIM).astype(BF)
    qg = jnp.tile(q_gain.astype(F32), NA_HEADS)[None]
    kg = jnp.tile(k_gain.astype(F32), NA_HEADS)[None]
    tok = lambda width: pl.BlockSpec((batch, tq, width), lambda i: (0, i, 0))
    full = lambda shp: pl.BlockSpec(shp, lambda i: (0,) * len(shp))
    gw = S5_CHUNK * S5_GROUP
    perm = _chunk_major_perm(batch, tq)
    return pl.pallas_call(
        _in_even_kernel,
        out_shape=[jax.ShapeDtypeStruct((S5_GROUPS, (t // S5_CHUNK) * batch, gw), BF)]
        + [jax.ShapeDtypeStruct((batch, t, d), BF)] * 3,
        grid=(t // tq,),
        in_specs=[tok(dm), full((1, dm)), full(w.shape), full((1, d)), full((1, d)), full((d, d)), full(perm.shape)],
        out_specs=[pl.BlockSpec((S5_GROUPS, nr, gw), lambda i: (0, i, 0)), tok(d), tok(d), tok(d)],
        compiler_params=_params(1, VMEM_LIMIT),
        name="in_even",
    )(x3, gain[None].astype(F32), w, qg, kg, seg, perm)


def _s5_taps_kernel(l_ref, r_ref, o_ref):
    for g in range(l_ref.shape[0]):
        o_ref[g] = _dot3(l_ref[g], r_ref[g])


def _s5_taps(lhs, rhs, per_step=8):
    g2, m, k = lhs.shape
    nn = rhs.shape[-1]
    return pl.pallas_call(
        _s5_taps_kernel,
        out_shape=jax.ShapeDtypeStruct((g2, m, nn), F32),
        grid=(g2 // per_step,),
        in_specs=[pl.BlockSpec((per_step, m, k), lambda i: (i, 0, 0)), pl.BlockSpec((per_step, k, nn), lambda i: (i, 0, 0))],
        out_specs=pl.BlockSpec((per_step, m, nn), lambda i: (i, 0, 0)),
        compiler_params=_params(1),
        name="s5_taps",
    )(lhs, rhs)


def _s5_operators(lam_re, lam_im, log_dt, b_re, b_im, c_re, c_im):
    L, G, H, P_ = S5_CHUNK, S5_GROUPS, S5_GROUP, S5_STATE
    lr, li = lam_re.astype(F32), lam_im.astype(F32)
    dt = jnp.exp(log_dt.astype(F32))[..., None]
    kk = jnp.arange(L + 1, dtype=F32)[:, None, None, None]
    mag = jnp.exp(kk * (lr * dt)[None])
    ang = kk * (li * dt)[None]
    pw = (mag * jnp.cos(ang), mag * jnp.sin(ang))

    def cmul(x, y):
        return x[0] * y[0] - x[1] * y[1], x[0] * y[1] + x[1] * y[0]

    xr, xi = pw[0][1] - 1.0, pw[1][1]
    den = lr * lr + li * li
    coef = ((xr * lr + xi * li) / den, (xi * lr - xr * li) / den)
    bbar = cmul((coef[0][..., None], coef[1][..., None]), (b_re.astype(F32), b_im.astype(F32)))
    c = (c_re.astype(F32), c_im.astype(F32))

    def at(k_slice, direction, perm):
        return tuple(jnp.transpose(z[k_slice, direction], perm) for z in pw)

    apl = tuple(jnp.transpose(z[:L], (1, 2, 0, 3))[:, :, :, None, :] for z in pw)
    ce = cmul((c[0][:, :, None], c[1][:, :, None]), apl)
    lhs = jnp.concatenate([ce[0], -ce[1]], axis=-1).reshape(2 * G, L * H, 2 * P_)
    rhs = jnp.concatenate([bbar[0], bbar[1]], axis=2).reshape(2 * G, 2 * P_, H)
    taps = _s5_taps(lhs, rhs).reshape(2, G, L, H, H)
    taps = jnp.transpose(taps, (0, 1, 2, 4, 3))
    kf, kb = taps[0], taps[1]
    lagged = jnp.concatenate([jnp.flip(kb[:, 1:], axis=1), (kf[:, 0] + kb[:, 0])[:, None], kf[:, 1:]], axis=1)
    idx = jnp.arange(L)[None, :] - jnp.arange(L)[:, None] + (L - 1)
    m = jnp.transpose(lagged[:, idx], (0, 1, 3, 2, 4)).reshape(G, L * H, L * H)

    bt = tuple(jnp.transpose(z, (0, 1, 3, 2)) for z in bbar)
    expand_j = lambda z: z[:, :, None, :]
    pf = cmul(tuple(map(expand_j, at(slice(L - 1, None, -1), 0, (1, 0, 2)))), (bt[0][0][:, None], bt[1][0][:, None]))
    pb = cmul(tuple(map(expand_j, at(slice(0, L), 1, (1, 0, 2)))), (bt[0][1][:, None], bt[1][1][:, None]))
    p = jnp.concatenate([pf[0], pb[0], pf[1], pb[1]], axis=-1).reshape(G, L * H, 4 * P_)

    ct = tuple(jnp.transpose(z, (0, 1, 3, 2)) for z in c)
    expand_h = lambda z: z[..., None]
    gf = cmul(tuple(map(expand_h, at(slice(1, None), 0, (1, 2, 0)))), (ct[0][0][:, :, None, :], ct[1][0][:, :, None, :]))
    gb = cmul(tuple(map(expand_h, at(slice(L, 0, -1), 1, (1, 2, 0)))), (ct[0][1][:, :, None, :], ct[1][1][:, :, None, :]))
    q = jnp.concatenate([gf[0], gb[0], -gf[1], -gb[1]], axis=1).reshape(G, 4 * P_, L * H)

    a_mat = jnp.stack([jnp.concatenate([pw[0][L, 0], pw[0][L, 1]], -1),
                       jnp.concatenate([pw[1][L, 0], pw[1][L, 1]], -1)], axis=1)
    return m.astype(BF), p.astype(BF), q.astype(BF), a_mat.astype(F32)


def _s5_kernel(u_ref, m_ref, p_ref, q_ref, a_ref, d_ref, y_ref, z_ref, s_ref, *, n_chunks, batch):
    u = u_ref[0]
    z_ref[...] = _dot(u, p_ref[0])
    half = 2 * S5_STATE
    ar = a_ref[0, 0:1, :]
    ai = a_ref[0, 1:2, :]
    fwd_lane = lax.broadcasted_iota(jnp.int32, (batch, half), 1) < S5_STATE

    def step(k, carry):
        sr, si = carry
        f0 = pl.multiple_of(k * batch, batch)
        b0 = pl.multiple_of((n_chunks - 1 - k) * batch, batch)
        s_ref[pl.ds(f0, batch), 0:S5_STATE] = sr[:, 0:S5_STATE]
        s_ref[pl.ds(b0, batch), S5_STATE:half] = sr[:, S5_STATE:half]
        s_ref[pl.ds(f0, batch), half:half + S5_STATE] = si[:, 0:S5_STATE]
        s_ref[pl.ds(b0, batch), half + S5_STATE:2 * half] = si[:, S5_STATE:half]
        zr = jnp.where(fwd_lane, z_ref[pl.ds(f0, batch), 0:half], z_ref[pl.ds(b0, batch), 0:half])
        zi = jnp.where(fwd_lane, z_ref[pl.ds(f0, batch), half:2 * half], z_ref[pl.ds(b0, batch), half:2 * half])
        return ar * sr - ai * si + zr, ar * si + ai * sr + zi

    zero = jnp.zeros((batch, half), F32)
    lax.fori_loop(0, n_chunks, step, (zero, zero), unroll=8)
    y = _dot(u, m_ref[0]) + _dot(s_ref[...].astype(BF), q_ref[0]) + u.astype(F32) * d_ref[0]
    y_ref[0] = y.astype(BF)


def _s5_scan(ug, m, p, q, a_mat, d_skip, n_chunks, batch):
    g, r, w = ug.shape
    blk = lambda shp: pl.BlockSpec((1,) + shp, lambda i: (i, 0, 0))
    return pl.pallas_call(
        functools.partial(_s5_kernel, n_chunks=n_chunks, batch=batch),
        out_shape=jax.ShapeDtypeStruct((g, r, w), BF),
        grid=(g,),
        in_specs=[blk((r, w)), blk((w, w)), blk((w, w)), blk((w, w)), blk((2, w // 2)), blk((1, w))],
        out_specs=blk((r, w)),
        scratch_shapes=[pltpu.VMEM((r, w), F32), pltpu.VMEM((r, w), F32)],
        compiler_params=_params(1, VMEM_LIMIT),
        name="s5_scan",
    )(ug, m, p, q, a_mat, d_skip)


def _na_bias_table(rpb):
    qc = jnp.arange(GRID_W)[:, None]
    kc = jnp.arange(GRID_W)[None, :]
    ws = jnp.clip(qc - NA_WIN_C // 2, 0, GRID_W - NA_WIN_C)
    valid = (kc >= ws) & (kc < ws + NA_WIN_C)
    n_tap = 2 * NA_WIN_C - 1
    pad = GRID_W - NA_WIN_C
    row = n_tap + 2 * pad
    taps = jnp.pad(rpb.astype(F32), ((0, 0), (0, 0), (pad, pad)))
    skew = jnp.tile(taps, (1, 1, GRID_W))[:, :, GRID_W - 1:GRID_W - 1 + GRID_W * (row - 1)]
    bias = skew.reshape(rpb.shape[0], rpb.shape[1], GRID_W, row - 1)[..., :GRID_W]
    e = jnp.where(valid[None, None], bias, MASK_NEG)
    return jnp.concatenate([e[:, :-1], e[:, 1:]], axis=-1)


def _na_kernel(q_ref, k_ref, v_ref, e_ref, o_ref, *, rows):
    hg_w = NA_HEAD_GROUP * NA_HEAD_DIM
    lane_head = lax.broadcasted_iota(jnp.int32, (GRID_W, hg_w), 1) // NA_HEAD_DIM
    n_keys = NA_WIN_R * GRID_W

    def row_body(r, carry):
        rs = jnp.clip(r - NA_WIN_R // 2, 0, rows - NA_WIN_R)
        di = rs - r + (NA_WIN_R - 1)
        q0 = pl.multiple_of(r * GRID_W, GRID_W)
        k0 = pl.multiple_of(rs * GRID_W, GRID_W)
        for hg in range(NA_HEADS // NA_HEAD_GROUP):
            lanes = slice(hg * hg_w, (hg + 1) * hg_w)
            q4 = q_ref[0, pl.ds(q0, GRID_W), lanes]
            qs = jnp.concatenate([jnp.where(lane_head == hh, q4, jnp.zeros_like(q4))
                                  for hh in range(NA_HEAD_GROUP)], axis=0)
            k4 = k_ref[0, pl.ds(k0, n_keys), lanes]
            v4 = v_ref[0, pl.ds(k0, n_keys), lanes]
            s = _dot_nt(qs, k4)
            bias = jnp.concatenate(
                [jnp.concatenate([e_ref[hg * NA_HEAD_GROUP + hh, di + 2 * w2] for w2 in range(NA_WIN_R // 2)], axis=1)
                 for hh in range(NA_HEAD_GROUP)], axis=0)
            s = s + bias
            mx = jnp.max(s, axis=-1, keepdims=True)
            p = jnp.exp(s - mx)
            den = jnp.sum(p, axis=-1, keepdims=True)
            o = _dot(p.astype(BF), v4) * (1.0 / den)
            out = jnp.zeros((GRID_W, hg_w), F32)
            for hh in range(NA_HEAD_GROUP):
                out = jnp.where(lane_head == hh, o[hh * GRID_W:(hh + 1) * GRID_W], out)
            o_ref[0, pl.ds(q0, GRID_W), lanes] = out.astype(BF)
        return carry

    lax.fori_loop(0, rows, row_body, 0, unroll=16)


def _na(q, k, v, table):
    b, t, d = q.shape
    rows = t // GRID_W
    assert rows >= NA_WIN_R and t % GRID_W == 0
    seq = pl.BlockSpec((1, t, d), lambda i: (i, 0, 0))
    return pl.pallas_call(
        functools.partial(_na_kernel, rows=rows),
        out_shape=jax.ShapeDtypeStruct((b, t, d), BF),
        grid=(b,),
        in_specs=[seq, seq, seq, pl.BlockSpec(table.shape, lambda i: (0, 0, 0, 0))],
        out_specs=seq,
        compiler_params=_params(1, VMEM_LIMIT),
        name="na",
    )(q, k, v, table)


EVEN_TILES = 2


def _out_even_kernel(yg_ref, a_ref, x_ref, wg_ref, bg_ref, wo_ref, perm_ref, *route_refs):
    o_ref = route_refs[N_ROUTE_IN]
    batch, t_step, dm = x_ref.shape
    tq = t_step // EVEN_TILES
    nr = yg_ref.shape[1] // EVEN_TILES
    da = a_ref.shape[-1]
    c0 = math.sqrt(2.0 / math.pi)
    _route_init(route_refs[-1])
    for s in range(EVEN_TILES):
        toks = slice(s * tq, (s + 1) * tq)
        rows = [jnp.concatenate([yg_ref[g, s * nr:(s + 1) * nr, i * S5_GROUP:(i + 1) * S5_GROUP]
                                 for g in range(S5_GROUPS)], axis=1) for i in range(S5_CHUNK)]
        ys = _dot(perm_ref[...], jnp.concatenate(rows, axis=0))
        ys = 0.5 * ys * (1.0 + jnp.tanh(c0 * (ys + 0.044715 * (ys * ys * ys))))
        gate = _sigmoid(_dot(ys.astype(BF), wg_ref[...]) + bg_ref[...])
        a_out = (ys * gate).astype(BF)
        att = a_ref[:, toks, :].reshape(batch * tq, da)
        x_new = (x_ref[:, toks, :].reshape(batch * tq, dm) + _dot(a_out, wo_ref[0:da, :])
                 + _dot(att, wo_ref[da:, :]))
        o_ref[:, toks, :] = x_new.reshape(batch, tq, dm)
        _route(x_new, *route_refs[:N_ROUTE_IN], *route_refs[N_ROUTE_IN + 1:], row0=s * batch * tq)


def _out_even(yg, att, x3, w_glu, b_glu, w_out, router):
    batch, t, dm = x3.shape
    d = att.shape[-1]
    groups, _, gw = yg.shape
    tq = _even_tile(batch)
    t_step = EVEN_TILES * tq
    assert t % t_step == 0
    nr = (t_step // S5_CHUNK) * batch
    tok = lambda width: pl.BlockSpec((batch, t_step, width), lambda i: (0, i, 0))
    full = lambda shp: pl.BlockSpec(shp, lambda i: (0,) * len(shp))
    r_ops, r_in, r_shapes, r_out, r_scratch = _router_io(batch * t, dm, batch * t_step, *router)
    perm = _chunk_major_perm(batch, tq).T
    return pl.pallas_call(
        _out_even_kernel,
        out_shape=[jax.ShapeDtypeStruct((batch, t, dm), F32)] + r_shapes,
        grid=(t // t_step,),
        in_specs=[pl.BlockSpec((groups, nr, gw), lambda i: (0, i, 0)), tok(d), tok(dm),
                  full((d, d)), full((1, d)), full(w_out.shape), full(perm.shape)] + r_in,
        out_specs=[tok(dm)] + r_out,
        scratch_shapes=r_scratch,
        compiler_params=_params(1, VMEM_LIMIT),
        name="out_even",
    )(yg, att, x3, w_glu, b_glu[None].astype(F32), w_out, perm, *r_ops)


def _in_odd_kernel(x_ref, g_ref, w_ref, wkt_ref, cq_ref, sq_ref, ck_ref, sk_ref, ng_ref, o_ref, kt_ref):
    h = _rms(x_ref[...], g_ref[...]).astype(BF)
    dk_all = RET_HEADS * RET_DK
    half = RET_DK // 2
    q = _dot(h, w_ref[:, 0:dk_all])
    cq, sq = cq_ref[...], sq_ref[...]
    for hd in range(RET_HEADS):
        lanes = slice(hd * RET_DK, (hd + 1) * RET_DK)
        qh = q[:, lanes]
        o_ref[:, lanes] = (qh * cq + pltpu.roll(qh, half, 1) * sq).astype(BF)
    kt = _dot_nt(wkt_ref[...], h)
    ck, sk = ck_ref[...], sk_ref[...]
    scale = RET_DK ** -0.5
    for hd in range(RET_HEADS):
        k1 = kt[hd * RET_DK:hd * RET_DK + half, :]
        k2 = kt[hd * RET_DK + half:(hd + 1) * RET_DK, :]
        kt_ref[0, hd * RET_DK:hd * RET_DK + half, :] = ((k1 * ck - k2 * sk) * scale).astype(BF)
        kt_ref[0, hd * RET_DK + half:(hd + 1) * RET_DK, :] = ((k1 * sk + k2 * ck) * scale).astype(BF)
    g0 = dk_all + RET_HEADS * RET_DV
    for c0 in range(dk_all, o_ref.shape[1], dk_all):
        z = _dot(h, w_ref[:, c0 + dk_all:c0 + 2 * dk_all])
        if c0 >= g0:
            z = z * _sigmoid(z) * ng_ref[:, c0 - g0:c0 - g0 + dk_all]
        o_ref[:, c0:c0 + dk_all] = z.astype(BF)


def _in_odd(x2, gain, w, wkt, norm_gain, batch, tm=512):
    n, dm = x2.shape
    t = n // batch
    nt = t // tm
    dk_all = RET_HEADS * RET_DK
    half = RET_DK // 2
    inv = ROPE_BASE ** (-jnp.arange(half, dtype=F32) / half)
    ang = jnp.arange(t, dtype=F32)[:, None] * inv[None, :]
    cos, sin = jnp.cos(ang), jnp.sin(ang)
    cq = jnp.concatenate([cos, cos], axis=1)
    sq = jnp.concatenate([-sin, sin], axis=1)
    const = lambda shp: pl.BlockSpec(shp, lambda i: (0,) * len(shp))
    return pl.pallas_call(
        _in_odd_kernel,
        out_shape=[jax.ShapeDtypeStruct((n, w.shape[1] - dk_all), BF), jax.ShapeDtypeStruct((batch, dk_all, t), BF)],
        grid=(n // tm,),
        in_specs=[pl.BlockSpec((tm, dm), lambda i: (i, 0)), const((1, dm)), const(w.shape), const(wkt.shape),
                  pl.BlockSpec((tm, RET_DK), lambda i: (i % nt, 0)), pl.BlockSpec((tm, RET_DK), lambda i: (i % nt, 0)),
                  pl.BlockSpec((half, tm), lambda i: (0, i % nt)), pl.BlockSpec((half, tm), lambda i: (0, i % nt)),
                  const((1, norm_gain.shape[0]))],
        out_specs=[pl.BlockSpec((tm, w.shape[1] - dk_all), lambda i: (i, 0)),
                   pl.BlockSpec((1, dk_all, tm), lambda i: (i // nt, 0, i % nt))],
        compiler_params=_params(1, VMEM_LIMIT),
        name="in_odd",
    )(x2, gain[None].astype(F32), w, wkt, cq, sq, cos.T, sin.T, norm_gain[None].astype(F32))


def _ret_kernel(lg_ref, q_ref, kt_ref, v_ref, g_ref, o_ref, sb_ref, sf_ref, sbc_ref, *, n_chunks):
    c = RET_CHUNK
    hd = pl.program_id(1)
    lgf = lg_ref[0, hd]
    lgb = lg_ref[1, hd]

    ii = lax.broadcasted_iota(jnp.int32, (c, c), 0)
    jj = lax.broadcasted_iota(jnp.int32, (c, c), 1)
    dist = (ii - jj).astype(F32)
    decay = jnp.where(ii >= jj, jnp.exp(lgf * jnp.maximum(dist, 0.0)),
                      jnp.exp(lgb * jnp.maximum(-dist, 0.0))).astype(BF)
    tok_col = lax.broadcasted_iota(jnp.int32, (c, RET_DK), 0).astype(F32)
    tok_row = lax.broadcasted_iota(jnp.int32, (RET_DK, c), 1).astype(F32)
    q_dec_f = jnp.exp(lgf * (tok_col + 1.0)).astype(BF)
    q_dec_b = jnp.exp(lgb * (c - tok_col)).astype(BF)
    k_dec_f = jnp.exp(lgf * (c - 1.0 - tok_row)).astype(BF)
    k_dec_b = jnp.exp(lgb * tok_row).astype(BF)
    chunk_f = jnp.exp(lgf * c)
    chunk_b = jnp.exp(lgb * c)

    sbc_ref[...] = jnp.zeros_like(sbc_ref)

    def back(i, carry):
        ci = n_chunks - 1 - i
        t0 = pl.multiple_of(ci * c, c)
        sb_ref[ci] = sbc_ref[...].astype(BF)
        kd = kt_ref[0, :, pl.ds(t0, c)] * k_dec_b
        sbc_ref[...] = sbc_ref[...] * chunk_b + _dot(kd, v_ref[0, pl.ds(t0, c), :])
        return carry

    lax.fori_loop(0, n_chunks, back, 0, unroll=min(16, n_chunks))

    sf_ref[...] = jnp.zeros_like(sf_ref)

    def fwd(ci, carry):
        t0 = pl.multiple_of(ci * c, c)
        qc = q_ref[0, pl.ds(t0, c), :]
        kc = kt_ref[0, :, pl.ds(t0, c)]
        vc = v_ref[0, pl.ds(t0, c), :]
        s = _dot(qc, kc).astype(BF) * decay
        qd = jnp.concatenate([qc * q_dec_f, qc * q_dec_b], axis=1)
        st = jnp.concatenate([sf_ref[...].astype(BF), sb_ref[ci]], axis=0)
        o = _dot(s, vc) + _dot(qd, st)
        sf_ref[...] = sf_ref[...] * chunk_f + _dot(kc * k_dec_f, vc)
        o = o * lax.rsqrt(jnp.mean(o * o, axis=-1, keepdims=True) + EPS)
        o_ref[0, pl.ds(t0, c), :] = o.astype(BF) * g_ref[0, pl.ds(t0, c), :]
        return carry

    lax.fori_loop(0, n_chunks, fwd, 0, unroll=min(16, n_chunks))


def _retention(qvg, kt, log_gamma, batch):
    n, wq = qvg.shape
    t = n // batch
    assert t % RET_CHUNK == 0
    n_chunks = t // RET_CHUNK
    qvg3 = qvg.reshape(batch, t, wq)
    v_off = (RET_HEADS * RET_DK) // RET_DV
    g_off = v_off + RET_HEADS
    return pl.pallas_call(
        functools.partial(_ret_kernel, n_chunks=n_chunks),
        out_shape=jax.ShapeDtypeStruct((batch, t, RET_HEADS * RET_DV), BF),
        grid_spec=pltpu.PrefetchScalarGridSpec(
            num_scalar_prefetch=1,
            grid=(batch, RET_HEADS),
            in_specs=[pl.BlockSpec((1, t, RET_DK), lambda b, h, lg: (b, 0, h)),
                      pl.BlockSpec((1, RET_DK, t), lambda b, h, lg: (b, h, 0)),
                      pl.BlockSpec((1, t, RET_DV), lambda b, h, lg: (b, 0, v_off + h)),
                      pl.BlockSpec((1, t, RET_DV), lambda b, h, lg: (b, 0, g_off + h))],
            out_specs=pl.BlockSpec((1, t, RET_DV), lambda b, h, lg: (b, 0, h)),
            scratch_shapes=[pltpu.VMEM((n_chunks, RET_DK, RET_DV), BF),
                            pltpu.VMEM((RET_DK, RET_DV), F32), pltpu.VMEM((RET_DK, RET_DV), F32)]),
        compiler_params=_params(2, VMEM_LIMIT),
        name="retention",
    )(log_gamma, qvg3, kt, qvg3, qvg3)


OUT_ODD_ROWS = 512


def _out_odd_kernel(y_ref, x_ref, w_ref, *route_refs):
    o_ref = route_refs[N_ROUTE_IN]
    _route_init(route_refs[-1])
    for r0 in range(0, o_ref.shape[0], OUT_ODD_ROWS):
        rows = slice(r0, r0 + OUT_ODD_ROWS)
        x_new = x_ref[rows, :] + _dot(y_ref[rows, :], w_ref[...])
        o_ref[rows, :] = x_new
        _route(x_new, *route_refs[:N_ROUTE_IN], *route_refs[N_ROUTE_IN + 1:], row0=r0)


def _out_odd(y, x2, w, router, tm=2 * OUT_ODD_ROWS):
    n, dm = x2.shape
    kdim = y.shape[1]
    r_ops, r_in, r_shapes, r_out, r_scratch = _router_io(n, dm, tm, *router)
    return pl.pallas_call(
        _out_odd_kernel,
        out_shape=[jax.ShapeDtypeStruct((n, dm), F32)] + r_shapes,
        grid=(n // tm,),
        in_specs=[pl.BlockSpec((tm, kdim), lambda i: (i, 0)), pl.BlockSpec((tm, dm), lambda i: (i, 0)),
                  pl.BlockSpec(w.shape, lambda i: (0, 0))] + r_in,
        out_specs=[pl.BlockSpec((tm, dm), lambda i: (i, 0))] + r_out,
        scratch_shapes=r_scratch,
        compiler_params=_params(1, VMEM_LIMIT),
        name="out_odd",
    )(y, x2, w, *r_ops)


def _route_init(run_ref):
    @pl.when(pl.program_id(0) == 0)
    def _():
        run_ref[...] = jnp.zeros_like(run_ref)


def _route(x, g_ref, wc_ref, wh_ref, b_ref, tri_ref, o_ref, ot_ref, cnt_ref, run_ref, row0=0):
    h = _rms(x, g_ref[...])
    hh, hl = _split_bf16(h)
    both = _dot(hh, wc_ref[...])
    nl = wh_ref.shape[1]
    logits = both[:, 0:nl] + both[:, nl:] + _dot(hl, wh_ref[...]) + b_ref[...]
    sub = tri_ref.shape[0]
    tri = tri_ref[...]
    run = run_ref[...]
    for r0 in range(0, x.shape[0], sub):
        info, run = _route_rows(logits[r0:r0 + sub], tri, run)
        o_ref[row0 + r0:row0 + r0 + sub, :] = info
        ot_ref[:, row0 + r0:row0 + r0 + sub] = info.T[0:ROW_ALIGN, :]
    run_ref[...] = run
    cnt_ref[...] = run


def _route_rows(logits, tri, run):
    lane = lax.broadcasted_iota(jnp.int32, logits.shape, 1).astype(F32)
    neg = jnp.float32(-jnp.inf)

    def top(mask):
        val = jnp.max(jnp.where(mask, logits, neg), axis=-1, keepdims=True)
        idx = jnp.min(jnp.where(mask & (logits == val), lane, 1e6), axis=-1, keepdims=True)
        return val, idx

    gmask = lane < MOE_GROUPS
    gmax, gidx = top(gmask)
    gprob = 1.0 / jnp.sum(jnp.where(gmask, jnp.exp(logits - gmax), 0.0), axis=-1, keepdims=True)
    base = MOE_GROUPS + MOE_PER_GROUP * gidx
    emask = (lane >= base) & (lane < base + MOE_PER_GROUP)
    v1, i1 = top(emask)
    v2, i2 = top(emask & (lane != i1))
    e2 = jnp.exp(v2 - v1)
    w1 = gprob / (1.0 + e2)
    w2 = gprob * e2 / (1.0 + e2)
    a1 = i1 - base
    a2 = i2 - base
    lo = jnp.minimum(a1, a2)
    hi = jnp.maximum(a1, a2)
    pair = lo * (7.0 - lo) * 0.5 + hi - lo - 1.0
    cls = gidx * MOE_PAIRS + pair
    w_lo = jnp.where(a1 < a2, w1, w2)
    w_hi = jnp.where(a1 < a2, w2, w1)
    onehot = lane == cls
    earlier = _dot(tri, jnp.where(onehot, 1.0, 0.0).astype(BF)) + run
    rank = jnp.sum(jnp.where(onehot, earlier, 0.0), axis=-1, keepdims=True)
    info = jnp.where(lane == 0, cls, jnp.where(lane == 1, w_lo, jnp.where(lane == 2, w_hi,
                     jnp.where(lane == 3, rank, 0.0))))
    return info, run + jnp.sum(jnp.where(onehot, 1.0, 0.0), axis=0, keepdims=True)


ROUTE_ROWS = 128
N_ROUTE_IN = 5


def _router_io(n, dm, tm, gain, w_cat, w_hi, b_r):
    tri = (jnp.arange(ROUTE_ROWS)[:, None] > jnp.arange(ROUTE_ROWS)[None, :]).astype(BF)
    const = lambda shp: pl.BlockSpec(shp, lambda i: (0, 0))
    operands = [gain[None].astype(F32), w_cat, w_hi, b_r, tri]
    in_specs = [const((1, dm)), const(w_cat.shape), const(w_hi.shape), const((1, 128)), const(tri.shape)]
    out_shapes = [jax.ShapeDtypeStruct((n, 128), F32), jax.ShapeDtypeStruct((ROW_ALIGN, n), F32),
                  jax.ShapeDtypeStruct((1, 128), F32)]
    out_specs = [pl.BlockSpec((tm, 128), lambda i: (i, 0)), pl.BlockSpec((ROW_ALIGN, tm), lambda i: (0, i)),
                 const((1, 128))]
    return operands, in_specs, out_shapes, out_specs, [pltpu.VMEM((1, 128), F32)]


def _start_row_dmas(n, copy_of_row):
    for r in range(n):
        copy_of_row(r).start()


def _dispatch_kernel(fpos_ref, fon_ref, pos_ref, x_ref, info_ref, o_hbm, aug_ref, zero_ref, sem, row_sem):
    dm = x_ref.shape[-1]
    tm = info_ref.shape[0]

    @pl.when(pl.program_id(0) == 0)
    def _():
        zero_ref[...] = jnp.zeros_like(zero_ref)
        n_fill = fpos_ref.shape[0]

        def fill(k):
            return pltpu.make_async_copy(
                zero_ref, o_hbm.at[pl.ds(pl.multiple_of(fpos_ref[k], ROW_ALIGN), MOE_FILL), :], sem)

        for k in range(n_fill - 1):
            pl.when(fon_ref[k] > 0)(lambda k=k: fill(k).start())
        for k in range(n_fill - 1):
            pl.when(fon_ref[k] > 0)(lambda k=k: fill(k).wait())
        fill(n_fill - 1).start()
        fill(n_fill - 1).wait()

    i = pl.program_id(0)
    n_steps = pl.num_programs(0)
    slot = i % 2

    def wait_rows(s):
        pltpu.make_async_copy(aug_ref.at[s], o_hbm.at[pl.ds(0, tm), :], row_sem.at[s]).wait()

    pl.when(i >= 2)(lambda: wait_rows(slot))
    aug_ref[slot, :, 0:dm] = x_ref[...].reshape(tm, dm)
    aug_ref[slot, :, dm:] = info_ref[...]
    _start_row_dmas(tm, lambda r: pltpu.make_async_copy(
        aug_ref.at[slot, pl.ds(r, 1), :], o_hbm.at[pl.ds(pos_ref[0, 0, r], 1), :], row_sem.at[slot]))

    @pl.when(i == n_steps - 1)
    def _():
        pl.when(n_steps >= 2)(lambda: wait_rows(1 - slot))
        wait_rows(slot)


def _token_tiles(x3, nb, tq):
    if nb == x3.shape[0]:
        return (lambda *shape_tail: pl.BlockSpec((nb, tq) + shape_tail, lambda i, *_: (0, i) + (0,) * len(shape_tail)),
                x3.shape[1] // tq)
    assert nb == 1 and tq == x3.shape[1]
    return (lambda *shape_tail: pl.BlockSpec((1, tq) + shape_tail, lambda i, *_: (i, 0) + (0,) * len(shape_tail)),
            x3.shape[0])


def _dispatch(x3, nb, tq, info, pos, first_pad, n_rows):
    dm = x3.shape[-1]
    n = info.shape[0]
    tm = nb * tq
    wa = dm + info.shape[1]
    spec, n_steps = _token_tiles(x3, nb, tq)
    class_fill = first_pad // ROW_ALIGN * ROW_ALIGN
    tail = class_fill[-1] + MOE_FILL + jnp.arange((n_rows - n) // MOE_FILL + 1, dtype=jnp.int32) * MOE_FILL
    tail_on = tail + MOE_FILL <= n_rows
    fpos = jnp.concatenate([class_fill, jnp.where(tail_on, tail, 0), jnp.array([n_rows - MOE_FILL], jnp.int32)])
    fon = jnp.concatenate([jnp.ones_like(class_fill), tail_on.astype(jnp.int32), jnp.ones((1,), jnp.int32)])
    return pl.pallas_call(
        _dispatch_kernel,
        out_shape=jax.ShapeDtypeStruct((n_rows, wa), F32),
        grid_spec=pltpu.PrefetchScalarGridSpec(
            num_scalar_prefetch=2,
            grid=(n_steps,),
            in_specs=[pl.BlockSpec((1, 1, tm), lambda i, fp, fo: (i, 0, 0), memory_space=pltpu.SMEM),
                      spec(dm),
                      pl.BlockSpec((tm, info.shape[1]), lambda i, fp, fo: (i, 0))],
            out_specs=pl.BlockSpec(memory_space=pl.ANY),
            scratch_shapes=[pltpu.VMEM((2, tm, wa), F32), pltpu.VMEM((MOE_FILL, wa), F32),
                            pltpu.SemaphoreType.DMA, pltpu.SemaphoreType.DMA((2,))]),
        compiler_params=_params(1, VMEM_LIMIT),
        name="moe_dispatch",
    )(fpos.astype(jnp.int32), fon, pos.reshape(n_steps, 1, tm), x3, info)


EXPERT_TILES = 2


def _expert_kernel(ea_ref, eb_ref, act_ref, xa_ref, g_ref, *refs):
    i = pl.program_id(0)
    o_ref = refs[-1]
    dm = o_ref.shape[1]
    tm = MOE_TILE

    @pl.when(act_ref[i * EXPERT_TILES] > 0)
    def _():
        for t in range(EXPERT_TILES):
            wga, wua, wda, wgb, wub, wdb = refs[6 * t:6 * t + 6]
            rows = slice(t * tm, (t + 1) * tm)
            h = _rms(xa_ref[rows, 0:dm], g_ref[...]).astype(BF)

            def ffn(wg, wu, wd, wrow):
                gate = _dot(h, wg[0])
                hid = gate * _sigmoid(gate) * _dot(h, wu[0])
                return _dot((hid * wrow).astype(BF), wd[0])

            o_ref[rows, :] = (ffn(wga, wua, wda, xa_ref[rows, dm + 1:dm + 2])
                              + ffn(wgb, wub, wdb, xa_ref[rows, dm + 2:dm + 3]))

    @pl.when(act_ref[i * EXPERT_TILES] == 0)
    def _():
        o_ref[...] = jnp.zeros_like(o_ref)


def _experts(xa_sorted, gain, tile_ea, tile_eb, tile_act, w_gate, w_up, w_down):
    dm = w_gate.shape[1]
    wa = xa_sorted.shape[1]
    n_tiles = tile_ea.shape[0]
    assert n_tiles % EXPERT_TILES == 0
    tm = MOE_TILE * EXPERT_TILES
    ff = w_gate.shape[-1]

    def wsel(shp, which, t):
        return pl.BlockSpec((1,) + shp, lambda i, ea, eb, act: ((ea, eb)[which][i * EXPERT_TILES + t], 0, 0))

    w_specs, w_ops = [], []
    for t in range(EXPERT_TILES):
        for which in (0, 1):
            w_specs += [wsel((dm, ff), which, t), wsel((dm, ff), which, t), wsel((ff, dm), which, t)]
            w_ops += [w_gate, w_up, w_down]
    return pl.pallas_call(
        _expert_kernel,
        out_shape=jax.ShapeDtypeStruct((n_tiles * MOE_TILE, dm), F32),
        grid_spec=pltpu.PrefetchScalarGridSpec(
            num_scalar_prefetch=3,
            grid=(n_tiles // EXPERT_TILES,),
            in_specs=[pl.BlockSpec((tm, wa), lambda i, ea, eb, act: (i * act[i * EXPERT_TILES], 0)),
                      pl.BlockSpec((1, dm), lambda i, ea, eb, act: (0, 0))] + w_specs,
            out_specs=pl.BlockSpec((tm, dm), lambda i, ea, eb, act: (i, 0))),
        compiler_params=_params(1, VMEM_LIMIT),
        name="moe_experts",
    )(tile_ea, tile_eb, tile_act, xa_sorted, gain, *w_ops)


def _combine_kernel(pos_ref, nxt_ref, x_ref, y_hbm, o_ref, ybuf, sem):
    i = pl.program_id(0)
    n_steps = pl.num_programs(0)
    tm = ybuf.shape[1]
    slot = i % 2

    def gather(idx_ref, s):
        _start_row_dmas(tm, lambda r: pltpu.make_async_copy(
            y_hbm.at[pl.ds(idx_ref[0, 0, r], 1), :], ybuf.at[s, pl.ds(r, 1), :], sem.at[s]))

    @pl.when(i == 0)
    def _():
        gather(pos_ref, 0)

    @pl.when(i + 1 < n_steps)
    def _():
        gather(nxt_ref, 1 - slot)

    pltpu.make_async_copy(y_hbm.at[pl.ds(0, tm), :], ybuf.at[slot], sem.at[slot]).wait()
    o_ref[...] = x_ref[...] + ybuf[slot].reshape(x_ref.shape)


def _combine(x3, nb, tq, y_sorted, pos):
    dm = x3.shape[-1]
    tm = nb * tq
    spec, n_steps = _token_tiles(x3, nb, tq)
    pos3 = pos.reshape(n_steps, 1, tm)
    return pl.pallas_call(
        _combine_kernel,
        out_shape=jax.ShapeDtypeStruct(x3.shape, F32),
        grid=(n_steps,),
        in_specs=[pl.BlockSpec((1, 1, tm), lambda i: (i, 0, 0), memory_space=pltpu.SMEM),
                  pl.BlockSpec((1, 1, tm), lambda i: (jnp.minimum(i + 1, n_steps - 1), 0, 0), memory_space=pltpu.SMEM),
                  spec(dm),
                  pl.BlockSpec(memory_space=pl.ANY)],
        out_specs=spec(dm),
        scratch_shapes=[pltpu.VMEM((2, tm, dm), F32), pltpu.SemaphoreType.DMA((2,))],
        compiler_params=_params(1, VMEM_LIMIT),
        name="moe_combine",
    )(pos3, pos3, x3, y_sorted)


def _slots_kernel(starts_ref, rows_ref, o_ref):
    cls = rows_ref[0:1, :]
    base = jnp.zeros_like(cls)
    for c in range(MOE_CLASSES):
        base = jnp.where(cls == c, starts_ref[c].astype(F32), base)
    o_ref[...] = (base + rows_ref[3:4, :]).astype(jnp.int32)


def _slots(info_rows, starts, tn=8192):
    n = info_rows.shape[1]
    tn = min(tn, n)
    return pl.pallas_call(
        _slots_kernel,
        out_shape=jax.ShapeDtypeStruct((1, n), jnp.int32),
        grid_spec=pltpu.PrefetchScalarGridSpec(
            num_scalar_prefetch=1,
            grid=(n // tn,),
            in_specs=[pl.BlockSpec((ROW_ALIGN, tn), lambda i, s: (0, i))],
            out_specs=pl.BlockSpec((1, tn), lambda i, s: (0, i))),
        compiler_params=_params(1),
        name="moe_slots",
    )(starts.astype(jnp.int32), info_rows)


def _moe(x3, nb, tq, info, info_rows, cnt, gain, layer, w_gate, w_up, w_down):
    n = info.shape[0]
    tm = MOE_TILE
    gain = gain[None].astype(F32)
    counts = cnt[0, :MOE_CLASSES].astype(jnp.int32)
    padded = ((counts + tm - 1) // tm) * tm
    ends = jnp.cumsum(padded)
    starts = ends - padded
    pos = _slots(info_rows, starts)
    n_tiles = -(-(n // tm + MOE_CLASSES) // EXPERT_TILES) * EXPERT_TILES
    tile_start = jnp.arange(n_tiles, dtype=jnp.int32) * tm
    tile_cls = jnp.minimum(jnp.sum(tile_start[:, None] >= ends[None, :], axis=1), MOE_CLASSES - 1)
    tile_act = (tile_start < ends[-1]).astype(jnp.int32)
    pair_lo = jnp.array([0, 0, 0, 1, 1, 2], jnp.int32)
    pair_hi = jnp.array([1, 2, 3, 2, 3, 3], jnp.int32)
    first = layer * MOE_GROUPS * MOE_PER_GROUP + (tile_cls // MOE_PAIRS) * MOE_PER_GROUP
    tile_ea = (first + pair_lo[tile_cls % MOE_PAIRS]).astype(jnp.int32)
    tile_eb = (first + pair_hi[tile_cls % MOE_PAIRS]).astype(jnp.int32)
    xa_sorted = _dispatch(x3, nb, tq, info, pos, (starts + counts).astype(jnp.int32), (n_tiles + 2) * tm)
    y_sorted = _experts(xa_sorted, gain, tile_ea, tile_eb, tile_act, w_gate, w_up, w_down)
    return _combine(x3, nb, tq, y_sorted, pos)


def _router_weights(w_group, b_group, w_router, b_router):
    dm = w_group.shape[0]
    w = jnp.concatenate([w_group.astype(F32), w_router.astype(F32).reshape(dm, -1)], axis=1)
    b = jnp.concatenate([b_group.astype(F32), b_router.astype(F32).reshape(-1)])
    pad = 128 - w.shape[1]
    w_hi, w_lo = _split_bf16(jnp.pad(w, ((0, 0), (0, pad))))
    return jnp.concatenate([w_hi, w_lo], axis=1), w_hi, jnp.pad(b, (0, pad))[None]


def _trunk(x, p):
    b, t, dm = x.shape
    n = b * t

    tq = _even_tile(b)
    ug, q, k, v = _in_even(x, p["norm_mix"][0], p["w_in_even"], p["na_q_gain"], p["na_k_gain"])
    yg = _s5_scan(ug, *p["s5_ops"], p["s5_d_rows"], t // S5_CHUNK, b)
    att = _na(q, k, v, p["na_table"])
    x3, *routing = _out_even(yg, att, x, p["s5_w_glu"], p["s5_b_glu"], p["w_out_even"],
                             (p["norm_ffn"][0], *p["router"][0]))
    x3 = _moe(x3, b, tq, *routing, p["norm_ffn"][0], 0, *p["experts"])

    x2 = x3.reshape(n, dm)
    qvg, kt = _in_odd(x2, p["norm_mix"][1], p["w_in_odd"], p["w_k_t"], p["ret_norm_gain"], b)
    yr = _retention(qvg, kt, p["log_gamma"], b)
    tm = 512
    x2, *routing = _out_odd(yr.reshape(n, -1), x2, p["w_out_odd"], (p["norm_ffn"][1], *p["router"][1]))
    x3 = _moe(x2.reshape(n // tm, tm, dm), 1, tm, *routing, p["norm_ffn"][1], 1, *p["experts"])
    return x3.reshape(b, t, dm)


def kernel(x_prompt, x_sample, norm_mix, norm_ffn, w_in_even, w_out_even, s5_lambda_re, s5_lambda_im, s5_log_dt, s5_b_re, s5_b_im, s5_c_re, s5_c_im, s5_d, s5_w_glu, s5_b_glu, na_q_gain, na_k_gain, na_rpb, w_in_odd, w_out_odd, ret_decay_logit, ret_norm_gain, moe_w_group, moe_b_group, moe_w_router, moe_b_router, moe_w_gate, moe_w_up, moe_w_down):
    assert norm_mix.shape[0] == 2
    dk_all = RET_HEADS * RET_DK
    w_odd = w_in_odd[0].astype(BF)
    p = {
        "norm_mix": norm_mix, "norm_ffn": norm_ffn,
        "w_in_even": w_in_even[0].astype(BF), "w_out_even": w_out_even[0].astype(BF),
        "s5_ops": _s5_operators(s5_lambda_re[0], s5_lambda_im[0], s5_log_dt[0], s5_b_re[0], s5_b_im[0],
                                s5_c_re[0], s5_c_im[0]),
        "s5_d_rows": jnp.tile(s5_d[0].astype(F32).reshape(S5_GROUPS, 1, S5_GROUP), (1, 1, S5_CHUNK)),
        "s5_w_glu": s5_w_glu[0].astype(BF), "s5_b_glu": s5_b_glu[0],
        "na_q_gain": na_q_gain[0], "na_k_gain": na_k_gain[0], "na_table": _na_bias_table(na_rpb[0]),
        "w_in_odd": w_odd, "w_k_t": w_in_odd[0][:, dk_all:2 * dk_all].T.astype(BF),
        "w_out_odd": w_out_odd[0].astype(BF),
        "log_gamma": jax.nn.log_sigmoid(ret_decay_logit[0].astype(F32)),
        "ret_norm_gain": ret_norm_gain[0],
        "router": [_router_weights(moe_w_group[l], moe_b_group[l], moe_w_router[l], moe_b_router[l]) for l in range(2)],
        "experts": tuple(w.astype(BF).reshape((-1,) + w.shape[2:]) for w in (moe_w_gate, moe_w_up, moe_w_down)),
    }
    return _trunk(x_prompt, p), _trunk(x_sample, p)
```

```python
import functools
import math

import jax
import jax.numpy as jnp
from jax import lax
from jax.experimental import pallas as pl
from jax.experimental.pallas import tpu as pltpu

F32 = jnp.float32
BF = jnp.bfloat16
EPS = 1e-6

D_MODEL = 1024
GRID_W = 64
S5_GROUPS = 32
S5_GROUP = 16
S5_STATE = 64
S5_CHUNK = 16
NA_HEADS = 8
NA_HEAD_DIM = 64
NA_WIN_R = 8
NA_WIN_C = 16
NA_HEAD_GROUP = 4
RET_HEADS = 8
RET_DK = 128
RET_DV = 256
RET_CHUNK = 256
ROPE_BASE = 10000.0
MOE_GROUPS = 4
MOE_PER_GROUP = 4
MOE_FF = 512
MOE_PAIRS = 6
MOE_CLASSES = MOE_GROUPS * MOE_PAIRS
MOE_TILE = 256
ROW_ALIGN = 8
MOE_FILL = MOE_TILE + ROW_ALIGN
MASK_NEG = -1e30
VMEM_LIMIT = 56 * 1024 * 1024


def _params(n_axes, vmem=None):
    return pltpu.CompilerParams(dimension_semantics=("arbitrary",) * n_axes,
                                vmem_limit_bytes=vmem)


def _rms(x, gain):
    ms = jnp.mean(x * x, axis=-1, keepdims=True)
    return x * lax.rsqrt(ms + EPS) * gain


def _sigmoid(x):
    return 1.0 / (1.0 + jnp.exp(-x))


def _dot(a, b):
    return jnp.dot(a, b, preferred_element_type=F32)


def _dot_nt(a, b):
    return lax.dot_general(a, b, (((1,), (1,)), ((), ())), preferred_element_type=F32)


def _split_bf16(x):
    hi = x.astype(BF)
    lo = (x - hi.astype(F32)).astype(BF)
    return hi, lo


def _dot3(a, b):
    ah, al = _split_bf16(a)
    bh, bl = _split_bf16(b)
    return _dot(ah, bh) + _dot(ah, bl) + _dot(al, bh)


EVEN_TILES = 2


def _chunk_major_perm(batch, tq):
    nc = tq // S5_CHUNK
    r = jnp.arange(batch * tq)
    j, c, b = r // (nc * batch), (r // batch) % nc, r % batch
    src = b * tq + c * S5_CHUNK + j
    return (src[:, None] == jnp.arange(batch * tq)[None, :]).astype(BF)


def _in_even_kernel(x_ref, g_ref, w_ref, qg_ref, kg_ref, seg_ref, perm_ref, ug_ref, q_ref, k_ref, v_ref):
    batch, t_step, dm = x_ref.shape
    tq = t_step // EVEN_TILES
    d = q_ref.shape[-1]
    nr = ug_ref.shape[1] // EVEN_TILES
    seg = seg_ref[...]

    def head_norm(z, gain):
        ssq = _dot((z * z).astype(BF), seg) * (1.0 / NA_HEAD_DIM)
        return z * lax.rsqrt(ssq + EPS) * gain

    for s in range(EVEN_TILES):
        toks = slice(s * tq, (s + 1) * tq)
        h = _rms(x_ref[:, toks, :].reshape(batch * tq, dm), g_ref[...]).astype(BF)
        up = _dot(perm_ref[...], _dot(h, w_ref[:, 0:d]).astype(BF))
        for g in range(S5_GROUPS):
            lanes = slice(g * S5_GROUP, (g + 1) * S5_GROUP)
            ug_ref[g, s * nr:(s + 1) * nr, :] = jnp.concatenate(
                [up[j * nr:(j + 1) * nr, lanes] for j in range(S5_CHUNK)], axis=1).astype(BF)
        q = _dot(h, w_ref[:, d:2 * d])
        q_ref[:, toks, :] = (head_norm(q, qg_ref[...]) * (NA_HEAD_DIM ** -0.5)).astype(BF).reshape(batch, tq, d)
        k = _dot(h, w_ref[:, 2 * d:3 * d])
        k_ref[:, toks, :] = head_norm(k, kg_ref[...]).astype(BF).reshape(batch, tq, d)
        v_ref[:, toks, :] = _dot(h, w_ref[:, 3 * d:4 * d]).astype(BF).reshape(batch, tq, d)


def _even_tile(batch, tm=512):
    tq = tm // batch
    assert tq * batch == tm and tq % S5_CHUNK == 0
    return tq


def _in_even(x3, gain, w, q_gain, k_gain):
    batch, t, dm = x3.shape
    d = w.shape[1] // 4
    tq = _even_tile(batch)
    t_step = EVEN_TILES * tq
    assert t % t_step == 0
    nr = (t_step // S5_CHUNK) * batch
    seg = (jnp.arange(d)[:, None] // NA_HEAD_DIM == jnp.arange(d)[None, :] // NA_HEAD_DIM).astype(BF)
    qg = jnp.tile(q_gain.astype(F32), NA_HEADS)[None]
    kg = jnp.tile(k_gain.astype(F32), NA_HEADS)[None]
    tok = lambda width: pl.BlockSpec((batch, t_step, width), lambda i: (0, i, 0))
    full = lambda shp: pl.BlockSpec(shp, lambda i: (0,) * len(shp))
    gw = S5_CHUNK * S5_GROUP
    perm = _chunk_major_perm(batch, tq)
    return pl.pallas_call(
        _in_even_kernel,
        out_shape=[jax.ShapeDtypeStruct((S5_GROUPS, (t // S5_CHUNK) * batch, gw), BF)]
        + [jax.ShapeDtypeStruct((batch, t, d), BF)] * 3,
        grid=(t // t_step,),
        in_specs=[tok(dm), full((1, dm)), full(w.shape), full((1, d)), full((1, d)), full((d, d)), full(perm.shape)],
        out_specs=[pl.BlockSpec((S5_GROUPS, nr, gw), lambda i: (0, i, 0)), tok(d), tok(d), tok(d)],
        compiler_params=_params(1, VMEM_LIMIT),
        name="in_even",
    )(x3, gain[None].astype(F32), w, qg, kg, seg, perm)


def _s5_taps_kernel(l_ref, r_ref, o_ref):
    for g in range(l_ref.shape[0]):
        o_ref[g] = _dot3(l_ref[g], r_ref[g])


def _s5_taps(lhs, rhs, per_step=8):
    g2, m, k = lhs.shape
    nn = rhs.shape[-1]
    return pl.pallas_call(
        _s5_taps_kernel,
        out_shape=jax.ShapeDtypeStruct((g2, m, nn), F32),
        grid=(g2 // per_step,),
        in_specs=[pl.BlockSpec((per_step, m, k), lambda i: (i, 0, 0)), pl.BlockSpec((per_step, k, nn), lambda i: (i, 0, 0))],
        out_specs=pl.BlockSpec((per_step, m, nn), lambda i: (i, 0, 0)),
        compiler_params=_params(1),
        name="s5_taps",
    )(lhs, rhs)


def _s5_operators(lam_re, lam_im, log_dt, b_re, b_im, c_re, c_im):
    L, G, H, P_ = S5_CHUNK, S5_GROUPS, S5_GROUP, S5_STATE
    lr, li = lam_re.astype(F32), lam_im.astype(F32)
    dt = jnp.exp(log_dt.astype(F32))[..., None]
    kk = jnp.arange(L + 1, dtype=F32)[:, None, None, None]
    mag = jnp.exp(kk * (lr * dt)[None])
    ang = kk * (li * dt)[None]
    pw = (mag * jnp.cos(ang), mag * jnp.sin(ang))

    def cmul(x, y):
        return x[0] * y[0] - x[1] * y[1], x[0] * y[1] + x[1] * y[0]

    xr, xi = pw[0][1] - 1.0, pw[1][1]
    den = lr * lr + li * li
    coef = ((xr * lr + xi * li) / den, (xi * lr - xr * li) / den)
    bbar = cmul((coef[0][..., None], coef[1][..., None]), (b_re.astype(F32), b_im.astype(F32)))
    c = (c_re.astype(F32), c_im.astype(F32))

    def at(k_slice, direction, perm):
        return tuple(jnp.transpose(z[k_slice, direction], perm) for z in pw)

    apl = tuple(jnp.transpose(z[:L], (1, 2, 0, 3))[:, :, :, None, :] for z in pw)
    ce = cmul((c[0][:, :, None], c[1][:, :, None]), apl)
    lhs = jnp.concatenate([ce[0], -ce[1]], axis=-1).reshape(2 * G, L * H, 2 * P_)
    rhs = jnp.concatenate([bbar[0], bbar[1]], axis=2).reshape(2 * G, 2 * P_, H)
    taps = _s5_taps(lhs, rhs).reshape(2, G, L, H, H)
    taps = jnp.transpose(taps, (0, 1, 2, 4, 3))
    kf, kb = taps[0], taps[1]
    lagged = jnp.concatenate([jnp.flip(kb[:, 1:], axis=1), (kf[:, 0] + kb[:, 0])[:, None], kf[:, 1:]], axis=1)
    idx = jnp.arange(L)[None, :] - jnp.arange(L)[:, None] + (L - 1)
    m = jnp.transpose(lagged[:, idx], (0, 1, 3, 2, 4)).reshape(G, L * H, L * H)

    bt = tuple(jnp.transpose(z, (0, 1, 3, 2)) for z in bbar)
    expand_j = lambda z: z[:, :, None, :]
    pf = cmul(tuple(map(expand_j, at(slice(L - 1, None, -1), 0, (1, 0, 2)))), (bt[0][0][:, None], bt[1][0][:, None]))
    pb = cmul(tuple(map(expand_j, at(slice(0, L), 1, (1, 0, 2)))), (bt[0][1][:, None], bt[1][1][:, None]))
    p = jnp.concatenate([pf[0], pb[0], pf[1], pb[1]], axis=-1).reshape(G, L * H, 4 * P_)

    ct = tuple(jnp.transpose(z, (0, 1, 3, 2)) for z in c)
    expand_h = lambda z: z[..., None]
    gf = cmul(tuple(map(expand_h, at(slice(1, None), 0, (1, 2, 0)))), (ct[0][0][:, :, None, :], ct[1][0][:, :, None, :]))
    gb = cmul(tuple(map(expand_h, at(slice(L, 0, -1), 1, (1, 2, 0)))), (ct[0][1][:, :, None, :], ct[1][1][:, :, None, :]))
    q = jnp.concatenate([gf[0], gb[0], -gf[1], -gb[1]], axis=1).reshape(G, 4 * P_, L * H)

    a_mat = jnp.stack([jnp.concatenate([pw[0][L, 0], pw[0][L, 1]], -1),
                       jnp.concatenate([pw[1][L, 0], pw[1][L, 1]], -1)], axis=1)
    return m.astype(BF), p.astype(BF), q.astype(BF), a_mat.astype(F32)


def _s5_kernel(u_ref, m_ref, p_ref, q_ref, a_ref, d_ref, y_ref, z_ref, s_ref, *, n_chunks, batch):
    u = u_ref[0]
    z_ref[...] = _dot(u, p_ref[0])
    half = 2 * S5_STATE
    ar = a_ref[0, 0:1, :]
    ai = a_ref[0, 1:2, :]
    fwd_lane = lax.broadcasted_iota(jnp.int32, (batch, half), 1) < S5_STATE

    def step(k, carry):
        sr, si = carry
        f0 = pl.multiple_of(k * batch, batch)
        b0 = pl.multiple_of((n_chunks - 1 - k) * batch, batch)
        s_ref[pl.ds(f0, batch), 0:S5_STATE] = sr[:, 0:S5_STATE]
        s_ref[pl.ds(b0, batch), S5_STATE:half] = sr[:, S5_STATE:half]
        s_ref[pl.ds(f0, batch), half:half + S5_STATE] = si[:, 0:S5_STATE]
        s_ref[pl.ds(b0, batch), half + S5_STATE:2 * half] = si[:, S5_STATE:half]
        zr = jnp.where(fwd_lane, z_ref[pl.ds(f0, batch), 0:half], z_ref[pl.ds(b0, batch), 0:half])
        zi = jnp.where(fwd_lane, z_ref[pl.ds(f0, batch), half:2 * half], z_ref[pl.ds(b0, batch), half:2 * half])
        return ar * sr - ai * si + zr, ar * si + ai * sr + zi

    zero = jnp.zeros((batch, half), F32)
    lax.fori_loop(0, n_chunks, step, (zero, zero), unroll=8)
    y = _dot(u, m_ref[0]) + _dot(s_ref[...].astype(BF), q_ref[0]) + u.astype(F32) * d_ref[0]
    y_ref[0] = y.astype(BF)


def _s5_scan(ug, m, p, q, a_mat, d_skip, n_chunks, batch):
    g, r, w = ug.shape
    blk = lambda shp: pl.BlockSpec((1,) + shp, lambda i: (i, 0, 0))
    return pl.pallas_call(
        functools.partial(_s5_kernel, n_chunks=n_chunks, batch=batch),
        out_shape=jax.ShapeDtypeStruct((g, r, w), BF),
        grid=(g,),
        in_specs=[blk((r, w)), blk((w, w)), blk((w, w)), blk((w, w)), blk((2, w // 2)), blk((1, w))],
        out_specs=blk((r, w)),
        scratch_shapes=[pltpu.VMEM((r, w), F32), pltpu.VMEM((r, w), F32)],
        compiler_params=_params(1, VMEM_LIMIT),
        name="s5_scan",
    )(ug, m, p, q, a_mat, d_skip)


def _na_bias_table(rpb):
    qc = jnp.arange(GRID_W)[:, None]
    kc = jnp.arange(GRID_W)[None, :]
    ws = jnp.clip(qc - NA_WIN_C // 2, 0, GRID_W - NA_WIN_C)
    valid = (kc >= ws) & (kc < ws + NA_WIN_C)
    n_tap = 2 * NA_WIN_C - 1
    pad = GRID_W - NA_WIN_C
    row = n_tap + 2 * pad
    taps = jnp.pad(rpb.astype(F32), ((0, 0), (0, 0), (pad, pad)))
    skew = jnp.tile(taps, (1, 1, GRID_W))[:, :, GRID_W - 1:GRID_W - 1 + GRID_W * (row - 1)]
    bias = skew.reshape(rpb.shape[0], rpb.shape[1], GRID_W, row - 1)[..., :GRID_W]
    e = jnp.where(valid[None, None], bias, MASK_NEG)
    return jnp.concatenate([e[:, :-1], e[:, 1:]], axis=-1)


def _na_kernel(q_ref, k_ref, v_ref, e_ref, o_ref, *, rows):
    hg_w = NA_HEAD_GROUP * NA_HEAD_DIM
    lane_head = lax.broadcasted_iota(jnp.int32, (GRID_W, hg_w), 1) // NA_HEAD_DIM
    n_keys = NA_WIN_R * GRID_W

    def row_body(r, carry):
        rs = jnp.clip(r - NA_WIN_R // 2, 0, rows - NA_WIN_R)
        di = rs - r + (NA_WIN_R - 1)
        q0 = pl.multiple_of(r * GRID_W, GRID_W)
        k0 = pl.multiple_of(rs * GRID_W, GRID_W)
        for hg in range(NA_HEADS // NA_HEAD_GROUP):
            lanes = slice(hg * hg_w, (hg + 1) * hg_w)
            q4 = q_ref[0, pl.ds(q0, GRID_W), lanes]
            qs = jnp.concatenate([jnp.where(lane_head == hh, q4, jnp.zeros_like(q4))
                                  for hh in range(NA_HEAD_GROUP)], axis=0)
            k4 = k_ref[0, pl.ds(k0, n_keys), lanes]
            v4 = v_ref[0, pl.ds(k0, n_keys), lanes]
            s = _dot_nt(qs, k4)
            bias = jnp.concatenate(
                [jnp.concatenate([e_ref[hg * NA_HEAD_GROUP + hh, di + 2 * w2] for w2 in range(NA_WIN_R // 2)], axis=1)
                 for hh in range(NA_HEAD_GROUP)], axis=0)
            s = s + bias
            mx = jnp.max(s, axis=-1, keepdims=True)
            p = jnp.exp(s - mx)
            den = jnp.sum(p, axis=-1, keepdims=True)
            o = _dot(p.astype(BF), v4) * (1.0 / den)
            out = jnp.zeros((GRID_W, hg_w), F32)
            for hh in range(NA_HEAD_GROUP):
                out = jnp.where(lane_head == hh, o[hh * GRID_W:(hh + 1) * GRID_W], out)
            o_ref[0, pl.ds(q0, GRID_W), lanes] = out.astype(BF)
        return carry

    lax.fori_loop(0, rows, row_body, 0, unroll=16)


def _na(q, k, v, table):
    b, t, d = q.shape
    rows = t // GRID_W
    assert rows >= NA_WIN_R and t % GRID_W == 0
    seq = pl.BlockSpec((1, t, d), lambda i: (i, 0, 0))
    return pl.pallas_call(
        functools.partial(_na_kernel, rows=rows),
        out_shape=jax.ShapeDtypeStruct((b, t, d), BF),
        grid=(b,),
        in_specs=[seq, seq, seq, pl.BlockSpec(table.shape, lambda i: (0, 0, 0, 0))],
        out_specs=seq,
        compiler_params=_params(1, VMEM_LIMIT),
        name="na",
    )(q, k, v, table)


def _out_even_kernel(yg_ref, a_ref, x_ref, wg_ref, bg_ref, wo_ref, perm_ref, *route_refs):
    o_ref = route_refs[N_ROUTE_IN]
    batch, t_step, dm = x_ref.shape
    tq = t_step // EVEN_TILES
    nr = yg_ref.shape[1] // EVEN_TILES
    da = a_ref.shape[-1]
    c0 = math.sqrt(2.0 / math.pi)
    _route_init(route_refs[-1])
    for s in range(EVEN_TILES):
        toks = slice(s * tq, (s + 1) * tq)
        rows = [jnp.concatenate([yg_ref[g, s * nr:(s + 1) * nr, i * S5_GROUP:(i + 1) * S5_GROUP]
                                 for g in range(S5_GROUPS)], axis=1) for i in range(S5_CHUNK)]
        ys = _dot(perm_ref[...], jnp.concatenate(rows, axis=0))
        ys = 0.5 * ys * (1.0 + jnp.tanh(c0 * (ys + 0.044715 * (ys * ys * ys))))
        gate = _sigmoid(_dot(ys.astype(BF), wg_ref[...]) + bg_ref[...])
        a_out = (ys * gate).astype(BF)
        att = a_ref[:, toks, :].reshape(batch * tq, da)
        x_new = (x_ref[:, toks, :].reshape(batch * tq, dm) + _dot(a_out, wo_ref[0:da, :])
                 + _dot(att, wo_ref[da:, :]))
        o_ref[:, toks, :] = x_new.reshape(batch, tq, dm)
        _route(x_new, *route_refs[:N_ROUTE_IN], *route_refs[N_ROUTE_IN + 1:], row0=s * batch * tq)


def _out_even(yg, att, x3, w_glu, b_glu, w_out, router):
    batch, t, dm = x3.shape
    d = att.shape[-1]
    groups, _, gw = yg.shape
    tq = _even_tile(batch)
    t_step = EVEN_TILES * tq
    assert t % t_step == 0
    nr = (t_step // S5_CHUNK) * batch
    tok = lambda width: pl.BlockSpec((batch, t_step, width), lambda i: (0, i, 0))
    full = lambda shp: pl.BlockSpec(shp, lambda i: (0,) * len(shp))
    r_ops, r_in, r_shapes, r_out, r_scratch = _router_io(batch * t, dm, batch * t_step, *router)
    perm = _chunk_major_perm(batch, tq).T
    return pl.pallas_call(
        _out_even_kernel,
        out_shape=[jax.ShapeDtypeStruct((batch, t, dm), F32)] + r_shapes,
        grid=(t // t_step,),
        in_specs=[pl.BlockSpec((groups, nr, gw), lambda i: (0, i, 0)), tok(d), tok(dm),
                  full((d, d)), full((1, d)), full(w_out.shape), full(perm.shape)] + r_in,
        out_specs=[tok(dm)] + r_out,
        scratch_shapes=r_scratch,
        compiler_params=_params(1, VMEM_LIMIT),
        name="out_even",
    )(yg, att, x3, w_glu, b_glu[None].astype(F32), w_out, perm, *r_ops)


def _in_odd_kernel(x_ref, g_ref, w_ref, wkt_ref, cq_ref, sq_ref, ck_ref, sk_ref, ng_ref, o_ref, kt_ref):
    h = _rms(x_ref[...], g_ref[...]).astype(BF)
    dk_all = RET_HEADS * RET_DK
    half = RET_DK // 2
    q = _dot(h, w_ref[:, 0:dk_all])
    cq, sq = cq_ref[...], sq_ref[...]
    for hd in range(RET_HEADS):
        lanes = slice(hd * RET_DK, (hd + 1) * RET_DK)
        qh = q[:, lanes]
        o_ref[:, lanes] = (qh * cq + pltpu.roll(qh, half, 1) * sq).astype(BF)
    kt = _dot_nt(wkt_ref[...], h)
    ck, sk = ck_ref[...], sk_ref[...]
    scale = RET_DK ** -0.5
    for hd in range(RET_HEADS):
        k1 = kt[hd * RET_DK:hd * RET_DK + half, :]
        k2 = kt[hd * RET_DK + half:(hd + 1) * RET_DK, :]
        kt_ref[0, hd * RET_DK:hd * RET_DK + half, :] = ((k1 * ck - k2 * sk) * scale).astype(BF)
        kt_ref[0, hd * RET_DK + half:(hd + 1) * RET_DK, :] = ((k1 * sk + k2 * ck) * scale).astype(BF)
    g0 = dk_all + RET_HEADS * RET_DV
    for c0 in range(dk_all, o_ref.shape[1], dk_all):
        z = _dot(h, w_ref[:, c0 + dk_all:c0 + 2 * dk_all])
        if c0 >= g0:
            z = z * _sigmoid(z) * ng_ref[:, c0 - g0:c0 - g0 + dk_all]
        o_ref[:, c0:c0 + dk_all] = z.astype(BF)


def _in_odd(x2, gain, w, wkt, norm_gain, batch, tm=512):
    n, dm = x2.shape
    t = n // batch
    nt = t // tm
    dk_all = RET_HEADS * RET_DK
    half = RET_DK // 2
    inv = ROPE_BASE ** (-jnp.arange(half, dtype=F32) / half)
    ang = jnp.arange(t, dtype=F32)[:, None] * inv[None, :]
    cos, sin = jnp.cos(ang), jnp.sin(ang)
    cq = jnp.concatenate([cos, cos], axis=1)
    sq = jnp.concatenate([-sin, sin], axis=1)
    const = lambda shp: pl.BlockSpec(shp, lambda i: (0,) * len(shp))
    return pl.pallas_call(
        _in_odd_kernel,
        out_shape=[jax.ShapeDtypeStruct((n, w.shape[1] - dk_all), BF), jax.ShapeDtypeStruct((batch, dk_all, t), BF)],
        grid=(n // tm,),
        in_specs=[pl.BlockSpec((tm, dm), lambda i: (i, 0)), const((1, dm)), const(w.shape), const(wkt.shape),
                  pl.BlockSpec((tm, RET_DK), lambda i: (i % nt, 0)), pl.BlockSpec((tm, RET_DK), lambda i: (i % nt, 0)),
                  pl.BlockSpec((half, tm), lambda i: (0, i % nt)), pl.BlockSpec((half, tm), lambda i: (0, i % nt)),
                  const((1, norm_gain.shape[0]))],
        out_specs=[pl.BlockSpec((tm, w.shape[1] - dk_all), lambda i: (i, 0)),
                   pl.BlockSpec((1, dk_all, tm), lambda i: (i // nt, 0, i % nt))],
        compiler_params=_params(1, VMEM_LIMIT),
        name="in_odd",
    )(x2, gain[None].astype(F32), w, wkt, cq, sq, cos.T, sin.T, norm_gain[None].astype(F32))


def _ret_kernel(lg_ref, q_ref, kt_ref, v_ref, g_ref, o_ref, sb_ref, sf_ref, sbc_ref, *, n_chunks):
    c = RET_CHUNK
    hd = pl.program_id(1)
    lgf = lg_ref[0, hd]
    lgb = lg_ref[1, hd]

    ii = lax.broadcasted_iota(jnp.int32, (c, c), 0)
    jj = lax.broadcasted_iota(jnp.int32, (c, c), 1)
    dist = (ii - jj).astype(F32)
    decay = jnp.where(ii >= jj, jnp.exp(lgf * jnp.maximum(dist, 0.0)),
                      jnp.exp(lgb * jnp.maximum(-dist, 0.0))).astype(BF)
    tok_col = lax.broadcasted_iota(jnp.int32, (c, RET_DK), 0).astype(F32)
    tok_row = lax.broadcasted_iota(jnp.int32, (RET_DK, c), 1).astype(F32)
    q_dec_f = jnp.exp(lgf * (tok_col + 1.0)).astype(BF)
    q_dec_b = jnp.exp(lgb * (c - tok_col)).astype(BF)
    k_dec_f = jnp.exp(lgf * (c - 1.0 - tok_row)).astype(BF)
    k_dec_b = jnp.exp(lgb * tok_row).astype(BF)
    chunk_f = jnp.exp(lgf * c)
    chunk_b = jnp.exp(lgb * c)

    sbc_ref[...] = jnp.zeros_like(sbc_ref)

    def back(i, carry):
        ci = n_chunks - 1 - i
        t0 = pl.multiple_of(ci * c, c)
        sb_ref[ci] = sbc_ref[...].astype(BF)
        kd = kt_ref[0, :, pl.ds(t0, c)] * k_dec_b
        sbc_ref[...] = sbc_ref[...] * chunk_b + _dot(kd, v_ref[0, pl.ds(t0, c), :])
        return carry

    lax.fori_loop(0, n_chunks, back, 0, unroll=min(16, n_chunks))

    sf_ref[...] = jnp.zeros_like(sf_ref)

    def fwd(ci, carry):
        t0 = pl.multiple_of(ci * c, c)
        qc = q_ref[0, pl.ds(t0, c), :]
        kc = kt_ref[0, :, pl.ds(t0, c)]
        vc = v_ref[0, pl.ds(t0, c), :]
        s = _dot(qc, kc).astype(BF) * decay
        qd = jnp.concatenate([qc * q_dec_f, qc * q_dec_b], axis=1)
        st = jnp.concatenate([sf_ref[...].astype(BF), sb_ref[ci]], axis=0)
        o = _dot(s, vc) + _dot(qd, st)
        sf_ref[...] = sf_ref[...] * chunk_f + _dot(kc * k_dec_f, vc)
        o = o * lax.rsqrt(jnp.mean(o * o, axis=-1, keepdims=True) + EPS)
        o_ref[0, pl.ds(t0, c), :] = o.astype(BF) * g_ref[0, pl.ds(t0, c), :]
        return carry

    lax.fori_loop(0, n_chunks, fwd, 0, unroll=min(16, n_chunks))


def _retention(qvg, kt, log_gamma, batch):
    n, wq = qvg.shape
    t = n // batch
    assert t % RET_CHUNK == 0
    n_chunks = t // RET_CHUNK
    qvg3 = qvg.reshape(batch, t, wq)
    v_off = (RET_HEADS * RET_DK) // RET_DV
    g_off = v_off + RET_HEADS
    return pl.pallas_call(
        functools.partial(_ret_kernel, n_chunks=n_chunks),
        out_shape=jax.ShapeDtypeStruct((batch, t, RET_HEADS * RET_DV), BF),
        grid_spec=pltpu.PrefetchScalarGridSpec(
            num_scalar_prefetch=1,
            grid=(batch, RET_HEADS),
            in_specs=[pl.BlockSpec((1, t, RET_DK), lambda b, h, lg: (b, 0, h)),
                      pl.BlockSpec((1, RET_DK, t), lambda b, h, lg: (b, h, 0)),
                      pl.BlockSpec((1, t, RET_DV), lambda b, h, lg: (b, 0, v_off + h)),
                      pl.BlockSpec((1, t, RET_DV), lambda b, h, lg: (b, 0, g_off + h))],
            out_specs=pl.BlockSpec((1, t, RET_DV), lambda b, h, lg: (b, 0, h)),
            scratch_shapes=[pltpu.VMEM((n_chunks, RET_DK, RET_DV), BF),
                            pltpu.VMEM((RET_DK, RET_DV), F32), pltpu.VMEM((RET_DK, RET_DV), F32)]),
        compiler_params=_params(2, VMEM_LIMIT),
        name="retention",
    )(log_gamma, qvg3, kt, qvg3, qvg3)


OUT_ODD_ROWS = 512


def _out_odd_kernel(y_ref, x_ref, w_ref, *route_refs):
    o_ref = route_refs[N_ROUTE_IN]
    _route_init(route_refs[-1])
    for r0 in range(0, o_ref.shape[0], OUT_ODD_ROWS):
        rows = slice(r0, r0 + OUT_ODD_ROWS)
        x_new = x_ref[rows, :] + _dot(y_ref[rows, :], w_ref[...])
        o_ref[rows, :] = x_new
        _route(x_new, *route_refs[:N_ROUTE_IN], *route_refs[N_ROUTE_IN + 1:], row0=r0)


def _out_odd(y, x2, w, router, tm=2 * OUT_ODD_ROWS):
    n, dm = x2.shape
    kdim = y.shape[1]
    r_ops, r_in, r_shapes, r_out, r_scratch = _router_io(n, dm, tm, *router)
    return pl.pallas_call(
        _out_odd_kernel,
        out_shape=[jax.ShapeDtypeStruct((n, dm), F32)] + r_shapes,
        grid=(n // tm,),
        in_specs=[pl.BlockSpec((tm, kdim), lambda i: (i, 0)), pl.BlockSpec((tm, dm), lambda i: (i, 0)),
                  pl.BlockSpec(w.shape, lambda i: (0, 0))] + r_in,
        out_specs=[pl.BlockSpec((tm, dm), lambda i: (i, 0))] + r_out,
        scratch_shapes=r_scratch,
        compiler_params=_params(1, VMEM_LIMIT),
        name="out_odd",
    )(y, x2, w, *r_ops)


def _route_init(run_ref):
    @pl.when(pl.program_id(0) == 0)
    def _():
        run_ref[...] = jnp.zeros_like(run_ref)


def _route(x, g_ref, wc_ref, wh_ref, b_ref, tri_ref, o_ref, ot_ref, cnt_ref, run_ref, row0=0):
    h = _rms(x, g_ref[...])
    hh, hl = _split_bf16(h)
    both = _dot(hh, wc_ref[...])
    nl = wh_ref.shape[1]
    logits = both[:, 0:nl] + both[:, nl:] + _dot(hl, wh_ref[...]) + b_ref[...]
    sub = tri_ref.shape[0]
    tri = tri_ref[...]
    run = run_ref[...]
    for r0 in range(0, x.shape[0], sub):
        info, run = _route_rows(logits[r0:r0 + sub], tri, run)
        o_ref[row0 + r0:row0 + r0 + sub, :] = info
        ot_ref[:, row0 + r0:row0 + r0 + sub] = info.T[0:ROW_ALIGN, :]
    run_ref[...] = run
    cnt_ref[...] = run


def _route_rows(logits, tri, run):
    lane = lax.broadcasted_iota(jnp.int32, logits.shape, 1).astype(F32)
    neg = jnp.float32(-jnp.inf)

    def top(mask):
        val = jnp.max(jnp.where(mask, logits, neg), axis=-1, keepdims=True)
        idx = jnp.min(jnp.where(mask & (logits == val), lane, 1e6), axis=-1, keepdims=True)
        return val, idx

    gmask = lane < MOE_GROUPS
    gmax, gidx = top(gmask)
    gprob = 1.0 / jnp.sum(jnp.where(gmask, jnp.exp(logits - gmax), 0.0), axis=-1, keepdims=True)
    base = MOE_GROUPS + MOE_PER_GROUP * gidx
    emask = (lane >= base) & (lane < base + MOE_PER_GROUP)
    v1, i1 = top(emask)
    v2, i2 = top(emask & (lane != i1))
    e2 = jnp.exp(v2 - v1)
    w1 = gprob / (1.0 + e2)
    w2 = gprob * e2 / (1.0 + e2)
    a1 = i1 - base
    a2 = i2 - base
    lo = jnp.minimum(a1, a2)
    hi = jnp.maximum(a1, a2)
    pair = lo * (7.0 - lo) * 0.5 + hi - lo - 1.0
    cls = gidx * MOE_PAIRS + pair
    w_lo = jnp.where(a1 < a2, w1, w2)
    w_hi = jnp.where(a1 < a2, w2, w1)
    onehot = lane == cls
    earlier = _dot(tri, jnp.where(onehot, 1.0, 0.0).astype(BF)) + run
    rank = jnp.sum(jnp.where(onehot, earlier, 0.0), axis=-1, keepdims=True)
    info = jnp.where(lane == 0, cls, jnp.where(lane == 1, w_lo, jnp.where(lane == 2, w_hi,
                     jnp.where(lane == 3, rank, 0.0))))
    return info, run + jnp.sum(jnp.where(onehot, 1.0, 0.0), axis=0, keepdims=True)


ROUTE_ROWS = 128
N_ROUTE_IN = 5


def _router_io(n, dm, tm, gain, w_cat, w_hi, b_r):
    tri = (jnp.arange(ROUTE_ROWS)[:, None] > jnp.arange(ROUTE_ROWS)[None, :]).astype(BF)
    const = lambda shp: pl.BlockSpec(shp, lambda i: (0, 0))
    operands = [gain[None].astype(F32), w_cat, w_hi, b_r, tri]
    in_specs = [const((1, dm)), const(w_cat.shape), const(w_hi.shape), const((1, 128)), const(tri.shape)]
    out_shapes = [jax.ShapeDtypeStruct((n, 128), F32), jax.ShapeDtypeStruct((ROW_ALIGN, n), F32),
                  jax.ShapeDtypeStruct((1, 128), F32)]
    out_specs = [pl.BlockSpec((tm, 128), lambda i: (i, 0)), pl.BlockSpec((ROW_ALIGN, tm), lambda i: (0, i)),
                 const((1, 128))]
    return operands, in_specs, out_shapes, out_specs, [pltpu.VMEM((1, 128), F32)]


def _start_row_dmas(n, copy_of_row):
    for r in range(n):
        copy_of_row(r).start()


def _dispatch_kernel(fpos_ref, fon_ref, pos_ref, x_ref, info_ref, o_hbm, aug_ref, zero_ref, sem, row_sem):
    dm = x_ref.shape[-1]
    tm = info_ref.shape[0]

    @pl.when(pl.program_id(0) == 0)
    def _():
        zero_ref[...] = jnp.zeros_like(zero_ref)
        n_fill = fpos_ref.shape[0]

        def fill(k):
            return pltpu.make_async_copy(
                zero_ref, o_hbm.at[pl.ds(pl.multiple_of(fpos_ref[k], ROW_ALIGN), MOE_FILL), :], sem)

        for k in range(n_fill - 1):
            pl.when(fon_ref[k] > 0)(lambda k=k: fill(k).start())
        for k in range(n_fill - 1):
            pl.when(fon_ref[k] > 0)(lambda k=k: fill(k).wait())
        fill(n_fill - 1).start()
        fill(n_fill - 1).wait()

    i = pl.program_id(0)
    n_steps = pl.num_programs(0)
    slot = i % 2

    def wait_rows(s):
        pltpu.make_async_copy(aug_ref.at[s], o_hbm.at[pl.ds(0, tm), :], row_sem.at[s]).wait()

    pl.when(i >= 2)(lambda: wait_rows(slot))
    aug_ref[slot, :, 0:dm] = x_ref[...].reshape(tm, dm)
    aug_ref[slot, :, dm:] = info_ref[...]
    _start_row_dmas(tm, lambda r: pltpu.make_async_copy(
        aug_ref.at[slot, pl.ds(r, 1), :], o_hbm.at[pl.ds(pos_ref[0, 0, r], 1), :], row_sem.at[slot]))

    @pl.when(i == n_steps - 1)
    def _():
        pl.when(n_steps >= 2)(lambda: wait_rows(1 - slot))
        wait_rows(slot)


def _token_tiles(x3, nb, tq):
    if nb == x3.shape[0]:
        return (lambda *shape_tail: pl.BlockSpec((nb, tq) + shape_tail, lambda i, *_: (0, i) + (0,) * len(shape_tail)),
                x3.shape[1] // tq)
    assert nb == 1 and tq == x3.shape[1]
    return (lambda *shape_tail: pl.BlockSpec((1, tq) + shape_tail, lambda i, *_: (i, 0) + (0,) * len(shape_tail)),
            x3.shape[0])


def _dispatch(x3, nb, tq, info, pos, first_pad, n_rows):
    dm = x3.shape[-1]
    n = info.shape[0]
    tm = nb * tq
    wa = dm + info.shape[1]
    spec, n_steps = _token_tiles(x3, nb, tq)
    class_fill = first_pad // ROW_ALIGN * ROW_ALIGN
    tail = class_fill[-1] + MOE_FILL + jnp.arange((n_rows - n) // MOE_FILL + 1, dtype=jnp.int32) * MOE_FILL
    tail_on = tail + MOE_FILL <= n_rows
    fpos = jnp.concatenate([class_fill, jnp.where(tail_on, tail, 0), jnp.array([n_rows - MOE_FILL], jnp.int32)])
    fon = jnp.concatenate([jnp.ones_like(class_fill), tail_on.astype(jnp.int32), jnp.ones((1,), jnp.int32)])
    return pl.pallas_call(
        _dispatch_kernel,
        out_shape=jax.ShapeDtypeStruct((n_rows, wa), F32),
        grid_spec=pltpu.PrefetchScalarGridSpec(
            num_scalar_prefetch=2,
            grid=(n_steps,),
            in_specs=[pl.BlockSpec((1, 1, tm), lambda i, fp, fo: (i, 0, 0), memory_space=pltpu.SMEM),
                      spec(dm),
                      pl.BlockSpec((tm, info.shape[1]), lambda i, fp, fo: (i, 0))],
            out_specs=pl.BlockSpec(memory_space=pl.ANY),
            scratch_shapes=[pltpu.VMEM((2, tm, wa), F32), pltpu.VMEM((MOE_FILL, wa), F32),
                            pltpu.SemaphoreType.DMA, pltpu.SemaphoreType.DMA((2,))]),
        compiler_params=_params(1, VMEM_LIMIT),
        name="moe_dispatch",
    )(fpos.astype(jnp.int32), fon, pos.reshape(n_steps, 1, tm), x3, info)


EXPERT_TILES = 2


def _expert_kernel(ea_ref, eb_ref, act_ref, xa_ref, g_ref, *refs):
    i = pl.program_id(0)
    o_ref = refs[-1]
    dm = o_ref.shape[1]
    tm = MOE_TILE

    @pl.when(act_ref[i * EXPERT_TILES] > 0)
    def _():
        for t in range(EXPERT_TILES):
            wga, wua, wda, wgb, wub, wdb = refs[6 * t:6 * t + 6]
            rows = slice(t * tm, (t + 1) * tm)
            h = _rms(xa_ref[rows, 0:dm], g_ref[...]).astype(BF)

            def ffn(wg, wu, wd, wrow):
                gate = _dot(h, wg[0])
                hid = gate * _sigmoid(gate) * _dot(h, wu[0])
                return _dot((hid * wrow).astype(BF), wd[0])

            o_ref[rows, :] = (ffn(wga, wua, wda, xa_ref[rows, dm + 1:dm + 2])
                              + ffn(wgb, wub, wdb, xa_ref[rows, dm + 2:dm + 3]))

    @pl.when(act_ref[i * EXPERT_TILES] == 0)
    def _():
        o_ref[...] = jnp.zeros_like(o_ref)


def _experts(xa_sorted, gain, tile_ea, tile_eb, tile_act, w_gate, w_up, w_down):
    dm = w_gate.shape[1]
    wa = xa_sorted.shape[1]
    n_tiles = tile_ea.shape[0]
    assert n_tiles % EXPERT_TILES == 0
    tm = MOE_TILE * EXPERT_TILES
    ff = w_gate.shape[-1]

    def wsel(shp, which, t):
        return pl.BlockSpec((1,) + shp, lambda i, ea, eb, act: ((ea, eb)[which][i * EXPERT_TILES + t], 0, 0))

    w_specs, w_ops = [], []
    for t in range(EXPERT_TILES):
        for which in (0, 1):
            w_specs += [wsel((dm, ff), which, t), wsel((dm, ff), which, t), wsel((ff, dm), which, t)]
            w_ops += [w_gate, w_up, w_down]
    return pl.pallas_call(
        _expert_kernel,
        out_shape=jax.ShapeDtypeStruct((n_tiles * MOE_TILE, dm), F32),
        grid_spec=pltpu.PrefetchScalarGridSpec(
            num_scalar_prefetch=3,
            grid=(n_tiles // EXPERT_TILES,),
            in_specs=[pl.BlockSpec((tm, wa), lambda i, ea, eb, act: (i * act[i * EXPERT_TILES], 0)),
                      pl.BlockSpec((1, dm), lambda i, ea, eb, act: (0, 0))] + w_specs,
            out_specs=pl.BlockSpec((tm, dm), lambda i, ea, eb, act: (i, 0))),
        compiler_params=_params(1, VMEM_LIMIT),
        name="moe_experts",
    )(tile_ea, tile_eb, tile_act, xa_sorted, gain, *w_ops)


def _combine_kernel(pos_ref, nxt_ref, x_ref, y_hbm, o_ref, ybuf, sem):
    i = pl.program_id(0)
    n_steps = pl.num_programs(0)
    tm = ybuf.shape[1]
    slot = i % 2

    def gather(idx_ref, s):
        _start_row_dmas(tm, lambda r: pltpu.make_async_copy(
            y_hbm.at[pl.ds(idx_ref[0, 0, r], 1), :], ybuf.at[s, pl.ds(r, 1), :], sem.at[s]))

    @pl.when(i == 0)
    def _():
        gather(pos_ref, 0)

    @pl.when(i + 1 < n_steps)
    def _():
        gather(nxt_ref, 1 - slot)

    pltpu.make_async_copy(y_hbm.at[pl.ds(0, tm), :], ybuf.at[slot], sem.at[slot]).wait()
    o_ref[...] = x_ref[...] + ybuf[slot].reshape(x_ref.shape)


def _combine(x3, nb, tq, y_sorted, pos):
    dm = x3.shape[-1]
    tm = nb * tq
    spec, n_steps = _token_tiles(x3, nb, tq)
    pos3 = pos.reshape(n_steps, 1, tm)
    return pl.pallas_call(
        _combine_kernel,
        out_shape=jax.ShapeDtypeStruct(x3.shape, F32),
        grid=(n_steps,),
        in_specs=[pl.BlockSpec((1, 1, tm), lambda i: (i, 0, 0), memory_space=pltpu.SMEM),
                  pl.BlockSpec((1, 1, tm), lambda i: (jnp.minimum(i + 1, n_steps - 1), 0, 0), memory_space=pltpu.SMEM),
                  spec(dm),
                  pl.BlockSpec(memory_space=pl.ANY)],
        out_specs=spec(dm),
        scratch_shapes=[pltpu.VMEM((2, tm, dm), F32), pltpu.SemaphoreType.DMA((2,))],
        compiler_params=_params(1, VMEM_LIMIT),
        name="moe_combine",
    )(pos3, pos3, x3, y_sorted)


def _slots_kernel(starts_ref, rows_ref, o_ref):
    cls = rows_ref[0:1, :]
    base = jnp.zeros_like(cls)
    for c in range(MOE_CLASSES):
        base = jnp.where(cls == c, starts_ref[c].astype(F32), base)
    o_ref[...] = (base + rows_ref[3:4, :]).astype(jnp.int32)


def _slots(info_rows, starts, tn=8192):
    n = info_rows.shape[1]
    tn = min(tn, n)
    return pl.pallas_call(
        _slots_kernel,
        out_shape=jax.ShapeDtypeStruct((1, n), jnp.int32),
        grid_spec=pltpu.PrefetchScalarGridSpec(
            num_scalar_prefetch=1,
            grid=(n // tn,),
            in_specs=[pl.BlockSpec((ROW_ALIGN, tn), lambda i, s: (0, i))],
            out_specs=pl.BlockSpec((1, tn), lambda i, s: (0, i))),
        compiler_params=_params(1),
        name="moe_slots",
    )(starts.astype(jnp.int32), info_rows)


def _moe(x3, nb, tq, info, info_rows, cnt, gain, layer, w_gate, w_up, w_down):
    n = info.shape[0]
    tm = MOE_TILE
    gain = gain[None].astype(F32)
    counts = cnt[0, :MOE_CLASSES].astype(jnp.int32)
    padded = ((counts + tm - 1) // tm) * tm
    ends = jnp.cumsum(padded)
    starts = ends - padded
    pos = _slots(info_rows, starts)
    n_tiles = -(-(n // tm + MOE_CLASSES) // EXPERT_TILES) * EXPERT_TILES
    tile_start = jnp.arange(n_tiles, dtype=jnp.int32) * tm
    tile_cls = jnp.minimum(jnp.sum(tile_start[:, None] >= ends[None, :], axis=1), MOE_CLASSES - 1)
    tile_act = (tile_start < ends[-1]).astype(jnp.int32)
    pair_lo = jnp.array([0, 0, 0, 1, 1, 2], jnp.int32)
    pair_hi = jnp.array([1, 2, 3, 2, 3, 3], jnp.int32)
    first = layer * MOE_GROUPS * MOE_PER_GROUP + (tile_cls // MOE_PAIRS) * MOE_PER_GROUP
    tile_ea = (first + pair_lo[tile_cls % MOE_PAIRS]).astype(jnp.int32)
    tile_eb = (first + pair_hi[tile_cls % MOE_PAIRS]).astype(jnp.int32)
    xa_sorted = _dispatch(x3, nb, tq, info, pos, (starts + counts).astype(jnp.int32), (n_tiles + 2) * tm)
    y_sorted = _experts(xa_sorted, gain, tile_ea, tile_eb, tile_act, w_gate, w_up, w_down)
    return _combine(x3, nb, tq, y_sorted, pos)


def _router_weights(w_group, b_group, w_router, b_router):
    dm = w_group.shape[0]
    w = jnp.concatenate([w_group.astype(F32), w_router.astype(F32).reshape(dm, -1)], axis=1)
    b = jnp.concatenate([b_group.astype(F32), b_router.astype(F32).reshape(-1)])
    pad = 128 - w.shape[1]
    w_hi, w_lo = _split_bf16(jnp.pad(w, ((0, 0), (0, pad))))
    return jnp.concatenate([w_hi, w_lo], axis=1), w_hi, jnp.pad(b, (0, pad))[None]


def _trunk(x, p):
    b, t, dm = x.shape
    n = b * t

    tq = _even_tile(b)
    ug, q, k, v = _in_even(x, p["norm_mix"][0], p["w_in_even"], p["na_q_gain"], p["na_k_gain"])
    yg = _s5_scan(ug, *p["s5_ops"], p["s5_d_rows"], t // S5_CHUNK, b)
    att = _na(q, k, v, p["na_table"])
    x3, *routing = _out_even(yg, att, x, p["s5_w_glu"], p["s5_b_glu"], p["w_out_even"],
                             (p["norm_ffn"][0], *p["router"][0]))
    x3 = _moe(x3, b, tq, *routing, p["norm_ffn"][0], 0, *p["experts"])

    x2 = x3.reshape(n, dm)
    qvg, kt = _in_odd(x2, p["norm_mix"][1], p["w_in_odd"], p["w_k_t"], p["ret_norm_gain"], b)
    yr = _retention(qvg, kt, p["log_gamma"], b)
    tm = 1024
    x2, *routing = _out_odd(yr.reshape(n, -1), x2, p["w_out_odd"], (p["norm_ffn"][1], *p["router"][1]))
    x3 = _moe(x2.reshape(n // tm, tm, dm), 1, tm, *routing, p["norm_ffn"][1], 1, *p["experts"])
    return x3.reshape(b, t, dm)


def kernel(x_prompt, x_sample, norm_mix, norm_ffn, w_in_even, w_out_even, s5_lambda_re, s5_lambda_im, s5_log_dt, s5_b_re, s5_b_im, s5_c_re, s5_c_im, s5_d, s5_w_glu, s5_b_glu, na_q_gain, na_k_gain, na_rpb, w_in_odd, w_out_odd, ret_decay_logit, ret_norm_gain, moe_w_group, moe_b_group, moe_w_router, moe_b_router, moe_w_gate, moe_w_up, moe_w_down):
    assert norm_mix.shape[0] == 2
    dk_all = RET_HEADS * RET_DK
    w_odd = w_in_odd[0].astype(BF)
    p = {
        "norm_mix": norm_mix, "norm_ffn": norm_ffn,
        "w_in_even": w_in_even[0].astype(BF), "w_out_even": w_out_even[0].astype(BF),
        "s5_ops": _s5_operators(s5_lambda_re[0], s5_lambda_im[0], s5_log_dt[0], s5_b_re[0], s5_b_im[0],
                                s5_c_re[0], s5_c_im[0]),
        "s5_d_rows": jnp.tile(s5_d[0].astype(F32).reshape(S5_GROUPS, 1, S5_GROUP), (1, 1, S5_CHUNK)),
        "s5_w_glu": s5_w_glu[0].astype(BF), "s5_b_glu": s5_b_glu[0],
        "na_q_gain": na_q_gain[0], "na_k_gain": na_k_gain[0], "na_table": _na_bias_table(na_rpb[0]),
        "w_in_odd": w_odd, "w_k_t": w_in_odd[0][:, dk_all:2 * dk_all].T.astype(BF),
        "w_out_odd": w_out_odd[0].astype(BF),
        "log_gamma": jax.nn.log_sigmoid(ret_decay_logit[0].astype(F32)),
        "ret_norm_gain": ret_norm_gain[0],
        "router": [_router_weights(moe_w_group[l], moe_b_group[l], moe_w_router[l], moe_b_router[l]) for l in range(2)],
        "experts": tuple(w.astype(BF).reshape((-1,) + w.shape[2:]) for w in (moe_w_gate, moe_w_up, moe_w_down)),
    }
    return _trunk(x_prompt, p), _trunk(x_sample, p)
```
